```python
import math
import jax, jax.numpy as jnp
from jax import lax
import numpy as np

D_MODEL = 4096
BATCH = 2
SEQ = 4096
DEPTH = 2

CTX_LEN = 256
GRID_W = 64
EPS = 1e-6

N_EVEN = (DEPTH + 1) // 2
N_ODD = DEPTH // 2

ML_HEADS = 4
ML_QK_DIM = D_MODEL // 16
ML_V_DIM = D_MODEL // 8
ML_QK = ML_HEADS * ML_QK_DIM
ML_V = ML_HEADS * ML_V_DIM
ML_CHUNK = 128
GATE_CAP = 15.0

SSM_HEAD_DIM = 64
SSM_INNER = D_MODEL // 2
SSM_HEADS = SSM_INNER // SSM_HEAD_DIM
SSM_GROUPS = 8
SSM_HPG = SSM_HEADS // SSM_GROUPS
SSM_STATE = 128
SSM_CONV = 5
SSM_CHUNK = 128
SSM_XBC_SIZES = (SSM_INNER, SSM_GROUPS * SSM_STATE, SSM_GROUPS * SSM_STATE)
SSM_CONV_DIM = sum(SSM_XBC_SIZES)

IN0_SIZES = (ML_QK, ML_QK, ML_V, ML_V, 4 * ML_HEADS, SSM_INNER, SSM_CONV_DIM, 2 * SSM_HEADS)
IN0_COLS = sum(IN0_SIZES)
OUT0_ROWS = ML_V + SSM_INNER

HG_EXPAND = 128
HG_HEADS = D_MODEL // HG_EXPAND
HG_F = HG_HEADS * HG_EXPAND
HG_V_DIM = D_MODEL // HG_HEADS
HG_CHUNK = 64
IN1_SIZES = (HG_F, HG_F, HG_F, D_MODEL, D_MODEL)
IN1_COLS = sum(IN1_SIZES)

FFN_HIDDEN = -(-8 * D_MODEL // (3 * 256)) * 256

kernel_name = 'hybrid_mlstm_ssd_hgrn2_dit'


def rms_norm(x, gain):
    x32 = x.astype(jnp.float32)
    y = x32 * lax.rsqrt(jnp.mean(jnp.square(x32), axis=-1, keepdims=True) + EPS)
    return (y * gain.astype(jnp.float32)).astype(x.dtype)


def group_rms(x, n_groups):
    shp = x.shape
    x32 = x.astype(jnp.float32).reshape(shp[:-1] + (n_groups, shp[-1] // n_groups))
    y = x32 * lax.rsqrt(jnp.mean(jnp.square(x32), axis=-1, keepdims=True) + EPS)
    return y.reshape(shp)


def modulated_norm(h, gain, shift, scale):
    return rms_norm(h, gain) * (1.0 + scale) + shift


def split_cols(p, sizes):
    cuts = [int(s) for s in np.cumsum(sizes)[:-1]]
    return jnp.split(p, cuts, axis=-1)


def to_chunks(a, chunk):
    b, t = a.shape[:2]
    a = a.reshape((b, t // chunk, chunk) + a.shape[2:])
    return a.transpose((1, 0, 3, 2) + tuple(range(4, a.ndim)))


def from_chunks(a):
    nc, b, h, l = a.shape[:4]
    a = a.transpose((1, 0, 3, 2) + tuple(range(4, a.ndim)))
    return a.reshape((b, nc * l, h) + a.shape[4:])


def run_direction(scan_fn, ctx_args, lat_args, init_state, reverse):
    if reverse:
        ctx_args = tuple(jnp.flip(a, axis=1) for a in ctx_args)
        lat_args = tuple(jnp.flip(a, axis=1) for a in lat_args)
    y_c, state = scan_fn(*ctx_args, init_state)
    y_l, _ = scan_fn(*lat_args, state)
    if reverse:
        y_c, y_l = jnp.flip(y_c, axis=1), jnp.flip(y_l, axis=1)
    return y_c, y_l


def depthwise_conv(x, w, b):
    k = w.shape[0]
    y = lax.conv_general_dilated(x, w[:, None, :].astype(x.dtype), window_strides=(1,),
                                 padding=[(k // 2, k // 2)],
                                 dimension_numbers=('NWC', 'WIO', 'NWC'),
                                 feature_group_count=x.shape[-1])
    return y + b.astype(y.dtype)


def grid_to_colmajor(h):
    bsz, t, d = h.shape
    rows = t // GRID_W
    return h.reshape(bsz, rows, GRID_W, d).transpose(0, 2, 1, 3).reshape(bsz, t, d)


def grid_from_colmajor(h):
    bsz, t, d = h.shape
    rows = t // GRID_W
    return h.reshape(bsz, GRID_W, rows, d).transpose(0, 2, 1, 3).reshape(bsz, t, d)


def mlstm_scan(q, k, v, ig, lf, state):
    mask = jnp.tril(jnp.ones((ML_CHUNK, ML_CHUNK), dtype=bool))

    def body(carry, inp):
        c_st, n_st, m = carry
        qc, kc, vc, igc, lfc = inp
        b = jnp.cumsum(lfc, axis=-1)
        logw = jnp.where(mask, b[..., :, None] - b[..., None, :] + igc[..., None, :], -jnp.inf)
        inter = b + m[..., None]
        m_t = jnp.maximum(jnp.max(logw, axis=-1), inter)
        s = jnp.einsum('bhtd,bhsd->bhts', qc, kc) * jnp.exp(logw - m_t[..., None])
        inter_w = jnp.exp(inter - m_t)
        num = (jnp.einsum('bhts,bhsv->bhtv', s, vc)
               + inter_w[..., None] * jnp.einsum('bhtd,bhdv->bhtv', qc, c_st))
        den = jnp.sum(s, axis=-1) + inter_w * jnp.einsum('bhtd,bhd->bht', qc, n_st)
        h = num / jnp.maximum(jnp.abs(den), jnp.exp(-m_t))[..., None]
        g = b[..., -1:] - b + igc
        m_new = jnp.maximum(b[..., -1] + m, jnp.max(g, axis=-1))
        w = jnp.exp(g - m_new[..., None])
        decay = jnp.exp(b[..., -1] + m - m_new)
        c_st = decay[..., None, None] * c_st + jnp.einsum('bhs,bhsd,bhsv->bhdv', w, kc, vc)
        n_st = decay[..., None] * n_st + jnp.einsum('bhs,bhsd->bhd', w, kc)
        return (c_st, n_st, m_new), h

    state, h = lax.scan(body, state, tuple(to_chunks(a, ML_CHUNK) for a in (q, k, v, ig, lf)))
    return from_chunks(h), state


def ssd_scan(xdt, a, bm, cm, state):
    mask = jnp.tril(jnp.ones((SSM_CHUNK, SSM_CHUNK), dtype=bool))

    def body(h, inp):
        xc, ac, bc, cc = inp
        bsz = xc.shape[0]
        xc = xc.reshape(bsz, SSM_GROUPS, SSM_HPG, SSM_CHUNK, SSM_HEAD_DIM)
        ac = ac.reshape(bsz, SSM_GROUPS, SSM_HPG, SSM_CHUNK)
        cum = jnp.cumsum(ac, axis=-1)
        decay = jnp.exp(jnp.where(mask, cum[..., :, None] - cum[..., None, :], -jnp.inf))
        cb = jnp.einsum('bgtn,bgsn->bgts', cc, bc)
        y = (jnp.einsum('bgjts,bgts,bgjsp->bgjtp', decay, cb, xc)
             + jnp.exp(cum)[..., None] * jnp.einsum('bgtn,bgjpn->bgjtp', cc, h))
        w_end = jnp.exp(cum[..., -1:] - cum)
        h = (jnp.exp(cum[..., -1])[..., None, None] * h
             + jnp.einsum('bgjs,bgjsp,bgsn->bgjpn', w_end, xc, bc))
        return h, y.reshape(bsz, SSM_HEADS, SSM_CHUNK, SSM_HEAD_DIM)

    state, y = lax.scan(body, state, tuple(to_chunks(t, SSM_CHUNK) for t in (xdt, a, bm, cm)))
    return from_chunks(y), state


def hgrn2_scan(q, k, v, lf, state):
    mask = jnp.tril(jnp.ones((HG_CHUNK, HG_CHUNK), dtype=bool))[:, :, None]

    def body(s, inp):
        qc, kc, vc, lfc = inp
        cum = jnp.cumsum(lfc, axis=2)
        decay = jnp.exp(jnp.where(mask, cum[:, :, :, None, :] - cum[:, :, None, :, :], -jnp.inf))
        att = jnp.einsum('bhte,bhtse,bhse->bhts', qc, decay, kc)
        o = (jnp.einsum('bhts,bhsv->bhtv', att, vc)
             + jnp.einsum('bhte,bhev->bhtv', qc * jnp.exp(cum), s))
        tail = jnp.exp(cum[:, :, -1:, :] - cum)
        s = jnp.exp(cum[:, :, -1, :])[..., None] * s + jnp.einsum('bhse,bhsv->bhev', kc * tail, vc)
        return s, o

    state, o = lax.scan(body, state, tuple(to_chunks(a, HG_CHUNK) for a in (q, k, v, lf)))
    return from_chunks(o), state


def mlstm_ssd_mixer(h_c, h_l, w_in, w_out, ml_gate_b, ml_norm_g, conv_w, conv_b,
                    dt_bias, a_log, d_skip, ssm_norm_g, need_ctx):
    f32 = jnp.float32
    neg_a = -jnp.exp(a_log.astype(f32))

    def features(h):
        bsz, t, _ = h.shape
        q, k, v, o, gates, z, xbc, dt = split_cols(h @ w_in, IN0_SIZES)
        gates = gates.astype(f32).reshape(bsz, t, 2, 2, ML_HEADS) + ml_gate_b.astype(f32)
        gates = GATE_CAP * jnp.tanh(gates / GATE_CAP)
        xbc = jax.nn.silu(depthwise_conv(xbc, conv_w, conv_b))
        xs, bm, cm = split_cols(xbc, SSM_XBC_SIZES)
        return {
            'q': q.reshape(bsz, t, ML_HEADS, ML_QK_DIM) * (ML_QK_DIM ** -0.5),
            'k': k.reshape(bsz, t, ML_HEADS, ML_QK_DIM),
            'v': v.reshape(bsz, t, ML_HEADS, ML_V_DIM),
            'o': o,
            'ig': gates[:, :, :, 0],
            'lf': jax.nn.log_sigmoid(gates[:, :, :, 1]),
            'z': z,
            'xs': xs.reshape(bsz, t, SSM_HEADS, SSM_HEAD_DIM),
            'bm': bm.reshape(bsz, t, SSM_GROUPS, SSM_STATE),
            'cm': cm.reshape(bsz, t, SSM_GROUPS, SSM_STATE),
            'dt': jax.nn.softplus(dt.astype(f32).reshape(bsz, t, 2, SSM_HEADS) + dt_bias.astype(f32)),
        }

    def ml_args(f, d):
        return (f['q'], f['k'], f['v'], f['ig'][:, :, d], f['lf'][:, :, d])

    def ssd_args(f, d):
        dt = f['dt'][:, :, d]
        return (f['xs'] * dt[..., None], dt * neg_a[d], f['bm'], f['cm'])

    fc, fl = features(h_c), features(h_l)
    bsz = h_l.shape[0]
    ml_init = (jnp.zeros((bsz, ML_HEADS, ML_QK_DIM, ML_V_DIM), f32),
               jnp.zeros((bsz, ML_HEADS, ML_QK_DIM), f32),
               jnp.zeros((bsz, ML_HEADS), f32))
    ssd_init = jnp.zeros((bsz, SSM_GROUPS, SSM_HPG, SSM_HEAD_DIM, SSM_STATE), f32)

    ml_c = ml_l = ssd_c = ssd_l = 0.0
    for d in range(2):
        rev = d == 1
        yc, yl = run_direction(mlstm_scan, ml_args(fc, d), ml_args(fl, d), ml_init, rev)
        ml_c, ml_l = ml_c + yc, ml_l + yl
        yc, yl = run_direction(ssd_scan, ssd_args(fc, d), ssd_args(fl, d), ssd_init, rev)
        ssd_c, ssd_l = ssd_c + yc, ssd_l + yl

    def output(f, ml_h, ssd_y):
        b, t = ml_h.shape[:2]
        ml = group_rms(ml_h, 1).reshape(b, t, ML_V) * ml_norm_g * jax.nn.sigmoid(f['o'])
        y = ssd_y + d_skip[:, None] * f['xs']
        y = y.reshape(b, t, SSM_INNER) * jax.nn.silu(f['z'])
        y = group_rms(y, SSM_GROUPS) * ssm_norm_g
        return jnp.concatenate([ml, y], axis=-1).astype(h_l.dtype) @ w_out

    y_ctx = output(fc, ml_c, ssd_c) if need_ctx else None
    return y_ctx, output(fl, ml_l, ssd_l)


def hgrn2_mixer(h_c, h_l, w_in, w_out, lb, norm_g, need_ctx):
    f32 = jnp.float32
    lb = lb.astype(f32)

    def features(h):
        bsz, t, _ = h.shape
        q, f_fw, f_bw, i, g = split_cols(h @ w_in, IN1_SIZES)
        ks, lfs = [], []
        for d, f_raw in enumerate((f_fw, f_bw)):
            f = lb[d] + (1.0 - lb[d]) * jax.nn.sigmoid(f_raw.astype(f32))
            ks.append((1.0 - f).reshape(bsz, t, HG_HEADS, HG_EXPAND))
            lfs.append(jnp.log(f).reshape(bsz, t, HG_HEADS, HG_EXPAND))
        return {
            'q': jax.nn.silu(q).reshape(bsz, t, HG_HEADS, HG_EXPAND) * (HG_EXPAND ** -0.5),
            'i': i.reshape(bsz, t, HG_HEADS, HG_V_DIM),
            'k': ks, 'lf': lfs, 'g': g,
        }

    def hg_args(f, d):
        return (f['q'], f['k'][d], f['i'], f['lf'][d])

    fc, fl = features(h_c), features(h_l)
    bsz = h_l.shape[0]
    init = jnp.zeros((bsz, HG_HEADS, HG_EXPAND, HG_V_DIM), f32)
    o_c = o_l = 0.0
    for d in range(2):
        yc, yl = run_direction(hgrn2_scan, hg_args(fc, d), hg_args(fl, d), init, d == 1)
        o_c, o_l = o_c + yc, o_l + yl

    def output(f, o):
        b, t = o.shape[:2]
        o = group_rms(o, 1).reshape(b, t, D_MODEL) * norm_g * jax.nn.silu(f['g'])
        return o.astype(h_l.dtype) @ w_out

    y_ctx = output(fc, o_c) if need_ctx else None
    return y_ctx, output(fl, o_l)


def swiglu(h, w_gate, w_up, w_down):
    return (jax.nn.silu(h @ w_gate) * (h @ w_up)) @ w_down


def setup_inputs(seed: int = 0) -> dict:
    key = jax.random.key(seed)
    ks = jax.random.split(key, 28)
    f32 = jnp.float32

    def normal(k, shape, scale):
        return jax.random.normal(k, shape, f32) * scale

    def gain(k, shape):
        return 1.0 + 0.02 * jax.random.normal(k, shape, f32)

    ig_b = normal(ks[12], (N_EVEN, 2, 1, ML_HEADS), 0.1)
    fg_b = 3.0 + 3.0 * jax.random.uniform(ks[13], (N_EVEN, 2, 1, ML_HEADS), f32)
    dt0 = jnp.exp(jax.random.uniform(ks[16], (N_EVEN, 2, SSM_HEADS), f32,
                                     math.log(1e-3), math.log(1e-1)))
    return {
        'x': normal(ks[0], (BATCH, SEQ, D_MODEL), 1.0),
        'c': normal(ks[1], (BATCH, D_MODEL), 1.0),
        'ctx': normal(ks[2], (BATCH, CTX_LEN, D_MODEL), 1.0),
        'c_ctx': normal(ks[3], (D_MODEL,), 1.0),
        'ada_w': normal(ks[4], (DEPTH, D_MODEL, 6 * D_MODEL), 0.5 * D_MODEL ** -0.5),
        'ada_b': normal(ks[5], (DEPTH, 6 * D_MODEL), 0.02),
        'norm_g': gain(ks[6], (DEPTH, 4, D_MODEL)),
        'ffn_w_gate': normal(ks[7], (DEPTH, D_MODEL, FFN_HIDDEN), D_MODEL ** -0.5),
        'ffn_w_up': normal(ks[8], (DEPTH, D_MODEL, FFN_HIDDEN), D_MODEL ** -0.5),
        'ffn_w_down': normal(ks[9], (DEPTH, FFN_HIDDEN, D_MODEL), FFN_HIDDEN ** -0.5),
        'ab_w_in': normal(ks[10], (N_EVEN, D_MODEL, IN0_COLS), D_MODEL ** -0.5),
        'ab_w_out': normal(ks[11], (N_EVEN, OUT0_ROWS, D_MODEL), OUT0_ROWS ** -0.5),
        'ml_gate_b': jnp.concatenate([ig_b, fg_b], axis=2),
        'ml_norm_g': gain(ks[14], (N_EVEN, ML_V)),
        'ssm_conv_w': normal(ks[15], (N_EVEN, SSM_CONV, SSM_CONV_DIM), SSM_CONV ** -0.5),
        'ssm_conv_b': normal(ks[17], (N_EVEN, SSM_CONV_DIM), 0.02),
        'ssm_dt_bias': dt0 + jnp.log(-jnp.expm1(-dt0)),
        'ssm_a_log': jnp.log(jax.random.uniform(ks[18], (N_EVEN, 2, SSM_HEADS), f32, 1.0, 16.0)),
        'ssm_d': 1.0 + 0.1 * normal(ks[19], (N_EVEN, SSM_HEADS), 1.0),
        'ssm_norm_g': gain(ks[20], (N_EVEN, SSM_INNER)),
        'hg_w_in': normal(ks[21], (N_ODD, D_MODEL, IN1_COLS), D_MODEL ** -0.5),
        'hg_w_out': normal(ks[22], (N_ODD, D_MODEL, D_MODEL), D_MODEL ** -0.5),
        'hg_lb': normal(ks[23], (2, DEPTH, HG_F), 0.1),
        'hg_norm_g': gain(ks[24], (N_ODD, D_MODEL)),
    }


def reference(x, c, ctx, c_ctx, ada_w, ada_b, norm_g, ffn_w_gate, ffn_w_up, ffn_w_down,
              ab_w_in, ab_w_out, ml_gate_b, ml_norm_g, ssm_conv_w, ssm_conv_b, ssm_dt_bias,
              ssm_a_log, ssm_d, ssm_norm_g, hg_w_in, hg_w_out, hg_lb, hg_norm_g):
    f32 = jnp.float32
    lb_p = jax.nn.softmax(hg_lb.astype(f32), axis=1)
    lower_bounds = jnp.cumsum(lb_p, axis=1) - lb_p[:, :1]
    sc, scc = jax.nn.silu(c), jax.nn.silu(c_ctx)
    h_lat, h_ctx = x, ctx
    for layer in range(DEPTH):
        need_ctx = layer < DEPTH - 1
        j = layer // 2
        mod_l = jnp.split((sc @ ada_w[layer] + ada_b[layer])[:, None, :], 6, axis=-1)
        mod_c = jnp.split(scc @ ada_w[layer] + ada_b[layer], 6, axis=-1)
        g = norm_g[layer]
        a_l = modulated_norm(h_lat, g[0], mod_l[0], mod_l[1])
        a_c = modulated_norm(h_ctx, g[0], mod_c[0], mod_c[1])
        if layer % 2 == 0:
            y_c, y_l = mlstm_ssd_mixer(a_c, a_l, ab_w_in[j], ab_w_out[j], ml_gate_b[j], ml_norm_g[j],
                                       ssm_conv_w[j], ssm_conv_b[j], ssm_dt_bias[j], ssm_a_log[j],
                                       ssm_d[j], ssm_norm_g[j], need_ctx)
        else:
            y_c, y_l = hgrn2_mixer(a_c, grid_to_colmajor(a_l), hg_w_in[j], hg_w_out[j],
                                   lower_bounds[:, layer], hg_norm_g[j], need_ctx)
            y_l = grid_from_colmajor(y_l)
        h_lat = h_lat + mod_l[2] * rms_norm(y_l, g[1])
        f_l = swiglu(modulated_norm(h_lat, g[2], mod_l[3], mod_l[4]),
                     ffn_w_gate[layer], ffn_w_up[layer], ffn_w_down[layer])
        h_lat = h_lat + mod_l[5] * rms_norm(f_l, g[3])
        if need_ctx:
            h_ctx = h_ctx + mod_c[2] * rms_norm(y_c, g[1])
            f_c = swiglu(modulated_norm(h_ctx, g[2], mod_c[3], mod_c[4]),
                         ffn_w_gate[layer], ffn_w_up[layer], ffn_w_down[layer])
            h_ctx = h_ctx + mod_c[5] * rms_norm(f_c, g[3])
    return h_lat
```

```python
import functools
import math

import jax
import jax.numpy as jnp
import numpy as np
from jax import lax
from jax.experimental import pallas as pl
from jax.experimental.pallas import tpu as pltpu

F32 = jnp.float32
BF16 = jnp.bfloat16

D_MODEL = 4096
GRID_W = 64
EPS = 1e-6

ML_HEADS = 4
ML_QK_DIM = 256
ML_V_DIM = 512
ML_QK = ML_HEADS * ML_QK_DIM
ML_V = ML_HEADS * ML_V_DIM
ML_CHUNK = 128
GATE_CAP = 15.0

SSM_HEAD_DIM = 64
SSM_INNER = 2048
SSM_HEADS = 32
SSM_GROUPS = 8
SSM_HPG = 4
SSM_STATE = 128
SSM_CONV = 5
SSM_CHUNK = 128
SSM_GROUP_W = SSM_HPG * SSM_HEAD_DIM
SSM_CONV_DIM = SSM_INNER + 2 * SSM_GROUPS * SSM_STATE

HG_EXPAND = 128
HG_HEADS = 32
HG_F = HG_HEADS * HG_EXPAND
HG_V_DIM = 128
HG_CHUNK = 64
HG_LEVELS = 6
HG_HEADS_PER_STEP = 4

FFN_HIDDEN = 11008
FFN_HIDDEN_PAD = 11264

E_Q, E_K, E_V, E_O, E_Z, E_XBC = 0, 1024, 2048, 4096, 6144, 8192
E_MAIN = 12288
O_Q, O_F, O_I, O_G = 0, 4096, 12288, 16384

VMEM_LIMIT = 56 * 1024 * 1024


def _cparams(sem):
    return pltpu.CompilerParams(dimension_semantics=sem, vmem_limit_bytes=VMEM_LIMIT)


def _mm_kernel(x_ref, w_ref, o_ref):
    x = x_ref[...].astype(BF16)
    w = w_ref[...].astype(BF16)
    o_ref[...] = jnp.dot(x, w, preferred_element_type=F32).astype(o_ref.dtype)


def _mm_bias_kernel(x_ref, w_ref, b_ref, o_ref):
    x = x_ref[...].astype(BF16)
    w = w_ref[...].astype(BF16)
    o_ref[...] = (jnp.dot(x, w, preferred_element_type=F32) + b_ref[...]).astype(o_ref.dtype)


def matmul(x, w, *, bm, bn, out_dtype, bias=None, layer=None):
    m, k = x.shape
    n = w.shape[-1]
    assert m % bm == 0 and n % bn == 0, (m, bm, n, bn)
    if layer is None:
        w_spec = pl.BlockSpec((k, bn), lambda j, i: (0, j))
    else:
        w_spec = pl.BlockSpec((None, k, bn), lambda j, i: (layer, 0, j))
    in_specs = [pl.BlockSpec((bm, k), lambda j, i: (i, 0)), w_spec]
    args = [x, w]
    body = _mm_kernel
    if bias is not None:
        in_specs.append(pl.BlockSpec((1, bn), lambda j, i: (0, j)))
        args.append(bias.reshape(1, n))
        body = _mm_bias_kernel
    return pl.pallas_call(
        body,
        grid=(n // bn, m // bm),
        in_specs=in_specs,
        out_specs=pl.BlockSpec((bm, bn), lambda j, i: (i, j)),
        out_shape=jax.ShapeDtypeStruct((m, n), out_dtype),
        compiler_params=_cparams(("parallel", "parallel")),
    )(*args)


def _swiglu_kernel(x_ref, wg_ref, wu_ref, o_ref):
    x = x_ref[...]
    g = jnp.dot(x, wg_ref[...], preferred_element_type=F32)
    u = jnp.dot(x, wu_ref[...], preferred_element_type=F32)
    o_ref[...] = (g * jax.nn.sigmoid(g) * u).astype(o_ref.dtype)


def swiglu_up(x, wg, wu, *, bm, bn):
    m, k = x.shape
    n = wg.shape[1]
    assert m % bm == 0 and n % bn == 0
    return pl.pallas_call(
        _swiglu_kernel,
        grid=(n // bn, m // bm),
        in_specs=[pl.BlockSpec((bm, k), lambda j, i: (i, 0)),
                  pl.BlockSpec((k, bn), lambda j, i: (0, j)),
                  pl.BlockSpec((k, bn), lambda j, i: (0, j))],
        out_specs=pl.BlockSpec((bm, bn), lambda j, i: (i, j)),
        out_shape=jax.ShapeDtypeStruct((m, n), BF16),
        compiler_params=_cparams(("parallel", "parallel")),
    )(x, wg, wu)


def _rms(x):
    return x * lax.rsqrt(jnp.mean(x * x, axis=-1, keepdims=True) + EPS)


def _resid_norm_kernel(*refs, has_y, want_a):
    it = iter(refs)
    h_ref = next(it)
    if has_y:
        y_ref, gate_ref, g1_ref = next(it), next(it), next(it)
    if want_a:
        g2_ref, shift_ref, scale_ref = next(it), next(it), next(it)
    if has_y:
        hout_ref = next(it)
    if want_a:
        a_ref = next(it)
    h = h_ref[0]
    if has_y:
        h = h + gate_ref[0] * (_rms(y_ref[0]) * g1_ref[...])
        hout_ref[0] = h
    if want_a:
        a = (_rms(h) * g2_ref[...]) * (1.0 + scale_ref[0]) + shift_ref[0]
        a_ref[0] = a.astype(a_ref.dtype)


def resid_norm(h, y=None, gate=None, g1=None, g2=None, shift=None, scale=None, *, bt=128):
    b, t, d = h.shape
    has_y, want_a = y is not None, g2 is not None
    bt = min(bt, t)
    tok = pl.BlockSpec((1, bt, d), lambda i, j: (i, j, 0))
    per_b = pl.BlockSpec((1, 1, d), lambda i, j: (i, 0, 0))
    vec = pl.BlockSpec((1, d), lambda i, j: (0, 0))
    args, in_specs, out_shape, out_specs = [h], [tok], [], []
    if has_y:
        args += [y, gate, g1.reshape(1, d)]
        in_specs += [tok, per_b, vec]
        out_shape.append(jax.ShapeDtypeStruct((b, t, d), F32))
        out_specs.append(tok)
    if want_a:
        args += [g2.reshape(1, d), shift, scale]
        in_specs += [vec, per_b, per_b]
        out_shape.append(jax.ShapeDtypeStruct((b, t, d), BF16))
        out_specs.append(tok)
    outs = pl.pallas_call(
        functools.partial(_resid_norm_kernel, has_y=has_y, want_a=want_a),
        grid=(b, t // bt),
        in_specs=in_specs,
        out_specs=out_specs,
        out_shape=out_shape,
        compiler_params=_cparams(("parallel", "parallel")),
    )(*args)
    outs = list(outs)
    h_new = outs.pop(0) if has_y else h
    a = outs.pop(0) if want_a else None
    return h_new, a


def _softplus(x):
    return jnp.maximum(x, 0.0) + jnp.log1p(jnp.exp(-jnp.abs(x)))


def _log_sigmoid(x):
    return jnp.minimum(x, 0.0) - jnp.log1p(jnp.exp(-jnp.abs(x)))


def _chunk_index(c, nc, reverse):
    return (nc - 1 - c) if reverse else c


def _mlstm_kernel(*refs, reverse, nc, has_prev):
    (q_ref, k_ref, v_ref, gc_ref, gr_ref, c0_ref, n0_ref, m0_ref), rest = refs[:8], refs[8:]
    if has_prev:
        yprev_ref, rest = rest[0], rest[1:]
    y_ref, c_ref, n_ref, m_ref = rest
    l = ML_CHUNK
    step = pl.program_id(2)

    @pl.when(step == 0)
    def _():
        c_ref[...] = c0_ref[...]
        n_ref[...] = n0_ref[...]
        m_ref[...] = m0_ref[...]

    row = lax.broadcasted_iota(jnp.int32, (l, l), 0)
    col = lax.broadcasted_iota(jnp.int32, (l, l), 1)
    mask = (col >= row) if reverse else (col <= row)
    mask_t = (col <= row) if reverse else (col >= row)

    gcol = gc_ref[0, 0]
    grow = gr_ref[0, 0]
    ig_c = GATE_CAP * jnp.tanh(gcol[:, 0:1] / GATE_CAP)
    lf_c = _log_sigmoid(GATE_CAP * jnp.tanh(gcol[:, 1:2] / GATE_CAP))
    ig_r = GATE_CAP * jnp.tanh(grow[0:1, :] / GATE_CAP)
    lf_r = _log_sigmoid(GATE_CAP * jnp.tanh(grow[1:2, :] / GATE_CAP))

    b_c = jnp.sum(jnp.where(mask, lf_r, 0.0), axis=1, keepdims=True)
    b_r = jnp.sum(jnp.where(mask_t, lf_c, 0.0), axis=0, keepdims=True)
    total = jnp.sum(lf_c, axis=0, keepdims=True)

    m_prev = m_ref[0, 0]
    logw = jnp.where(mask, b_c - b_r + ig_r, -jnp.inf)
    inter = b_c + m_prev
    m_t = jnp.maximum(jnp.max(logw, axis=1, keepdims=True), inter)

    q = q_ref[0]
    k = k_ref[0]
    v = v_ref[0]
    scale = ML_QK_DIM ** -0.5
    qk = lax.dot_general(q, k, (((1,), (1,)), ((), ())), preferred_element_type=F32)
    s = qk * scale * jnp.exp(logw - m_t)
    inter_w = jnp.exp(inter - m_t) * scale

    c_st = c_ref[0, 0]
    n_st = n_ref[0, 0]
    num = (jnp.dot(s.astype(BF16), v, preferred_element_type=F32)
           + inter_w * jnp.dot(q, c_st.astype(BF16), preferred_element_type=F32))
    qn = jnp.sum(q.astype(F32) * n_st, axis=1, keepdims=True)
    den = jnp.sum(s, axis=1, keepdims=True) + inter_w * qn
    h = num * (1.0 / jnp.maximum(jnp.abs(den), jnp.exp(-m_t)))
    if has_prev:
        h = h + yprev_ref[0]
    y_ref[0] = h

    g_c = total - b_c + ig_c
    g_r = total - b_r + ig_r
    m_new = jnp.maximum(total + m_prev, jnp.max(g_r, axis=1, keepdims=True))
    w_c = jnp.exp(g_c - m_new)
    decay = jnp.exp(total + m_prev - m_new)
    kw = k.astype(F32) * w_c
    kv = lax.dot_general(kw.astype(BF16), v, (((0,), (0,)), ((), ())), preferred_element_type=F32)
    c_ref[0, 0] = decay * c_st + kv
    n_ref[0, 0] = decay * n_st + jnp.sum(kw, axis=0, keepdims=True)
    m_ref[0, 0] = m_new


def mlstm_scan(proj, gates_col, gates_row, state, y_prev, *, reverse):
    b, t, _ = proj.shape
    l = ML_CHUNK
    nc = t // l
    ci = functools.partial(_chunk_index, nc=nc, reverse=reverse)
    c0, n0, m0 = state
    qb, kb, vb = E_Q // ML_QK_DIM, E_K // ML_QK_DIM, E_V // ML_V_DIM
    in_specs = [
        pl.BlockSpec((1, l, ML_QK_DIM), lambda i, h, c: (i, ci(c), qb + h)),
        pl.BlockSpec((1, l, ML_QK_DIM), lambda i, h, c: (i, ci(c), kb + h)),
        pl.BlockSpec((1, l, ML_V_DIM), lambda i, h, c: (i, ci(c), vb + h)),
        pl.BlockSpec((1, 1, l, 2), lambda i, h, c: (i, h, ci(c), 0)),
        pl.BlockSpec((1, 1, 2, l), lambda i, h, c: (i, h, 0, ci(c))),
        pl.BlockSpec((1, 1, ML_QK_DIM, ML_V_DIM), lambda i, h, c: (i, h, 0, 0)),
        pl.BlockSpec((1, 1, 1, ML_QK_DIM), lambda i, h, c: (i, h, 0, 0)),
        pl.BlockSpec((1, 1, 1, 1), lambda i, h, c: (i, h, 0, 0)),
    ]
    args = [proj, proj, proj, gates_col, gates_row, c0, n0, m0]
    y_spec = pl.BlockSpec((1, l, ML_V_DIM), lambda i, h, c: (i, ci(c), h))
    has_prev = y_prev is not None
    if has_prev:
        in_specs.append(y_spec)
        args.append(y_prev)
    outs = pl.pallas_call(
        functools.partial(_mlstm_kernel, reverse=reverse, nc=nc, has_prev=has_prev),
        grid=(b, ML_HEADS, nc),
        in_specs=in_specs,
        out_specs=[y_spec, in_specs[5], in_specs[6], in_specs[7]],
        out_shape=[jax.ShapeDtypeStruct((b, t, ML_V), F32),
                   jax.ShapeDtypeStruct(c0.shape, F32),
                   jax.ShapeDtypeStruct(n0.shape, F32),
                   jax.ShapeDtypeStruct(m0.shape, F32)],
        compiler_params=_cparams(("parallel", "parallel", "arbitrary")),
    )(*args)
    return outs[0], (outs[1], outs[2], outs[3])


CONV_ROWS = 256
CONV_HALO = 16


def _conv_kernel(x_ref, w_ref, b_ref, o_ref, *, t):
    w = w_ref[...]
    bias = b_ref[...]
    r = min(CONV_ROWS, t)
    half = SSM_CONV // 2
    cw = x_ref.shape[-1]
    for r0 in range(0, t, r):
        parts = []
        if r0 == 0:
            parts.append(jnp.zeros((CONV_HALO, cw), F32))
        else:
            parts.append(x_ref[0, pl.ds(r0 - CONV_HALO, CONV_HALO), :].astype(F32))
        parts.append(x_ref[0, pl.ds(r0, r), :].astype(F32))
        if r0 + r == t:
            parts.append(jnp.zeros((CONV_HALO, cw), F32))
        else:
            parts.append(x_ref[0, pl.ds(r0 + r, CONV_HALO), :].astype(F32))
        win = jnp.concatenate(parts, axis=0)
        n = r + 2 * CONV_HALO
        acc = jnp.zeros((r, cw), F32) + bias
        for kk in range(SSM_CONV):
            sh = (half - kk) % n
            rolled = win if sh == 0 else pltpu.roll(win, sh, 0)
            acc = acc + w[kk:kk + 1, :] * rolled[CONV_HALO:CONV_HALO + r, :]
        o_ref[0, pl.ds(r0, r), :] = (acc * jax.nn.sigmoid(acc)).astype(o_ref.dtype)


def conv_silu(proj, conv_w, conv_b, *, bc=512):
    b, t, _ = proj.shape
    c = SSM_CONV_DIM
    off = E_XBC // bc
    return pl.pallas_call(
        functools.partial(_conv_kernel, t=t),
        grid=(b, c // bc),
        in_specs=[pl.BlockSpec((1, t, bc), lambda i, j: (i, 0, off + j)),
                  pl.BlockSpec((SSM_CONV, bc), lambda i, j: (0, j)),
                  pl.BlockSpec((1, bc), lambda i, j: (0, j))],
        out_specs=pl.BlockSpec((1, t, bc), lambda i, j: (i, 0, j)),
        out_shape=jax.ShapeDtypeStruct((b, t, c), BF16),
        compiler_params=_cparams(("parallel", "parallel")),
    )(proj, conv_w, conv_b.reshape(1, c))


def _expand_heads(x_col):
    l = x_col.shape[0]
    lane_head = lax.broadcasted_iota(jnp.int32, (l, SSM_GROUP_W), 1) // SSM_HEAD_DIM
    out = jnp.zeros((l, SSM_GROUP_W), F32)
    for j in range(SSM_HPG):
        out = jnp.where(lane_head == j, x_col[:, j:j + 1], out)
    return out


def _ssd_kernel(*refs, reverse, has_prev):
    (xs_ref, bm_ref, cm_ref, dtc_ref, dtr_ref, pc_ref, pr_ref, h0_ref), rest = refs[:8], refs[8:]
    if has_prev:
        yprev_ref, rest = rest[0], rest[1:]
    y_ref, h_ref = rest
    l = SSM_CHUNK
    step = pl.program_id(2)

    @pl.when(step == 0)
    def _():
        h_ref[...] = h0_ref[...]

    row = lax.broadcasted_iota(jnp.int32, (l, l), 0)
    col = lax.broadcasted_iota(jnp.int32, (l, l), 1)
    mask = (col >= row) if reverse else (col <= row)
    mask_t = (col <= row) if reverse else (col >= row)

    pc = pc_ref[0, 0]
    pr = pr_ref[0, 0]
    dt_c = _softplus(dtc_ref[0, 0] + pc[0:1, :])
    dt_r = _softplus(dtr_ref[0, 0] + pr[:, 0:1])
    a_c = dt_c * pc[1:2, :]
    a_r = dt_r * pr[:, 1:2]
    total_c = jnp.sum(a_c, axis=0, keepdims=True)

    xs = xs_ref[0].astype(F32)
    bm = bm_ref[0]
    cm = cm_ref[0]
    xdt = xs * _expand_heads(dt_c)
    xdt_b = xdt.astype(BF16)
    cb = lax.dot_general(cm, bm, (((1,), (1,)), ((), ())), preferred_element_type=F32)
    hst = h_ref[0, 0]
    inter = jnp.dot(cm, hst.astype(BF16), preferred_element_type=F32)

    lane_head = lax.broadcasted_iota(jnp.int32, (l, SSM_GROUP_W), 1) // SSM_HEAD_DIM
    y = jnp.zeros((l, SSM_GROUP_W), F32)
    cum_cols = []
    for j in range(SSM_HPG):
        cum_c = jnp.sum(jnp.where(mask, a_r[j:j + 1, :], 0.0), axis=1, keepdims=True)
        cum_r = jnp.sum(jnp.where(mask_t, a_c[:, j:j + 1], 0.0), axis=0, keepdims=True)
        cum_cols.append(cum_c)
        decay = jnp.exp(jnp.where(mask, cum_c - cum_r, -jnp.inf))
        wj = (decay * cb).astype(BF16)
        xj = jnp.where(lane_head == j, xdt, 0.0).astype(BF16)
        y = y + jnp.dot(wj, xj, preferred_element_type=F32)
    cum_all = jnp.concatenate(cum_cols, axis=1)
    y = y + jnp.exp(_expand_heads(cum_all)) * inter
    if has_prev:
        y = y + yprev_ref[0]
    y_ref[0] = y

    w_end = jnp.exp(total_c - cum_all)
    xw = (xdt * _expand_heads(w_end)).astype(BF16)
    upd = lax.dot_general(bm, xw, (((0,), (0,)), ((), ())), preferred_element_type=F32)
    h_ref[0, 0] = jnp.exp(_expand_heads(total_c)) * hst + upd


def ssd_scan(xbc, dt_col, dt_row, par_col, par_row, state, y_prev, *, reverse):
    b, t, _ = xbc.shape
    l = SSM_CHUNK
    nc = t // l
    ci = functools.partial(_chunk_index, nc=nc, reverse=reverse)
    bmb = SSM_INNER // SSM_STATE
    cmb = bmb + SSM_GROUPS
    in_specs = [
        pl.BlockSpec((1, l, SSM_GROUP_W), lambda i, g, c: (i, ci(c), g)),
        pl.BlockSpec((1, l, SSM_STATE), lambda i, g, c: (i, ci(c), bmb + g)),
        pl.BlockSpec((1, l, SSM_STATE), lambda i, g, c: (i, ci(c), cmb + g)),
        pl.BlockSpec((1, 1, l, SSM_HPG), lambda i, g, c: (i, g, ci(c), 0)),
        pl.BlockSpec((1, 1, SSM_HPG, l), lambda i, g, c: (i, g, 0, ci(c))),
        pl.BlockSpec((1, 1, 2, SSM_HPG), lambda i, g, c: (0, g, 0, 0)),
        pl.BlockSpec((1, 1, SSM_HPG, 2), lambda i, g, c: (0, g, 0, 0)),
        pl.BlockSpec((1, 1, SSM_STATE, SSM_GROUP_W), lambda i, g, c: (i, g, 0, 0)),
    ]
    args = [xbc, xbc, xbc, dt_col, dt_row, par_col, par_row, state]
    y_spec = pl.BlockSpec((1, l, SSM_GROUP_W), lambda i, g, c: (i, ci(c), g))
    has_prev = y_prev is not None
    if has_prev:
        in_specs.append(y_spec)
        args.append(y_prev)
    y, h = pl.pallas_call(
        functools.partial(_ssd_kernel, reverse=reverse, has_prev=has_prev),
        grid=(b, SSM_GROUPS, nc),
        in_specs=in_specs,
        out_specs=[y_spec, in_specs[7]],
        out_shape=[jax.ShapeDtypeStruct((b, t, SSM_INNER), F32),
                   jax.ShapeDtypeStruct(state.shape, F32)],
        compiler_params=_cparams(("parallel", "parallel", "arbitrary")),
    )(*args)
    return y, h


def _even_out_kernel(ml_ref, ssd_ref, o_ref, z_ref, xs_ref, mlg_ref, dsk_ref, ssg_ref, out_ref):
    for h in range(ML_HEADS):
        sl = slice(h * ML_V_DIM, (h + 1) * ML_V_DIM)
        y = ml_ref[0, :, sl]
        og = o_ref[0, :, sl].astype(F32)
        out_ref[0, :, sl] = (_rms(y) * mlg_ref[:, sl] * jax.nn.sigmoid(og)).astype(out_ref.dtype)
    for g in range(SSM_GROUPS):
        sl = slice(g * SSM_GROUP_W, (g + 1) * SSM_GROUP_W)
        z = z_ref[0, :, sl].astype(F32)
        y = ssd_ref[0, :, sl] + dsk_ref[:, sl] * xs_ref[0, :, sl].astype(F32)
        y = y * (z * jax.nn.sigmoid(z))
        so = slice(ML_V + g * SSM_GROUP_W, ML_V + (g + 1) * SSM_GROUP_W)
        out_ref[0, :, so] = (_rms(y) * ssg_ref[:, sl]).astype(out_ref.dtype)


def even_out(ml_y, ssd_y, proj, xbc, ml_norm_g, d_skip_full, ssm_norm_g, *, bt=128):
    b, t, _ = ml_y.shape
    bt = min(bt, t)
    w = ML_V
    tok = lambda blk: pl.BlockSpec((1, bt, w), lambda i, j: (i, j, blk))
    vec = pl.BlockSpec((1, w), lambda i, j: (0, 0))
    return pl.pallas_call(
        _even_out_kernel,
        grid=(b, t // bt),
        in_specs=[tok(0), tok(0), tok(E_O // w), tok(E_Z // w), tok(0), vec, vec, vec],
        out_specs=pl.BlockSpec((1, bt, 2 * w), lambda i, j: (i, j, 0)),
        out_shape=jax.ShapeDtypeStruct((b, t, 2 * w), BF16),
        compiler_params=_cparams(("parallel", "parallel")),
    )(ml_y, ssd_y, proj, proj, xbc, ml_norm_g.reshape(1, w), d_skip_full.reshape(1, w),
      ssm_norm_g.reshape(1, w))


def _hg_constants(reverse):
    l = HG_CHUNK
    pos = np.arange(l)
    mats = [(pos[None, :] <= pos[:, None]).astype(np.float32)]
    masks = []
    for lv in range(HG_LEVELS):
        m = 1 << lv
        blk = pos // (2 * m)
        later = (pos % (2 * m)) >= m
        mid = blk * 2 * m + m - 1
        mat = np.zeros((l, l), np.float32)
        for p in range(l):
            if later[p]:
                mat[p, mid[p] + 1:p + 1] = 1.0
            else:
                mat[p, p + 1:mid[p] + 1] = 1.0
        mats.append(mat)
        masks.append(((blk[:, None] == blk[None, :]) & later[:, None] & (~later)[None, :]).astype(np.float32))
    mats = np.stack(mats)
    masks = np.stack(masks)
    if reverse:
        mats = mats[:, ::-1, ::-1]
        masks = masks[:, ::-1, ::-1]
    return (jnp.asarray(mats.reshape(-1, l), BF16), jnp.asarray(masks.reshape(-1, l), F32))


def _split3(x):
    hi = x.astype(BF16)
    r1 = x - hi.astype(F32)
    mid = r1.astype(BF16)
    lo = (r1 - mid.astype(F32)).astype(BF16)
    return hi, mid, lo


def _hg_kernel(*refs, reverse, n_inner, has_prev):
    (q_ref, f_ref, i_ref, lb_ref, mats_ref, masks_ref, s0_ref), rest = refs[:7], refs[7:]
    if has_prev:
        yprev_ref, rest = rest[0], rest[1:]
    y_ref, s_ref = rest
    l = HG_CHUNK
    e = HG_EXPAND
    step = pl.program_id(2)

    @pl.when(step == 0)
    def _():
        s_ref[...] = s0_ref[...]

    row = lax.broadcasted_iota(jnp.int32, (l, l), 0)
    col = lax.broadcasted_iota(jnp.int32, (l, l), 1)
    eye = row == col
    mats = mats_ref[...]
    end_row = 0 if reverse else l - 1

    def chunk(ci, carry):
        cidx = (n_inner - 1 - ci) if reverse else ci
        r0 = pl.multiple_of(cidx * l, l)
        for hh in range(HG_HEADS_PER_STEP):
            sl = slice(hh * e, (hh + 1) * e)
            lb = lb_ref[0, :, sl]
            q_raw = q_ref[0, pl.ds(r0, l), sl].astype(F32)
            f_raw = f_ref[0, pl.ds(r0, l), sl].astype(F32)
            v = i_ref[0, pl.ds(r0, l), sl]
            f = lb + (1.0 - lb) * jax.nn.sigmoid(f_raw)
            k = 1.0 - f
            lf = jnp.log(f)
            q = q_raw * jax.nn.sigmoid(q_raw) * (e ** -0.5)
            hi, mid, lo = _split3(lf)
            x = (jnp.dot(mats, hi, preferred_element_type=F32)
                 + jnp.dot(mats, mid, preferred_element_type=F32)
                 + jnp.dot(mats, lo, preferred_element_type=F32))
            cum = x[0:l]
            att = jnp.where(eye, jnp.sum(q * k, axis=1, keepdims=True), 0.0)
            for lv in range(HG_LEVELS):
                ex = jnp.exp(x[(lv + 1) * l:(lv + 2) * l])
                qt = (q * ex).astype(BF16)
                kt = (k * ex).astype(BF16)
                a_lv = lax.dot_general(qt, kt, (((1,), (1,)), ((), ())), preferred_element_type=F32)
                att = att + masks_ref[pl.ds(lv * l, l), :] * a_lv
            st = s_ref[0, hh]
            qc = (q * jnp.exp(cum)).astype(BF16)
            o = (jnp.dot(att.astype(BF16), v, preferred_element_type=F32)
                 + lax.dot_general(qc, st.astype(BF16), (((1,), (1,)), ((), ())),
                                   preferred_element_type=F32))
            if has_prev:
                o = o + yprev_ref[0, pl.ds(r0, l), sl]
            y_ref[0, pl.ds(r0, l), sl] = o
            cum_end = cum[end_row:end_row + 1, :]
            kt_end = (k * jnp.exp(cum_end - cum)).astype(BF16)
            upd = lax.dot_general(v, kt_end, (((0,), (0,)), ((), ())), preferred_element_type=F32)
            s_ref[0, hh] = jnp.exp(cum_end) * st + upd
        return carry

    lax.fori_loop(0, n_inner, chunk, 0)


def hgrn2_scan(proj, lb, state, y_prev, *, d, tb=512):
    b, t, _ = proj.shape
    reverse = d == 1
    tb = min(tb, t)
    nb = t // tb
    n_inner = tb // HG_CHUNK
    hw = HG_HEADS_PER_STEP * HG_EXPAND
    nh = HG_HEADS // HG_HEADS_PER_STEP
    ci = functools.partial(_chunk_index, nc=nb, reverse=reverse)
    fb = (O_F + d * HG_F) // hw
    ib = O_I // hw
    mats, masks = _hg_constants(reverse)
    in_specs = [
        pl.BlockSpec((1, tb, hw), lambda i, h, c: (i, ci(c), h)),
        pl.BlockSpec((1, tb, hw), lambda i, h, c: (i, ci(c), fb + h)),
        pl.BlockSpec((1, tb, hw), lambda i, h, c: (i, ci(c), ib + h)),
        pl.BlockSpec((1, 1, hw), lambda i, h, c: (d, 0, h)),
        pl.BlockSpec(mats.shape, lambda i, h, c: (0, 0)),
        pl.BlockSpec(masks.shape, lambda i, h, c: (0, 0)),
        pl.BlockSpec((1, HG_HEADS_PER_STEP, HG_V_DIM, HG_EXPAND), lambda i, h, c: (i, h, 0, 0)),
    ]
    args = [proj, proj, proj, lb, mats, masks, state]
    y_spec = pl.BlockSpec((1, tb, hw), lambda i, h, c: (i, ci(c), h))
    has_prev = y_prev is not None
    if has_prev:
        in_specs.append(y_spec)
        args.append(y_prev)
    y, s = pl.pallas_call(
        functools.partial(_hg_kernel, reverse=reverse, n_inner=n_inner, has_prev=has_prev),
        grid=(b, nh, nb),
        in_specs=in_specs,
        out_specs=[y_spec, in_specs[6]],
        out_shape=[jax.ShapeDtypeStruct((b, t, D_MODEL), F32),
                   jax.ShapeDtypeStruct(state.shape, F32)],
        compiler_params=_cparams(("parallel", "parallel", "arbitrary")),
    )(*args)
    return y, s


def _hg_out_kernel(o_ref, g_ref, ng_ref, out_ref):
    for h in range(o_ref.shape[-1] // HG_V_DIM):
        sl = slice(h * HG_V_DIM, (h + 1) * HG_V_DIM)
        g = g_ref[0, :, sl].astype(F32)
        out_ref[0, :, sl] = (_rms(o_ref[0, :, sl]) * ng_ref[:, sl] * (g * jax.nn.sigmoid(g))).astype(out_ref.dtype)


def hg_out(o, proj, norm_g, *, bt=256, bc=1024):
    b, t, d = o.shape
    bt = min(bt, t)
    gb = O_G // bc
    return pl.pallas_call(
        _hg_out_kernel,
        grid=(b, t // bt, d // bc),
        in_specs=[pl.BlockSpec((1, bt, bc), lambda i, j, c: (i, j, c)),
                  pl.BlockSpec((1, bt, bc), lambda i, j, c: (i, j, gb + c)),
                  pl.BlockSpec((1, bc), lambda i, j, c: (0, c))],
        out_specs=pl.BlockSpec((1, bt, bc), lambda i, j, c: (i, j, c)),
        out_shape=jax.ShapeDtypeStruct((b, t, d), BF16),
        compiler_params=_cparams(("parallel", "parallel", "parallel")),
    )(o, proj, norm_g.reshape(1, d))


def _run_both(scan, args_c, args_l, init):
    y_c = y_l = None
    for d in range(2):
        y_c, st = scan(*args_c, init, y_c, d)
        y_l, _ = scan(*args_l, st, y_l, d)
    return y_c, y_l


def even_mixer(a_c, a_l, w_main, w_small, w_out, ml_gate_b, ml_norm_g, conv_w, conv_b,
               dt_bias, a_log, d_skip, ssm_norm_g, need_ctx):
    bsz = a_l.shape[0]

    def features(a):
        b, t, d = a.shape
        bm = min(1024, b * t)
        a2 = a.reshape(b * t, d)
        proj = matmul(a2, w_main, bm=bm, bn=1024, out_dtype=BF16).reshape(b, t, E_MAIN)
        small = matmul(a2, w_small, bm=bm, bn=128, out_dtype=F32).reshape(b, t, 128)
        gates = small[..., :16].reshape(b, t, 2, 2, ML_HEADS) + ml_gate_b.astype(F32)
        g_col = gates.transpose(2, 0, 4, 1, 3)
        g_row = gates.transpose(2, 0, 4, 3, 1)
        dt = small[..., 16:16 + 2 * SSM_HEADS].reshape(b, t, 2, SSM_GROUPS, SSM_HPG)
        dt_col = dt.transpose(2, 0, 3, 1, 4)
        dt_row = dt.transpose(2, 0, 3, 4, 1)
        xbc = conv_silu(proj, conv_w, conv_b)
        return dict(proj=proj, xbc=xbc, g_col=g_col, g_row=g_row, dt_col=dt_col, dt_row=dt_row)

    fc, fl = features(a_c), features(a_l)
    neg_a = -jnp.exp(a_log.astype(F32))
    par = jnp.stack([dt_bias.astype(F32), neg_a], axis=1).reshape(2, 2, SSM_GROUPS, SSM_HPG)
    par_col = par.transpose(0, 2, 1, 3)[:, None]
    par_row = par.transpose(0, 2, 3, 1)[:, None]

    ml_init = (jnp.zeros((bsz, ML_HEADS, ML_QK_DIM, ML_V_DIM), F32),
               jnp.zeros((bsz, ML_HEADS, 1, ML_QK_DIM), F32),
               jnp.zeros((bsz, ML_HEADS, 1, 1), F32))
    ssd_init = jnp.zeros((bsz, SSM_GROUPS, SSM_STATE, SSM_GROUP_W), F32)

    def ml(f, st, y_prev, d):
        return mlstm_scan(f['proj'], f['g_col'][d], f['g_row'][d], st, y_prev, reverse=d == 1)

    def ssd(f, st, y_prev, d):
        return ssd_scan(f['xbc'], f['dt_col'][d], f['dt_row'][d], par_col[d], par_row[d], st, y_prev,
                        reverse=d == 1)

    ml_c, ml_l = _run_both(ml, (fc,), (fl,), ml_init)
    ssd_c, ssd_l = _run_both(ssd, (fc,), (fl,), ssd_init)
    d_full = jnp.repeat(d_skip.astype(F32), SSM_HEAD_DIM)

    def output(f, ml_y, ssd_y):
        b, t, _ = ml_y.shape
        cat = even_out(ml_y, ssd_y, f['proj'], f['xbc'], ml_norm_g, d_full, ssm_norm_g)
        return matmul(cat.reshape(b * t, D_MODEL), w_out, bm=min(1024, b * t), bn=1024,
                      out_dtype=F32).reshape(b, t, D_MODEL)

    y_ctx = output(fc, ml_c, ssd_c) if need_ctx else None
    return y_ctx, output(fl, ml_l, ssd_l)


def hgrn2_mixer(a_c, a_l, w_in, w_out, lb, norm_g, need_ctx, lat_colmajor=True):
    bsz = a_l.shape[0]
    lb3 = lb.astype(F32).reshape(2, 1, HG_F)

    def features(a):
        b, t, d = a.shape
        return matmul(a.reshape(b * t, d), w_in, bm=min(1024, b * t), bn=1024,
                      out_dtype=BF16).reshape(b, t, 5 * D_MODEL)

    pc, pl_ = features(a_c), features(a_l)
    init = jnp.zeros((bsz, HG_HEADS, HG_V_DIM, HG_EXPAND), F32)

    def hg(p, st, y_prev, d):
        return hgrn2_scan(p, lb3, st, y_prev, d=d)

    o_c, o_l = _run_both(hg, (pc,), (pl_,), init)

    def output(p, o, colmajor):
        b, t, _ = o.shape
        gated = hg_out(o, p, norm_g)
        if colmajor:
            gated = grid_from_colmajor(gated)
        return matmul(gated.reshape(b * t, D_MODEL), w_out, bm=min(1024, b * t), bn=1024,
                      out_dtype=F32).reshape(b, t, D_MODEL)

    y_ctx = output(pc, o_c, False) if need_ctx else None
    return y_ctx, output(pl_, o_l, lat_colmajor)


def ffn(a, wg, wu, wd):
    b, t, d = a.shape
    m = b * t
    hid = swiglu_up(a.reshape(m, d), wg, wu, bm=min(1024, m), bn=512)
    return matmul(hid, wd, bm=512, bn=512, out_dtype=F32).reshape(b, t, d)


def grid_to_colmajor(h):
    bsz, t, d = h.shape
    rows = t // GRID_W
    return h.reshape(bsz, rows, GRID_W, d).transpose(0, 2, 1, 3).reshape(bsz, t, d)


def grid_from_colmajor(h):
    bsz, t, d = h.shape
    rows = t // GRID_W
    return h.reshape(bsz, GRID_W, rows, d).transpose(0, 2, 1, 3).reshape(bsz, t, d)


def kernel(x, c, ctx, c_ctx, ada_w, ada_b, norm_g, ffn_w_gate, ffn_w_up, ffn_w_down,
           ab_w_in, ab_w_out, ml_gate_b, ml_norm_g, ssm_conv_w, ssm_conv_b, ssm_dt_bias,
           ssm_a_log, ssm_d, ssm_norm_g, hg_w_in, hg_w_out, hg_lb, hg_norm_g):
    depth = ada_w.shape[0]
    bsz = x.shape[0]
    d = D_MODEL
    lb_p = jax.nn.softmax(hg_lb.astype(F32), axis=1)
    lower_bounds = jnp.cumsum(lb_p, axis=1) - lb_p[:, :1]
    cond = jnp.concatenate([jax.nn.silu(c), jax.nn.silu(c_ctx)[None, :],
                            jnp.zeros((8 - bsz - 1, d), F32)], axis=0)
    h_lat, h_ctx = x, ctx
    y_l = y_c = f_l = f_c = None
    gate_l = gate_c = g_prev = None
    pad_h = FFN_HIDDEN_PAD - FFN_HIDDEN
    for layer in range(depth):
        need_ctx = layer < depth - 1
        j = layer // 2
        mod = matmul(cond, ada_w, bm=8, bn=512, out_dtype=F32, bias=ada_b[layer], layer=layer)
        mod_l = [mod[:bsz, i * d:(i + 1) * d][:, None, :] for i in range(6)]
        mod_c = [jnp.broadcast_to(mod[bsz, i * d:(i + 1) * d][None, None, :], (bsz, 1, d)) for i in range(6)]
        g = norm_g[layer]
        h_lat, a_l = resid_norm(h_lat, f_l, gate_l, g_prev, g[0], mod_l[0], mod_l[1])
        h_ctx, a_c = resid_norm(h_ctx, f_c, gate_c, g_prev, g[0], mod_c[0], mod_c[1])
        if layer % 2 == 0:
            w_in = ab_w_in[j]
            w_main = jnp.concatenate(
                [w_in[:, :6144], w_in[:, 6160:6160 + 2048 + SSM_CONV_DIM]], axis=1).astype(BF16)
            w_small = jnp.concatenate(
                [w_in[:, 6144:6160], w_in[:, 12304:12368], jnp.zeros((d, 128 - 80), F32)], axis=1).astype(BF16)
            y_c, y_l = even_mixer(a_c, a_l, w_main, w_small, ab_w_out[j].astype(BF16), ml_gate_b[j],
                                  ml_norm_g[j], ssm_conv_w[j], ssm_conv_b[j], ssm_dt_bias[j],
                                  ssm_a_log[j], ssm_d[j], ssm_norm_g[j], need_ctx)
        else:
            y_c, y_l = hgrn2_mixer(a_c, grid_to_colmajor(a_l), hg_w_in[j].astype(BF16),
                                   hg_w_out[j].astype(BF16), lower_bounds[:, layer], hg_norm_g[j], need_ctx)
        wg = jnp.pad(ffn_w_gate[layer], ((0, 0), (0, pad_h))).astype(BF16)
        wu = jnp.pad(ffn_w_up[layer], ((0, 0), (0, pad_h))).astype(BF16)
        wd = jnp.pad(ffn_w_down[layer], ((0, pad_h), (0, 0))).astype(BF16)
        h_lat, a2_l = resid_norm(h_lat, y_l, mod_l[2], g[1], g[2], mod_l[3], mod_l[4])
        f_l = ffn(a2_l, wg, wu, wd)
        gate_l = mod_l[5]
        if need_ctx:
            h_ctx, a2_c = resid_norm(h_ctx, y_c, mod_c[2], g[1], g[2], mod_c[3], mod_c[4])
            f_c = ffn(a2_c, wg, wu, wd)
            gate_c = mod_c[5]
        else:
            f_c = gate_c = None
        g_prev = g[3]
    h_lat, _ = resid_norm(h_lat, f_l, gate_l, g_prev)
    return h_lat
```

```python
import functools
import math

import jax
import jax.numpy as jnp
import numpy as np
from jax import lax
from jax.experimental import pallas as pl
from jax.experimental.pallas import tpu as pltpu

F32 = jnp.float32
BF16 = jnp.bfloat16

D_MODEL = 4096
GRID_W = 64
EPS = 1e-6

ML_HEADS = 4
ML_QK_DIM = 256
ML_V_DIM = 512
ML_QK = ML_HEADS * ML_QK_DIM
ML_V = ML_HEADS * ML_V_DIM
ML_CHUNK = 128
GATE_CAP = 15.0

SSM_HEAD_DIM = 64
SSM_INNER = 2048
SSM_HEADS = 32
SSM_GROUPS = 8
SSM_HPG = 4
SSM_STATE = 128
SSM_CONV = 5
SSM_CHUNK = 128
SSM_GROUP_W = SSM_HPG * SSM_HEAD_DIM
SSM_CONV_DIM = SSM_INNER + 2 * SSM_GROUPS * SSM_STATE

HG_EXPAND = 128
HG_HEADS = 32
HG_F = HG_HEADS * HG_EXPAND
HG_V_DIM = 128
HG_CHUNK = 64
HG_LEVELS = 6
HG_HEADS_PER_STEP = 4

FFN_HIDDEN = 11008

E_Q, E_K, E_V, E_O, E_Z, E_XBC = 0, 1024, 2048, 4096, 6144, 8192
E_MAIN = 12288
O_Q, O_F, O_I, O_G = 0, 4096, 12288, 16384

VMEM_LIMIT = 56 * 1024 * 1024


def _cparams(sem):
    return pltpu.CompilerParams(dimension_semantics=sem, vmem_limit_bytes=VMEM_LIMIT)


def _mm_kernel(*refs, has_bias, cast_w):
    x_ref, w_ref = refs[0], refs[1]
    b_ref = refs[2] if has_bias else None
    o_ref = refs[2 + has_bias]
    if cast_w:
        wb_ref = refs[3 + has_bias]

        @pl.when(pl.program_id(1) == 0)
        def _():
            wb_ref[...] = w_ref[...].astype(BF16)

        w = wb_ref[...]
    else:
        w = w_ref[...]
    acc = jnp.dot(x_ref[...].astype(BF16), w, preferred_element_type=F32)
    if has_bias:
        acc = acc + b_ref[...]
    o_ref[...] = acc.astype(o_ref.dtype)


def matmul(x, w, *, bm, bn, out_dtype, name, bias=None, layer=None):
    m, k = x.shape
    n = w.shape[-1]
    assert m % bm == 0 and n % bn == 0, (m, bm, n, bn)
    if layer is None:
        w_spec = pl.BlockSpec((k, bn), lambda j, i: (0, j))
    else:
        w_spec = pl.BlockSpec((None, k, bn), lambda j, i: (layer, 0, j))
    in_specs = [pl.BlockSpec((bm, k), lambda j, i: (i, 0)), w_spec]
    args = [x, w]
    if bias is not None:
        in_specs.append(pl.BlockSpec((1, bn), lambda j, i: (0, j)))
        args.append(bias.reshape(1, n))
    cast_w = w.dtype != BF16
    return pl.pallas_call(
        functools.partial(_mm_kernel, has_bias=bias is not None, cast_w=cast_w),
        grid=(n // bn, m // bm),
        in_specs=in_specs,
        out_specs=pl.BlockSpec((bm, bn), lambda j, i: (i, j)),
        out_shape=jax.ShapeDtypeStruct((m, n), out_dtype),
        scratch_shapes=[pltpu.VMEM((k, bn), BF16)] if cast_w else [],
        compiler_params=_cparams(("parallel", "arbitrary" if cast_w else "parallel")),
        name=name,
    )(*args)


def _swiglu_kernel(x_ref, wg_ref, wu_ref, o_ref, wgb_ref, wub_ref):
    @pl.when(pl.program_id(1) == 0)
    def _():
        wgb_ref[...] = wg_ref[...].astype(BF16)
        wub_ref[...] = wu_ref[...].astype(BF16)

    x = x_ref[...]
    g = jnp.dot(x, wgb_ref[...], preferred_element_type=F32)
    u = jnp.dot(x, wub_ref[...], preferred_element_type=F32)
    o_ref[...] = (g * jax.nn.sigmoid(g) * u).astype(o_ref.dtype)


def swiglu_up(x, wg, wu, layer, *, bm, bn, name):
    m, k = x.shape
    n = wg.shape[-1]
    assert m % bm == 0 and n % bn == 0
    w_spec = pl.BlockSpec((None, k, bn), lambda j, i: (layer, 0, j))
    return pl.pallas_call(
        _swiglu_kernel,
        grid=(n // bn, m // bm),
        in_specs=[pl.BlockSpec((bm, k), lambda j, i: (i, 0)), w_spec, w_spec],
        out_specs=pl.BlockSpec((bm, bn), lambda j, i: (i, j)),
        out_shape=jax.ShapeDtypeStruct((m, n), BF16),
        scratch_shapes=[pltpu.VMEM((k, bn), BF16), pltpu.VMEM((k, bn), BF16)],
        compiler_params=_cparams(("parallel", "arbitrary")),
        name=name,
    )(x, wg, wu)


def _rms(x):
    return x * lax.rsqrt(jnp.mean(x * x, axis=-1, keepdims=True) + EPS)


def _resid_norm_kernel(*refs, has_y, want_a):
    it = iter(refs)
    h_ref = next(it)
    if has_y:
        y_ref, gate_ref, g1_ref = next(it), next(it), next(it)
    if want_a:
        g2_ref, shift_ref, scale_ref = next(it), next(it), next(it)
    if has_y:
        hout_ref = next(it)
    if want_a:
        a_ref = next(it)
    h = h_ref[0]
    if has_y:
        h = h + gate_ref[0] * (_rms(y_ref[0]) * g1_ref[...])
        hout_ref[0] = h
    if want_a:
        a = (_rms(h) * g2_ref[...]) * (1.0 + scale_ref[0]) + shift_ref[0]
        a_ref[0] = a.astype(a_ref.dtype)


def resid_norm(h, y=None, gate=None, g1=None, g2=None, shift=None, scale=None, *, bt=128):
    b, t, d = h.shape
    has_y, want_a = y is not None, g2 is not None
    bt = min(bt, t)
    tok = pl.BlockSpec((1, bt, d), lambda i, j: (i, j, 0))
    per_b = pl.BlockSpec((1, 1, d), lambda i, j: (i, 0, 0))
    vec = pl.BlockSpec((1, d), lambda i, j: (0, 0))
    args, in_specs, out_shape, out_specs = [h], [tok], [], []
    if has_y:
        args += [y, gate, g1.reshape(1, d)]
        in_specs += [tok, per_b, vec]
        out_shape.append(jax.ShapeDtypeStruct((b, t, d), F32))
        out_specs.append(tok)
    if want_a:
        args += [g2.reshape(1, d), shift, scale]
        in_specs += [vec, per_b, per_b]
        out_shape.append(jax.ShapeDtypeStruct((b, t, d), BF16))
        out_specs.append(tok)
    outs = pl.pallas_call(
        functools.partial(_resid_norm_kernel, has_y=has_y, want_a=want_a),
        grid=(b, t // bt),
        in_specs=in_specs,
        out_specs=out_specs,
        out_shape=out_shape,
        compiler_params=_cparams(("parallel", "parallel")),
        name="resid_norm",
    )(*args)
    outs = list(outs)
    h_new = outs.pop(0) if has_y else h
    a = outs.pop(0) if want_a else None
    return h_new, a


def _softplus(x):
    return jnp.maximum(x, 0.0) + jnp.log1p(jnp.exp(-jnp.abs(x)))


def _log_sigmoid(x):
    return jnp.minimum(x, 0.0) - jnp.log1p(jnp.exp(-jnp.abs(x)))


def _chunk_index(c, nc, reverse):
    return (nc - 1 - c) if reverse else c


def _mlstm_kernel(*refs, reverse, nc, has_prev):
    (q_ref, k_ref, v_ref, gc_ref, gr_ref, c0_ref, n0_ref, m0_ref), rest = refs[:8], refs[8:]
    if has_prev:
        yprev_ref, rest = rest[0], rest[1:]
    y_ref, c_ref, n_ref, m_ref = rest
    l = ML_CHUNK
    step = pl.program_id(2)

    @pl.when(step == 0)
    def _():
        c_ref[...] = c0_ref[...]
        n_ref[...] = n0_ref[...]
        m_ref[...] = m0_ref[...]

    row = lax.broadcasted_iota(jnp.int32, (l, l), 0)
    col = lax.broadcasted_iota(jnp.int32, (l, l), 1)
    mask = (col >= row) if reverse else (col <= row)
    mask_t = (col <= row) if reverse else (col >= row)

    gcol = gc_ref[0, 0]
    grow = gr_ref[0, 0]
    ig_c = GATE_CAP * jnp.tanh(gcol[:, 0:1] / GATE_CAP)
    lf_c = _log_sigmoid(GATE_CAP * jnp.tanh(gcol[:, 1:2] / GATE_CAP))
    ig_r = GATE_CAP * jnp.tanh(grow[0:1, :] / GATE_CAP)
    lf_r = _log_sigmoid(GATE_CAP * jnp.tanh(grow[1:2, :] / GATE_CAP))

    b_c = jnp.sum(jnp.where(mask, lf_r, 0.0), axis=1, keepdims=True)
    b_r = jnp.sum(jnp.where(mask_t, lf_c, 0.0), axis=0, keepdims=True)
    total = jnp.sum(lf_c, axis=0, keepdims=True)

    m_prev = m_ref[0, 0]
    logw = jnp.where(mask, b_c - b_r + ig_r, -jnp.inf)
    inter = b_c + m_prev
    m_t = jnp.maximum(jnp.max(logw, axis=1, keepdims=True), inter)

    q = q_ref[0]
    k = k_ref[0]
    v = v_ref[0]
    scale = ML_QK_DIM ** -0.5
    qk = lax.dot_general(q, k, (((1,), (1,)), ((), ())), preferred_element_type=F32)
    s = qk * scale * jnp.exp(logw - m_t)
    inter_w = jnp.exp(inter - m_t) * scale

    c_st = c_ref[0, 0]
    n_st = n_ref[0, 0]
    num = (jnp.dot(s.astype(BF16), v, preferred_element_type=F32)
           + inter_w * jnp.dot(q, c_st.astype(BF16), preferred_element_type=F32))
    qn = jnp.sum(q.astype(F32) * n_st, axis=1, keepdims=True)
    den = jnp.sum(s, axis=1, keepdims=True) + inter_w * qn
    h = num * (1.0 / jnp.maximum(jnp.abs(den), jnp.exp(-m_t)))
    if has_prev:
        h = h + yprev_ref[0]
    y_ref[0] = h

    g_c = total - b_c + ig_c
    g_r = total - b_r + ig_r
    m_new = jnp.maximum(total + m_prev, jnp.max(g_r, axis=1, keepdims=True))
    w_c = jnp.exp(g_c - m_new)
    decay = jnp.exp(total + m_prev - m_new)
    kw = k.astype(F32) * w_c
    kv = lax.dot_general(kw.astype(BF16), v, (((0,), (0,)), ((), ())), preferred_element_type=F32)
    c_ref[0, 0] = decay * c_st + kv
    n_ref[0, 0] = decay * n_st + jnp.sum(kw, axis=0, keepdims=True)
    m_ref[0, 0] = m_new


def mlstm_scan(proj, gates_col, gates_row, state, y_prev, *, reverse):
    b, t, _ = proj.shape
    l = ML_CHUNK
    nc = t // l
    ci = functools.partial(_chunk_index, nc=nc, reverse=reverse)
    c0, n0, m0 = state
    qb, kb, vb = E_Q // ML_QK_DIM, E_K // ML_QK_DIM, E_V // ML_V_DIM
    in_specs = [
        pl.BlockSpec((1, l, ML_QK_DIM), lambda i, h, c: (i, ci(c), qb + h)),
        pl.BlockSpec((1, l, ML_QK_DIM), lambda i, h, c: (i, ci(c), kb + h)),
        pl.BlockSpec((1, l, ML_V_DIM), lambda i, h, c: (i, ci(c), vb + h)),
        pl.BlockSpec((1, 1, l, 2), lambda i, h, c: (i, h, ci(c), 0)),
        pl.BlockSpec((1, 1, 2, l), lambda i, h, c: (i, h, 0, ci(c))),
        pl.BlockSpec((1, 1, ML_QK_DIM, ML_V_DIM), lambda i, h, c: (i, h, 0, 0)),
        pl.BlockSpec((1, 1, 1, ML_QK_DIM), lambda i, h, c: (i, h, 0, 0)),
        pl.BlockSpec((1, 1, 1, 1), lambda i, h, c: (i, h, 0, 0)),
    ]
    args = [proj, proj, proj, gates_col, gates_row, c0, n0, m0]
    y_spec = pl.BlockSpec((1, l, ML_V_DIM), lambda i, h, c: (i, ci(c), h))
    has_prev = y_prev is not None
    if has_prev:
        in_specs.append(y_spec)
        args.append(y_prev)
    outs = pl.pallas_call(
        functools.partial(_mlstm_kernel, reverse=reverse, nc=nc, has_prev=has_prev),
        grid=(b, ML_HEADS, nc),
        in_specs=in_specs,
        out_specs=[y_spec, in_specs[5], in_specs[6], in_specs[7]],
        out_shape=[jax.ShapeDtypeStruct((b, t, ML_V), F32),
                   jax.ShapeDtypeStruct(c0.shape, F32),
                   jax.ShapeDtypeStruct(n0.shape, F32),
                   jax.ShapeDtypeStruct(m0.shape, F32)],
        compiler_params=_cparams(("parallel", "parallel", "arbitrary")),
        name="mlstm_scan",
    )(*args)
    return outs[0], (outs[1], outs[2], outs[3])


CONV_ROWS = 256
CONV_HALO = 16


def _conv_kernel(x_ref, w_ref, b_ref, o_ref, *, t):
    w = w_ref[...]
    bias = b_ref[...]
    r = min(CONV_ROWS, t)
    half = SSM_CONV // 2
    cw = x_ref.shape[-1]
    for r0 in range(0, t, r):
        parts = []
        if r0 == 0:
            parts.append(jnp.zeros((CONV_HALO, cw), F32))
        else:
            parts.append(x_ref[0, pl.ds(r0 - CONV_HALO, CONV_HALO), :].astype(F32))
        parts.append(x_ref[0, pl.ds(r0, r), :].astype(F32))
        if r0 + r == t:
            parts.append(jnp.zeros((CONV_HALO, cw), F32))
        else:
            parts.append(x_ref[0, pl.ds(r0 + r, CONV_HALO), :].astype(F32))
        win = jnp.concatenate(parts, axis=0)
        n = r + 2 * CONV_HALO
        acc = jnp.zeros((r, cw), F32) + bias
        for kk in range(SSM_CONV):
            sh = (half - kk) % n
            rolled = win if sh == 0 else pltpu.roll(win, sh, 0)
            acc = acc + w[kk:kk + 1, :] * rolled[CONV_HALO:CONV_HALO + r, :]
        o_ref[0, pl.ds(r0, r), :] = (acc * jax.nn.sigmoid(acc)).astype(o_ref.dtype)


def conv_silu(proj, conv_w, conv_b, *, bc=512):
    b, t, _ = proj.shape
    c = SSM_CONV_DIM
    off = E_XBC // bc
    return pl.pallas_call(
        functools.partial(_conv_kernel, t=t),
        grid=(b, c // bc),
        in_specs=[pl.BlockSpec((1, t, bc), lambda i, j: (i, 0, off + j)),
                  pl.BlockSpec((SSM_CONV, bc), lambda i, j: (0, j)),
                  pl.BlockSpec((1, bc), lambda i, j: (0, j))],
        out_specs=pl.BlockSpec((1, t, bc), lambda i, j: (i, 0, j)),
        out_shape=jax.ShapeDtypeStruct((b, t, c), BF16),
        compiler_params=_cparams(("parallel", "parallel")),
        name="conv_silu",
    )(proj, conv_w, conv_b.reshape(1, c))


def _expand_heads(x_col):
    l = x_col.shape[0]
    lane_head = lax.broadcasted_iota(jnp.int32, (l, SSM_GROUP_W), 1) // SSM_HEAD_DIM
    out = jnp.zeros((l, SSM_GROUP_W), F32)
    for j in range(SSM_HPG):
        out = jnp.where(lane_head == j, x_col[:, j:j + 1], out)
    return out


def _ssd_kernel(*refs, reverse, has_prev):
    (xs_ref, bm_ref, cm_ref, dtc_ref, dtr_ref, pc_ref, pr_ref, h0_ref), rest = refs[:8], refs[8:]
    if has_prev:
        yprev_ref, rest = rest[0], rest[1:]
    y_ref, h_ref = rest
    l = SSM_CHUNK
    step = pl.program_id(2)

    @pl.when(step == 0)
    def _():
        h_ref[...] = h0_ref[...]

    row = lax.broadcasted_iota(jnp.int32, (l, l), 0)
    col = lax.broadcasted_iota(jnp.int32, (l, l), 1)
    mask = (col >= row) if reverse else (col <= row)
    mask_t = (col <= row) if reverse else (col >= row)

    pc = pc_ref[0, 0]
    pr = pr_ref[0, 0]
    dt_c = _softplus(dtc_ref[0, 0] + pc[0:1, :])
    dt_r = _softplus(dtr_ref[0, 0] + pr[:, 0:1])
    a_c = dt_c * pc[1:2, :]
    a_r = dt_r * pr[:, 1:2]
    total_c = jnp.sum(a_c, axis=0, keepdims=True)

    xs = xs_ref[0].astype(F32)
    bm = bm_ref[0]
    cm = cm_ref[0]
    xdt = xs * _expand_heads(dt_c)
    xdt_b = xdt.astype(BF16)
    cb = lax.dot_general(cm, bm, (((1,), (1,)), ((), ())), preferred_element_type=F32)
    hst = h_ref[0, 0]
    inter = jnp.dot(cm, hst.astype(BF16), preferred_element_type=F32)

    lane_head = lax.broadcasted_iota(jnp.int32, (l, SSM_GROUP_W), 1) // SSM_HEAD_DIM
    y = jnp.zeros((l, SSM_GROUP_W), F32)
    cum_cols = []
    for j in range(SSM_HPG):
        cum_c = jnp.sum(jnp.where(mask, a_r[j:j + 1, :], 0.0), axis=1, keepdims=True)
        cum_r = jnp.sum(jnp.where(mask_t, a_c[:, j:j + 1], 0.0), axis=0, keepdims=True)
        cum_cols.append(cum_c)
        decay = jnp.exp(jnp.where(mask, cum_c - cum_r, -jnp.inf))
        wj = (decay * cb).astype(BF16)
        xj = jnp.where(lane_head == j, xdt, 0.0).astype(BF16)
        y = y + jnp.dot(wj, xj, preferred_element_type=F32)
    cum_all = jnp.concatenate(cum_cols, axis=1)
    y = y + jnp.exp(_expand_heads(cum_all)) * inter
    if has_prev:
        y = y + yprev_ref[0]
    y_ref[0] = y

    w_end = jnp.exp(total_c - cum_all)
    xw = (xdt * _expand_heads(w_end)).astype(BF16)
    upd = lax.dot_general(bm, xw, (((0,), (0,)), ((), ())), preferred_element_type=F32)
    h_ref[0, 0] = jnp.exp(_expand_heads(total_c)) * hst + upd


def ssd_scan(xbc, dt_col, dt_row, par_col, par_row, state, y_prev, *, reverse):
    b, t, _ = xbc.shape
    l = SSM_CHUNK
    nc = t // l
    ci = functools.partial(_chunk_index, nc=nc, reverse=reverse)
    bmb = SSM_INNER // SSM_STATE
    cmb = bmb + SSM_GROUPS
    in_specs = [
        pl.BlockSpec((1, l, SSM_GROUP_W), lambda i, g, c: (i, ci(c), g)),
        pl.BlockSpec((1, l, SSM_STATE), lambda i, g, c: (i, ci(c), bmb + g)),
        pl.BlockSpec((1, l, SSM_STATE), lambda i, g, c: (i, ci(c), cmb + g)),
        pl.BlockSpec((1, 1, l, SSM_HPG), lambda i, g, c: (i, g, ci(c), 0)),
        pl.BlockSpec((1, 1, SSM_HPG, l), lambda i, g, c: (i, g, 0, ci(c))),
        pl.BlockSpec((1, 1, 2, SSM_HPG), lambda i, g, c: (0, g, 0, 0)),
        pl.BlockSpec((1, 1, SSM_HPG, 2), lambda i, g, c: (0, g, 0, 0)),
        pl.BlockSpec((1, 1, SSM_STATE, SSM_GROUP_W), lambda i, g, c: (i, g, 0, 0)),
    ]
    args = [xbc, xbc, xbc, dt_col, dt_row, par_col, par_row, state]
    y_spec = pl.BlockSpec((1, l, SSM_GROUP_W), lambda i, g, c: (i, ci(c), g))
    has_prev = y_prev is not None
    if has_prev:
        in_specs.append(y_spec)
        args.append(y_prev)
    y, h = pl.pallas_call(
        functools.partial(_ssd_kernel, reverse=reverse, has_prev=has_prev),
        grid=(b, SSM_GROUPS, nc),
        in_specs=in_specs,
        out_specs=[y_spec, in_specs[7]],
        out_shape=[jax.ShapeDtypeStruct((b, t, SSM_INNER), F32),
                   jax.ShapeDtypeStruct(state.shape, F32)],
        compiler_params=_cparams(("parallel", "parallel", "arbitrary")),
        name="ssd_scan",
    )(*args)
    return y, h


def _even_out_kernel(ml_ref, ssd_ref, o_ref, z_ref, xs_ref, mlg_ref, dsk_ref, ssg_ref, out_ref):
    for h in range(ML_HEADS):
        sl = slice(h * ML_V_DIM, (h + 1) * ML_V_DIM)
        y = ml_ref[0, :, sl]
        og = o_ref[0, :, sl].astype(F32)
        out_ref[0, :, sl] = (_rms(y) * mlg_ref[:, sl] * jax.nn.sigmoid(og)).astype(out_ref.dtype)
    for g in range(SSM_GROUPS):
        sl = slice(g * SSM_GROUP_W, (g + 1) * SSM_GROUP_W)
        z = z_ref[0, :, sl].astype(F32)
        y = ssd_ref[0, :, sl] + dsk_ref[:, sl] * xs_ref[0, :, sl].astype(F32)
        y = y * (z * jax.nn.sigmoid(z))
        so = slice(ML_V + g * SSM_GROUP_W, ML_V + (g + 1) * SSM_GROUP_W)
        out_ref[0, :, so] = (_rms(y) * ssg_ref[:, sl]).astype(out_ref.dtype)


def even_out(ml_y, ssd_y, proj, xbc, ml_norm_g, d_skip_full, ssm_norm_g, *, bt=128):
    b, t, _ = ml_y.shape
    bt = min(bt, t)
    w = ML_V
    tok = lambda blk: pl.BlockSpec((1, bt, w), lambda i, j: (i, j, blk))
    vec = pl.BlockSpec((1, w), lambda i, j: (0, 0))
    return pl.pallas_call(
        _even_out_kernel,
        grid=(b, t // bt),
        in_specs=[tok(0), tok(0), tok(E_O // w), tok(E_Z // w), tok(0), vec, vec, vec],
        out_specs=pl.BlockSpec((1, bt, 2 * w), lambda i, j: (i, j, 0)),
        out_shape=jax.ShapeDtypeStruct((b, t, 2 * w), BF16),
        compiler_params=_cparams(("parallel", "parallel")),
        name="even_gate",
    )(ml_y, ssd_y, proj, proj, xbc, ml_norm_g.reshape(1, w), d_skip_full.reshape(1, w),
      ssm_norm_g.reshape(1, w))


def _hg_constants(reverse):
    l = HG_CHUNK
    pos = np.arange(l)
    tri = (pos[None, :] <= pos[:, None]).astype(np.float32)
    masks = []
    for lv in range(HG_LEVELS):
        m = 1 << lv
        blk = pos // (2 * m)
        later = (pos % (2 * m)) >= m
        masks.append(((blk[:, None] == blk[None, :]) & later[:, None] & (~later)[None, :]).astype(np.float32))
    masks = np.stack(masks)
    if reverse:
        tri = tri[::-1, ::-1]
        masks = masks[:, ::-1, ::-1]
    return (jnp.asarray(tri, BF16), jnp.asarray(masks.reshape(-1, l), F32))


def _split3(x):
    hi = x.astype(BF16)
    r1 = x - hi.astype(F32)
    mid = r1.astype(BF16)
    lo = (r1 - mid.astype(F32)).astype(BF16)
    return hi, mid, lo


def _hg_level_factors(q, k, f, cum, reverse):
    l, w = q.shape
    row = lax.broadcasted_iota(jnp.int32, (l, w), 0)
    sub = lax.broadcasted_iota(jnp.int32, (8, w), 0)
    zs = []
    for lv in range(HG_LEVELS):
        m = 1 << lv
        late = ((row & m) == 0) if reverse else ((row & m) != 0)
        if m == 1:
            ex = jnp.where(late, f, 1.0)
        else:
            pieces = []
            if m >= 4:
                for b0 in range(0, l, 2 * m):
                    r = b0 + (m if reverse else m - 1)
                    pieces.append(cum[b0:b0 + 2 * m] - cum[r:r + 1])
            else:
                for b0 in range(0, l, 8):
                    ra, rb = (b0 + 2, b0 + 6) if reverse else (b0 + 1, b0 + 5)
                    mid = jnp.where(sub < 4, cum[ra:ra + 1], cum[rb:rb + 1])
                    pieces.append(cum[b0:b0 + 8] - mid)
            dd = pieces[0] if len(pieces) == 1 else jnp.concatenate(pieces, axis=0)
            ex = jnp.exp(jnp.where(late, dd, -dd))
        zs.append((jnp.where(late, q, k) * ex).astype(BF16))
    return zs


def _hg_chunk(q_ref, f_ref, i_ref, lb, tri, masks_ref, y_ref, s_ref, r0, reverse):
    l, e = HG_CHUNK, HG_EXPAND
    q_raw = q_ref[0, pl.ds(r0, l), :].astype(F32)
    f_raw = f_ref[0, pl.ds(r0, l), :].astype(F32)
    v = i_ref[0, pl.ds(r0, l), :]
    f = lb + (1.0 - lb) * jax.nn.sigmoid(f_raw)
    k = 1.0 - f
    lf = jnp.log(f)
    q = q_raw * jax.nn.sigmoid(q_raw) * (e ** -0.5)
    hi, mid, lo = _split3(lf)
    cum = (jnp.dot(tri, hi, preferred_element_type=F32)
           + jnp.dot(tri, mid, preferred_element_type=F32)
           + jnp.dot(tri, lo, preferred_element_type=F32))
    zs = _hg_level_factors(q, k, f, cum, reverse)
    end_row = 0 if reverse else l - 1
    cum_end = cum[end_row:end_row + 1, :]
    qc = (q * jnp.exp(cum)).astype(BF16)
    kt_end = (k * jnp.exp(cum_end - cum)).astype(BF16)
    s_decay = jnp.exp(cum_end)
    qk = q * k
    row = lax.broadcasted_iota(jnp.int32, (l, l), 0)
    col = lax.broadcasted_iota(jnp.int32, (l, l), 1)
    eye = row == col
    nt = (((1,), (1,)), ((), ()))
    tn = (((0,), (0,)), ((), ()))
    for hh in range(HG_HEADS_PER_STEP):
        sl = slice(hh * e, (hh + 1) * e)
        att = jnp.where(eye, jnp.sum(qk[:, sl], axis=1, keepdims=True), 0.0)
        for lv in range(HG_LEVELS):
            z = zs[lv][:, sl]
            a_lv = lax.dot_general(z, z, nt, preferred_element_type=F32)
            att = att + masks_ref[pl.ds(lv * l, l), :] * a_lv
        st = s_ref[0, hh]
        vh = v[:, sl]
        o = (jnp.dot(att.astype(BF16), vh, preferred_element_type=F32)
             + lax.dot_general(qc[:, sl], st.astype(BF16), nt, preferred_element_type=F32))
        y_ref[0, pl.ds(r0, l), sl] = o
        upd = lax.dot_general(vh, kt_end[:, sl], tn, preferred_element_type=F32)
        s_ref[0, hh] = s_decay[:, sl] * st + upd


def _hg_kernel(qf_ref, ff_ref, if_ref, qb_ref, fb_ref, ib_ref, lbf_ref, lbb_ref, trif_ref, trib_ref,
               mf_ref, mb_ref, sf0_ref, sb0_ref, yf_ref, yb_ref, sf_ref, sb_ref, *, n_inner):
    l = HG_CHUNK

    @pl.when(pl.program_id(2) == 0)
    def _():
        sf_ref[...] = sf0_ref[...]
        sb_ref[...] = sb0_ref[...]

    lbf = lbf_ref[0]
    lbb = lbb_ref[0]
    trif = trif_ref[...]
    trib = trib_ref[...]

    def chunk(ci, carry):
        rf = pl.multiple_of(ci * l, l)
        rb = pl.multiple_of((n_inner - 1 - ci) * l, l)
        _hg_chunk(qf_ref, ff_ref, if_ref, lbf, trif, mf_ref, yf_ref, sf_ref, rf, False)
        _hg_chunk(qb_ref, fb_ref, ib_ref, lbb, trib, mb_ref, yb_ref, sb_ref, rb, True)
        return carry

    lax.fori_loop(0, n_inner, chunk, 0, unroll=2)


def hgrn2_scan(proj, lb, states, *, tb=512, name):
    b, t, _ = proj.shape
    tb = min(tb, t)
    nb = t // tb
    n_inner = tb // HG_CHUNK
    hw = HG_HEADS_PER_STEP * HG_EXPAND
    nh = HG_HEADS // HG_HEADS_PER_STEP
    qb_, ffb, fbb, ib_ = O_Q // hw, O_F // hw, (O_F + HG_F) // hw, O_I // hw
    trif, mf = _hg_constants(False)
    trib, mb = _hg_constants(True)
    fwd = lambda blk: pl.BlockSpec((1, tb, hw), lambda i, h, c: (i, c, blk + h))
    bwd = lambda blk: pl.BlockSpec((1, tb, hw), lambda i, h, c: (i, nb - 1 - c, blk + h))
    const = lambda a: pl.BlockSpec(a.shape, lambda i, h, c: (0,) * a.ndim)
    st_spec = pl.BlockSpec((1, HG_HEADS_PER_STEP, HG_V_DIM, HG_EXPAND), lambda i, h, c: (i, h, 0, 0))
    in_specs = [fwd(qb_), fwd(ffb), fwd(ib_), bwd(qb_), bwd(fbb), bwd(ib_),
                pl.BlockSpec((1, 1, hw), lambda i, h, c: (0, 0, h)),
                pl.BlockSpec((1, 1, hw), lambda i, h, c: (1, 0, h)),
                const(trif), const(trib), const(mf), const(mb), st_spec, st_spec]
    yf, yb, sf, sb = pl.pallas_call(
        functools.partial(_hg_kernel, n_inner=n_inner),
        grid=(b, nh, nb),
        in_specs=in_specs,
        out_specs=[fwd(0), bwd(0), st_spec, st_spec],
        out_shape=[jax.ShapeDtypeStruct((b, t, D_MODEL), F32),
                   jax.ShapeDtypeStruct((b, t, D_MODEL), F32),
                   jax.ShapeDtypeStruct(states[0].shape, F32),
                   jax.ShapeDtypeStruct(states[1].shape, F32)],
        compiler_params=_cparams(("parallel", "parallel", "arbitrary")),
        name=name,
    )(proj, proj, proj, proj, proj, proj, lb, lb, trif, trib, mf, mb, states[0], states[1])
    return (yf, yb), (sf, sb)


def _hg_out_kernel(of_ref, ob_ref, g_ref, ng_ref, out_ref):
    for h in range(of_ref.shape[-1] // HG_V_DIM):
        sl = slice(h * HG_V_DIM, (h + 1) * HG_V_DIM)
        g = g_ref[0, :, sl].astype(F32)
        o = of_ref[0, :, sl] + ob_ref[0, :, sl]
        out_ref[0, :, sl] = (_rms(o) * ng_ref[:, sl] * (g * jax.nn.sigmoid(g))).astype(out_ref.dtype)


def hg_out(o_f, o_b, proj, norm_g, *, bt=256, bc=1024, name):
    b, t, d = o_f.shape
    bt = min(bt, t)
    gb = O_G // bc
    tok = pl.BlockSpec((1, bt, bc), lambda i, j, c: (i, j, c))
    return pl.pallas_call(
        _hg_out_kernel,
        grid=(b, t // bt, d // bc),
        in_specs=[tok, tok,
                  pl.BlockSpec((1, bt, bc), lambda i, j, c: (i, j, gb + c)),
                  pl.BlockSpec((1, bc), lambda i, j, c: (0, c))],
        out_specs=tok,
        out_shape=jax.ShapeDtypeStruct((b, t, d), BF16),
        compiler_params=_cparams(("parallel", "parallel", "parallel")),
        name=name,
    )(o_f, o_b, proj, norm_g.reshape(1, d))


def _run_both(scan, args_c, args_l, init):
    y_c = y_l = None
    for d in range(2):
        y_c, st = scan(*args_c, init, y_c, d)
        y_l, _ = scan(*args_l, st, y_l, d)
    return y_c, y_l


def even_mixer(a_c, a_l, w_main, w_small, w_out, layer, ml_gate_b, ml_norm_g, conv_w, conv_b,
               dt_bias, a_log, d_skip, ssm_norm_g, need_ctx):
    bsz = a_l.shape[0]

    def features(a, tag):
        b, t, d = a.shape
        bm = min(1024, b * t)
        a2 = a.reshape(b * t, d)
        proj = matmul(a2, w_main, bm=bm, bn=1024, out_dtype=BF16, name="ab_in_" + tag).reshape(b, t, E_MAIN)
        small = matmul(a2, w_small, bm=bm, bn=128, out_dtype=F32, name="ab_in_small_" + tag).reshape(b, t, 128)
        gates = small[..., :16].reshape(b, t, 2, 2, ML_HEADS) + ml_gate_b.astype(F32)
        g_col = gates.transpose(2, 0, 4, 1, 3)
        g_row = gates.transpose(2, 0, 4, 3, 1)
        dt = small[..., 16:16 + 2 * SSM_HEADS].reshape(b, t, 2, SSM_GROUPS, SSM_HPG)
        dt_col = dt.transpose(2, 0, 3, 1, 4)
        dt_row = dt.transpose(2, 0, 3, 4, 1)
        xbc = conv_silu(proj, conv_w, conv_b)
        return dict(proj=proj, xbc=xbc, g_col=g_col, g_row=g_row, dt_col=dt_col, dt_row=dt_row)

    fc, fl = features(a_c, "ctx"), features(a_l, "lat")
    neg_a = -jnp.exp(a_log.astype(F32))
    par = jnp.stack([dt_bias.astype(F32), neg_a], axis=1).reshape(2, 2, SSM_GROUPS, SSM_HPG)
    par_col = par.transpose(0, 2, 1, 3)[:, None]
    par_row = par.transpose(0, 2, 3, 1)[:, None]

    ml_init = (jnp.zeros((bsz, ML_HEADS, ML_QK_DIM, ML_V_DIM), F32),
               jnp.zeros((bsz, ML_HEADS, 1, ML_QK_DIM), F32),
               jnp.zeros((bsz, ML_HEADS, 1, 1), F32))
    ssd_init = jnp.zeros((bsz, SSM_GROUPS, SSM_STATE, SSM_GROUP_W), F32)

    def ml(f, st, y_prev, d):
        return mlstm_scan(f['proj'], f['g_col'][d], f['g_row'][d], st, y_prev, reverse=d == 1)

    def ssd(f, st, y_prev, d):
        return ssd_scan(f['xbc'], f['dt_col'][d], f['dt_row'][d], par_col[d], par_row[d], st, y_prev,
                        reverse=d == 1)

    ml_c, ml_l = _run_both(ml, (fc,), (fl,), ml_init)
    ssd_c, ssd_l = _run_both(ssd, (fc,), (fl,), ssd_init)
    d_full = jnp.repeat(d_skip.astype(F32), SSM_HEAD_DIM)

    def output(f, ml_y, ssd_y, tag):
        b, t, _ = ml_y.shape
        cat = even_out(ml_y, ssd_y, f['proj'], f['xbc'], ml_norm_g, d_full, ssm_norm_g)
        return matmul(cat.reshape(b * t, D_MODEL), w_out, bm=min(1024, b * t), bn=512, out_dtype=F32,
                      layer=layer, name="ab_out_" + tag).reshape(b, t, D_MODEL)

    y_ctx = output(fc, ml_c, ssd_c, "ctx") if need_ctx else None
    return y_ctx, output(fl, ml_l, ssd_l, "lat")


def hgrn2_mixer(a_c, a_l, w_in, w_out, layer, lb, norm_g, need_ctx, lat_colmajor=True):
    bsz = a_l.shape[0]
    lb3 = lb.astype(F32).reshape(2, 1, HG_F)

    def features(a, tag):
        b, t, d = a.shape
        return matmul(a.reshape(b * t, d), w_in, bm=min(1024, b * t), bn=512, out_dtype=BF16,
                      layer=layer, name="hg_in_" + tag).reshape(b, t, 5 * D_MODEL)

    pc, pl_ = features(a_c, "ctx"), features(a_l, "lat")
    init = jnp.zeros((bsz, HG_HEADS, HG_V_DIM, HG_EXPAND), F32)
    o_c, st = hgrn2_scan(pc, lb3, (init, init), name="hg_scan_ctx")
    o_l, _ = hgrn2_scan(pl_, lb3, st, name="hg_scan_lat")

    def output(p, o, colmajor, tag):
        b, t, _ = o[0].shape
        gated = hg_out(o[0], o[1], p, norm_g, name="hg_gate_" + tag)
        if colmajor:
            gated = grid_from_colmajor(gated)
        return matmul(gated.reshape(b * t, D_MODEL), w_out, bm=min(1024, b * t), bn=512, out_dtype=F32,
                      layer=layer, name="hg_out_" + tag).reshape(b, t, D_MODEL)

    y_ctx = output(pc, o_c, False, "ctx") if need_ctx else None
    return y_ctx, output(pl_, o_l, lat_colmajor, "lat")


def ffn(a, wg, wu, wd, layer, tag):
    b, t, d = a.shape
    m = b * t
    hid = swiglu_up(a.reshape(m, d), wg, wu, layer, bm=min(1024, m), bn=256, name="ffn_up_" + tag)
    return matmul(hid, wd, bm=512, bn=512, out_dtype=F32, name="ffn_down_" + tag).reshape(b, t, d)


def grid_to_colmajor(h):
    bsz, t, d = h.shape
    rows = t // GRID_W
    return h.reshape(bsz, rows, GRID_W, d).transpose(0, 2, 1, 3).reshape(bsz, t, d)


def grid_from_colmajor(h):
    bsz, t, d = h.shape
    rows = t // GRID_W
    return h.reshape(bsz, GRID_W, rows, d).transpose(0, 2, 1, 3).reshape(bsz, t, d)


def kernel(x, c, ctx, c_ctx, ada_w, ada_b, norm_g, ffn_w_gate, ffn_w_up, ffn_w_down,
           ab_w_in, ab_w_out, ml_gate_b, ml_norm_g, ssm_conv_w, ssm_conv_b, ssm_dt_bias,
           ssm_a_log, ssm_d, ssm_norm_g, hg_w_in, hg_w_out, hg_lb, hg_norm_g):
    depth = ada_w.shape[0]
    bsz = x.shape[0]
    d = D_MODEL
    lb_p = jax.nn.softmax(hg_lb.astype(F32), axis=1)
    lower_bounds = jnp.cumsum(lb_p, axis=1) - lb_p[:, :1]
    cond = jnp.concatenate([jax.nn.silu(c), jax.nn.silu(c_ctx)[None, :],
                            jnp.zeros((8 - bsz - 1, d), F32)], axis=0)
    h_lat, h_ctx = x, ctx
    y_l = y_c = f_l = f_c = None
    gate_l = gate_c = g_prev = None
    for layer in range(depth):
        need_ctx = layer < depth - 1
        j = layer // 2
        mod = matmul(cond, ada_w, bm=8, bn=512, out_dtype=F32, bias=ada_b[layer], layer=layer,
                     name="ada_mod")
        mod_l = [mod[:bsz, i * d:(i + 1) * d][:, None, :] for i in range(6)]
        mod_c = [jnp.broadcast_to(mod[bsz, i * d:(i + 1) * d][None, None, :], (bsz, 1, d)) for i in range(6)]
        g = norm_g[layer]
        h_lat, a_l = resid_norm(h_lat, f_l, gate_l, g_prev, g[0], mod_l[0], mod_l[1])
        h_ctx, a_c = resid_norm(h_ctx, f_c, gate_c, g_prev, g[0], mod_c[0], mod_c[1])
        if layer % 2 == 0:
            w_in = ab_w_in[j]
            w_main = jnp.concatenate(
                [w_in[:, :6144], w_in[:, 6160:6160 + 2048 + SSM_CONV_DIM]], axis=1).astype(BF16)
            w_small = jnp.concatenate(
                [w_in[:, 6144:6160], w_in[:, 12304:12368], jnp.zeros((d, 128 - 80), F32)], axis=1).astype(BF16)
            y_c, y_l = even_mixer(a_c, a_l, w_main, w_small, ab_w_out, j, ml_gate_b[j],
                                  ml_norm_g[j], ssm_conv_w[j], ssm_conv_b[j], ssm_dt_bias[j],
                                  ssm_a_log[j], ssm_d[j], ssm_norm_g[j], need_ctx)
        else:
            y_c, y_l = hgrn2_mixer(a_c, grid_to_colmajor(a_l), hg_w_in, hg_w_out, j,
                                   lower_bounds[:, layer], hg_norm_g[j], need_ctx)
        wd = ffn_w_down[layer].astype(BF16)
        h_lat, a2_l = resid_norm(h_lat, y_l, mod_l[2], g[1], g[2], mod_l[3], mod_l[4])
        f_l = ffn(a2_l, ffn_w_gate, ffn_w_up, wd, layer, "lat")
        gate_l = mod_l[5]
        if need_ctx:
            h_ctx, a2_c = resid_norm(h_ctx, y_c, mod_c[2], g[1], g[2], mod_c[3], mod_c[4])
            f_c = ffn(a2_c, ffn_w_gate, ffn_w_up, wd, layer, "ctx")
            gate_c = mod_c[5]
        else:
            f_c = gate_c = None
        g_prev = g[3]
    h_lat, _ = resid_norm(h_lat, f_l, gate_l, g_prev)
    return h_lat
```

```python
import functools
import math

import jax
import jax.numpy as jnp
import numpy as np
from jax import lax
from jax.experimental import pallas as pl
from jax.experimental.pallas import tpu as pltpu

F32 = jnp.float32
BF16 = jnp.bfloat16

D_MODEL = 4096
GRID_W = 64
EPS = 1e-6

ML_HEADS = 4
ML_QK_DIM = 256
ML_V_DIM = 512
ML_QK = ML_HEADS * ML_QK_DIM
ML_V = ML_HEADS * ML_V_DIM
ML_CHUNK = 128
GATE_CAP = 15.0

SSM_HEAD_DIM = 64
SSM_INNER = 2048
SSM_HEADS = 32
SSM_GROUPS = 8
SSM_HPG = 4
SSM_STATE = 128
SSM_CONV = 5
SSM_CHUNK = 128
SSM_GROUP_W = SSM_HPG * SSM_HEAD_DIM
SSM_CONV_DIM = SSM_INNER + 2 * SSM_GROUPS * SSM_STATE
SSD_GROUPS_PER_STEP = 2

HG_EXPAND = 128
HG_HEADS = 32
HG_F = HG_HEADS * HG_EXPAND
HG_V_DIM = 128
HG_CHUNK = 64
HG_LEVELS = 6
HG_HEADS_PER_STEP = 4

FFN_HIDDEN = 11008

E_Q, E_K, E_V, E_O, E_Z, E_XBC = 0, 1024, 2048, 4096, 6144, 8192
E_MAIN = 12288
O_Q, O_F, O_I, O_G = 0, 4096, 12288, 16384

VMEM_LIMIT = 56 * 1024 * 1024


def _cparams(sem):
    return pltpu.CompilerParams(dimension_semantics=sem, vmem_limit_bytes=VMEM_LIMIT)


def _mm_kernel(*refs, has_bias, cast_w):
    x_ref, w_ref = refs[0], refs[1]
    b_ref = refs[2] if has_bias else None
    o_ref = refs[2 + has_bias]
    if cast_w:
        wb_ref = refs[3 + has_bias]

        @pl.when(pl.program_id(1) == 0)
        def _():
            wb_ref[...] = w_ref[...].astype(BF16)

        w = wb_ref[...]
    else:
        w = w_ref[...]
    acc = jnp.dot(x_ref[...].astype(BF16), w, preferred_element_type=F32)
    if has_bias:
        acc = acc + b_ref[...]
    o_ref[...] = acc.astype(o_ref.dtype)


def matmul(x, w, *, bm, bn, out_dtype, name, bias=None, layer=None):
    m, k = x.shape
    n = w.shape[-1]
    assert m % bm == 0 and n % bn == 0, (m, bm, n, bn)
    if layer is None:
        w_spec = pl.BlockSpec((k, bn), lambda j, i: (0, j))
    else:
        w_spec = pl.BlockSpec((None, k, bn), lambda j, i: (layer, 0, j))
    in_specs = [pl.BlockSpec((bm, k), lambda j, i: (i, 0)), w_spec]
    args = [x, w]
    if bias is not None:
        in_specs.append(pl.BlockSpec((1, bn), lambda j, i: (0, j)))
        args.append(bias.reshape(1, n))
    cast_w = w.dtype != BF16
    return pl.pallas_call(
        functools.partial(_mm_kernel, has_bias=bias is not None, cast_w=cast_w),
        grid=(n // bn, m // bm),
        in_specs=in_specs,
        out_specs=pl.BlockSpec((bm, bn), lambda j, i: (i, j)),
        out_shape=jax.ShapeDtypeStruct((m, n), out_dtype),
        scratch_shapes=[pltpu.VMEM((k, bn), BF16)] if cast_w else [],
        compiler_params=_cparams(("parallel", "arbitrary" if cast_w else "parallel")),
        name=name,
    )(*args)


def _swiglu(g, u):
    return g * jax.nn.sigmoid(g) * u


def _mm_stream_kernel(*refs, n_w, nj, kc):
    x_ref, w_refs, o_ref, wb_refs = refs[0], refs[1:1 + n_w], refs[1 + n_w], refs[2 + n_w:]
    j, i = pl.program_id(0), pl.program_id(1)
    slot = j % 2

    @pl.when(j < nj)
    def _():
        r0 = pl.multiple_of(i * kc, 16)
        for w_ref, wb_ref in zip(w_refs, wb_refs):
            wb_ref[slot, pl.ds(r0, kc), :] = w_ref[...].astype(BF16)

    @pl.when(j > 0)
    def _():
        x = x_ref[...]
        accs = [jnp.dot(x, wb_ref[1 - slot], preferred_element_type=F32) for wb_ref in wb_refs]
        out = _swiglu(*accs) if n_w == 2 else accs[0]
        o_ref[...] = out.astype(o_ref.dtype)


def matmul_stream(x, ws, layer, *, bm, bn, out_dtype, name):
    m, k = x.shape
    n = ws[0].shape[-1]
    mi, nj = m // bm, n // bn
    kc = k // mi
    assert m % bm == 0 and n % bn == 0 and k % mi == 0 and kc % 16 == 0 and mi > 1, (m, bm, n, bn, k)
    row = lambda j, i: jnp.where(j == 0, 0, i)
    w_spec = pl.BlockSpec((None, kc, bn),
                          lambda j, i: (layer, jnp.where(j < nj, i, mi - 1), jnp.minimum(j, nj - 1)))
    return pl.pallas_call(
        functools.partial(_mm_stream_kernel, n_w=len(ws), nj=nj, kc=kc),
        grid=(nj + 1, mi),
        in_specs=[pl.BlockSpec((bm, k), lambda j, i: (row(j, i), 0))] + [w_spec] * len(ws),
        out_specs=pl.BlockSpec((bm, bn), lambda j, i: (row(j, i), jnp.maximum(j - 1, 0))),
        out_shape=jax.ShapeDtypeStruct((m, n), out_dtype),
        scratch_shapes=[pltpu.VMEM((2, k, bn), BF16) for _ in ws],
        compiler_params=_cparams(("arbitrary", "arbitrary")),
        name=name,
    )(x, *ws)


def _swiglu_kernel(x_ref, wg_ref, wu_ref, o_ref):
    x = x_ref[...]
    g = jnp.dot(x, wg_ref[...].astype(BF16), preferred_element_type=F32)
    u = jnp.dot(x, wu_ref[...].astype(BF16), preferred_element_type=F32)
    o_ref[...] = _swiglu(g, u).astype(o_ref.dtype)


def swiglu_single(x, wg, wu, layer, *, bn, name):
    m, k = x.shape
    n = wg.shape[-1]
    assert n % bn == 0
    w_spec = pl.BlockSpec((None, k, bn), lambda j: (layer, 0, j))
    return pl.pallas_call(
        _swiglu_kernel,
        grid=(n // bn,),
        in_specs=[pl.BlockSpec((m, k), lambda j: (0, 0)), w_spec, w_spec],
        out_specs=pl.BlockSpec((m, bn), lambda j: (0, j)),
        out_shape=jax.ShapeDtypeStruct((m, n), BF16),
        compiler_params=_cparams(("parallel",)),
        name=name,
    )(x, wg, wu)


def dense(x, w, layer, *, out_dtype, name, bm=1024, bn=1024):
    m = x.shape[0]
    if m > bm:
        return matmul_stream(x, (w,), layer, bm=bm, bn=bn, out_dtype=out_dtype, name=name)
    return matmul(x, w, bm=m, bn=256, out_dtype=out_dtype, layer=layer, name=name)


def _rms(x):
    return x * lax.rsqrt(jnp.mean(x * x, axis=-1, keepdims=True) + EPS)


def _resid_norm_kernel(*refs, has_y, want_a):
    it = iter(refs)
    h_ref = next(it)
    if has_y:
        y_ref, gate_ref, g1_ref = next(it), next(it), next(it)
    if want_a:
        g2_ref, shift_ref, scale_ref = next(it), next(it), next(it)
    if has_y:
        hout_ref = next(it)
    if want_a:
        a_ref = next(it)
    h = h_ref[0]
    if has_y:
        h = h + gate_ref[0] * (_rms(y_ref[0]) * g1_ref[...])
        hout_ref[0] = h
    if want_a:
        a = (_rms(h) * g2_ref[...]) * (1.0 + scale_ref[0]) + shift_ref[0]
        a_ref[0] = a.astype(a_ref.dtype)


def resid_norm(h, y=None, gate=None, g1=None, g2=None, shift=None, scale=None, *, bt=128):
    b, t, d = h.shape
    has_y, want_a = y is not None, g2 is not None
    bt = min(bt, t)
    tok = pl.BlockSpec((1, bt, d), lambda i, j: (i, j, 0))
    per_b = pl.BlockSpec((1, 1, d), lambda i, j: (i, 0, 0))
    vec = pl.BlockSpec((1, d), lambda i, j: (0, 0))
    args, in_specs, out_shape, out_specs = [h], [tok], [], []
    if has_y:
        args += [y, gate, g1.reshape(1, d)]
        in_specs += [tok, per_b, vec]
        out_shape.append(jax.ShapeDtypeStruct((b, t, d), F32))
        out_specs.append(tok)
    if want_a:
        args += [g2.reshape(1, d), shift, scale]
        in_specs += [vec, per_b, per_b]
        out_shape.append(jax.ShapeDtypeStruct((b, t, d), BF16))
        out_specs.append(tok)
    outs = pl.pallas_call(
        functools.partial(_resid_norm_kernel, has_y=has_y, want_a=want_a),
        grid=(b, t // bt),
        in_specs=in_specs,
        out_specs=out_specs,
        out_shape=out_shape,
        compiler_params=_cparams(("parallel", "parallel")),
        name="resid_norm",
    )(*args)
    outs = list(outs)
    h_new = outs.pop(0) if has_y else h
    a = outs.pop(0) if want_a else None
    return h_new, a


def _softplus(x):
    return jnp.maximum(x, 0.0) + jnp.log1p(jnp.exp(-jnp.abs(x)))


def _log_sigmoid(x):
    return jnp.minimum(x, 0.0) - jnp.log1p(jnp.exp(-jnp.abs(x)))


def _chunk_index(c, nc, reverse):
    return (nc - 1 - c) if reverse else c


def _mlstm_kernel(*refs):
    ins, outs = refs[:16], refs[16:]

    @pl.when(pl.program_id(2) == 0)
    def _():
        for d in range(2):
            for s in range(3):
                outs[4 * d + 1 + s][...] = ins[8 * d + 5 + s][...]

    for d in range(2):
        _mlstm_chunk(*ins[8 * d:8 * d + 5], *outs[4 * d:4 * d + 4], reverse=d == 1)


def _mlstm_chunk(q_ref, k_ref, v_ref, gc_ref, gr_ref, y_ref, c_ref, n_ref, m_ref, *, reverse):
    l = ML_CHUNK
    row = lax.broadcasted_iota(jnp.int32, (l, l), 0)
    col = lax.broadcasted_iota(jnp.int32, (l, l), 1)
    mask = (col >= row) if reverse else (col <= row)
    mask_t = (col <= row) if reverse else (col >= row)

    gcol = gc_ref[0, 0, 0]
    grow = gr_ref[0, 0, 0]
    ig_c = GATE_CAP * jnp.tanh(gcol[:, 0:1] / GATE_CAP)
    lf_c = _log_sigmoid(GATE_CAP * jnp.tanh(gcol[:, 1:2] / GATE_CAP))
    ig_r = GATE_CAP * jnp.tanh(grow[0:1, :] / GATE_CAP)
    lf_r = _log_sigmoid(GATE_CAP * jnp.tanh(grow[1:2, :] / GATE_CAP))

    b_c = jnp.sum(jnp.where(mask, lf_r, 0.0), axis=1, keepdims=True)
    b_r = jnp.sum(jnp.where(mask_t, lf_c, 0.0), axis=0, keepdims=True)
    total = jnp.sum(lf_c, axis=0, keepdims=True)

    m_prev = m_ref[0, 0]
    logw = jnp.where(mask, b_c - b_r + ig_r, -jnp.inf)
    inter = b_c + m_prev
    m_t = jnp.maximum(jnp.max(logw, axis=1, keepdims=True), inter)

    q = q_ref[0]
    k = k_ref[0]
    v = v_ref[0]
    scale = ML_QK_DIM ** -0.5
    qk = lax.dot_general(q, k, (((1,), (1,)), ((), ())), preferred_element_type=F32)
    s = qk * scale * jnp.exp(logw - m_t)
    inter_w = jnp.exp(inter - m_t) * scale

    c_st = c_ref[0, 0]
    n_st = n_ref[0, 0]
    num = (jnp.dot(s.astype(BF16), v, preferred_element_type=F32)
           + inter_w * jnp.dot(q, c_st.astype(BF16), preferred_element_type=F32))
    qn = jnp.sum(q.astype(F32) * n_st, axis=1, keepdims=True)
    den = jnp.sum(s, axis=1, keepdims=True) + inter_w * qn
    y_ref[0] = (num * (1.0 / jnp.maximum(jnp.abs(den), jnp.exp(-m_t)))).astype(y_ref.dtype)

    g_c = total - b_c + ig_c
    g_r = total - b_r + ig_r
    m_new = jnp.maximum(total + m_prev, jnp.max(g_r, axis=1, keepdims=True))
    w_c = jnp.exp(g_c - m_new)
    decay = jnp.exp(total + m_prev - m_new)
    kw = k.astype(F32) * w_c
    kv = lax.dot_general(kw.astype(BF16), v, (((0,), (0,)), ((), ())), preferred_element_type=F32)
    c_ref[0, 0] = decay * c_st + kv
    n_ref[0, 0] = decay * n_st + jnp.sum(kw, axis=0, keepdims=True)
    m_ref[0, 0] = m_new


def mlstm_scan(proj, gates_col, gates_row, states, *, name):
    b, t, _ = proj.shape
    l = ML_CHUNK
    nc = t // l
    qb, kb, vb = E_Q // ML_QK_DIM, E_K // ML_QK_DIM, E_V // ML_V_DIM
    st_specs = [pl.BlockSpec((1, 1, ML_QK_DIM, ML_V_DIM), lambda i, h, c: (i, h, 0, 0)),
                pl.BlockSpec((1, 1, 1, ML_QK_DIM), lambda i, h, c: (i, h, 0, 0)),
                pl.BlockSpec((1, 1, 1, 1), lambda i, h, c: (i, h, 0, 0))]
    in_specs, args, out_specs, out_shape = [], [], [], []
    for d in range(2):
        ci = functools.partial(_chunk_index, nc=nc, reverse=d == 1)
        in_specs += [
            pl.BlockSpec((1, l, ML_QK_DIM), lambda i, h, c, ci=ci: (i, ci(c), qb + h)),
            pl.BlockSpec((1, l, ML_QK_DIM), lambda i, h, c, ci=ci: (i, ci(c), kb + h)),
            pl.BlockSpec((1, l, ML_V_DIM), lambda i, h, c, ci=ci: (i, ci(c), vb + h)),
            pl.BlockSpec((1, 1, 1, l, 2), lambda i, h, c, ci=ci, d=d: (d, i, h, ci(c), 0)),
            pl.BlockSpec((1, 1, 1, 2, l), lambda i, h, c, ci=ci, d=d: (d, i, h, 0, ci(c))),
        ] + st_specs
        args += [proj, proj, proj, gates_col, gates_row, *states[d]]
        out_specs += [pl.BlockSpec((1, l, ML_V_DIM), lambda i, h, c, ci=ci: (i, ci(c), h))] + st_specs
        out_shape += [jax.ShapeDtypeStruct((b, t, ML_V), BF16)]
        out_shape += [jax.ShapeDtypeStruct(s.shape, F32) for s in states[d]]
    outs = pl.pallas_call(
        _mlstm_kernel,
        grid=(b, ML_HEADS, nc),
        in_specs=in_specs,
        out_specs=out_specs,
        out_shape=out_shape,
        compiler_params=_cparams(("parallel", "parallel", "arbitrary")),
        name=name,
    )(*args)
    return (outs[0], outs[4]), (tuple(outs[1:4]), tuple(outs[5:8]))


CONV_ROWS = 256
CONV_HALO = 16


def _conv_kernel(x_ref, w_ref, b_ref, o_ref, *, t):
    w = w_ref[...]
    bias = b_ref[...]
    r = min(CONV_ROWS, t)
    half = SSM_CONV // 2
    cw = x_ref.shape[-1]
    for r0 in range(0, t, r):
        parts = []
        if r0 == 0:
            parts.append(jnp.zeros((CONV_HALO, cw), F32))
        else:
            parts.append(x_ref[0, pl.ds(r0 - CONV_HALO, CONV_HALO), :].astype(F32))
        parts.append(x_ref[0, pl.ds(r0, r), :].astype(F32))
        if r0 + r == t:
            parts.append(jnp.zeros((CONV_HALO, cw), F32))
        else:
            parts.append(x_ref[0, pl.ds(r0 + r, CONV_HALO), :].astype(F32))
        win = jnp.concatenate(parts, axis=0)
        n = r + 2 * CONV_HALO
        acc = jnp.zeros((r, cw), F32) + bias
        for kk in range(SSM_CONV):
            sh = (half - kk) % n
            rolled = win if sh == 0 else pltpu.roll(win, sh, 0)
            acc = acc + w[kk:kk + 1, :] * rolled[CONV_HALO:CONV_HALO + r, :]
        o_ref[0, pl.ds(r0, r), :] = (acc * jax.nn.sigmoid(acc)).astype(o_ref.dtype)


def conv_silu(proj, conv_w, conv_b, *, bc=512):
    b, t, _ = proj.shape
    c = SSM_CONV_DIM
    off = E_XBC // bc
    return pl.pallas_call(
        functools.partial(_conv_kernel, t=t),
        grid=(b, c // bc),
        in_specs=[pl.BlockSpec((1, t, bc), lambda i, j: (i, 0, off + j)),
                  pl.BlockSpec((SSM_CONV, bc), lambda i, j: (0, j)),
                  pl.BlockSpec((1, bc), lambda i, j: (0, j))],
        out_specs=pl.BlockSpec((1, t, bc), lambda i, j: (i, 0, j)),
        out_shape=jax.ShapeDtypeStruct((b, t, c), BF16),
        compiler_params=_cparams(("parallel", "parallel")),
        name="conv_silu",
    )(proj, conv_w, conv_b.reshape(1, c))


def _expand_heads(x_col):
    l = x_col.shape[0]
    lane_head = lax.broadcasted_iota(jnp.int32, (l, SSM_GROUP_W), 1) // SSM_HEAD_DIM
    out = jnp.zeros((l, SSM_GROUP_W), F32)
    for j in range(SSM_HPG):
        out = jnp.where(lane_head == j, x_col[:, j:j + 1], out)
    return out


def _ssd_kernel(*refs):
    ins, outs = refs[:16], refs[16:]

    @pl.when(pl.program_id(2) == 0)
    def _():
        for d in range(2):
            outs[2 * d + 1][...] = ins[8 * d + 7][...]

    for g in range(SSD_GROUPS_PER_STEP):
        for d in range(2):
            _ssd_chunk(*ins[8 * d:8 * d + 7], *outs[2 * d:2 * d + 2], g=g, reverse=d == 1)


def _ssd_chunk(xs_ref, bm_ref, cm_ref, dtc_ref, dtr_ref, pc_ref, pr_ref, y_ref, h_ref, *, g, reverse):
    l = SSM_CHUNK
    row = lax.broadcasted_iota(jnp.int32, (l, l), 0)
    col = lax.broadcasted_iota(jnp.int32, (l, l), 1)
    mask = (col >= row) if reverse else (col <= row)
    mask_t = (col <= row) if reverse else (col >= row)

    pc = pc_ref[0, g]
    pr = pr_ref[0, g]
    dt_c = _softplus(dtc_ref[0, 0, g] + pc[0:1, :])
    dt_r = _softplus(dtr_ref[0, 0, g] + pr[:, 0:1])
    a_c = dt_c * pc[1:2, :]
    a_r = dt_r * pr[:, 1:2]
    total_c = jnp.sum(a_c, axis=0, keepdims=True)

    xsl = slice(g * SSM_GROUP_W, (g + 1) * SSM_GROUP_W)
    nsl = slice(g * SSM_STATE, (g + 1) * SSM_STATE)
    xs = xs_ref[0, :, xsl].astype(F32)
    bm = bm_ref[0, :, nsl]
    cm = cm_ref[0, :, nsl]
    xdt = xs * _expand_heads(dt_c)
    cb = lax.dot_general(cm, bm, (((1,), (1,)), ((), ())), preferred_element_type=F32)
    hst = h_ref[0, g]
    inter = jnp.dot(cm, hst.astype(BF16), preferred_element_type=F32)

    lane_head = lax.broadcasted_iota(jnp.int32, (l, SSM_GROUP_W), 1) // SSM_HEAD_DIM
    y = jnp.zeros((l, SSM_GROUP_W), F32)
    cum_cols = []
    for j in range(SSM_HPG):
        cum_c = jnp.sum(jnp.where(mask, a_r[j:j + 1, :], 0.0), axis=1, keepdims=True)
        cum_r = jnp.sum(jnp.where(mask_t, a_c[:, j:j + 1], 0.0), axis=0, keepdims=True)
        cum_cols.append(cum_c)
        decay = jnp.exp(jnp.where(mask, cum_c - cum_r, -jnp.inf))
        wj = (decay * cb).astype(BF16)
        xj = jnp.where(lane_head == j, xdt, 0.0).astype(BF16)
        y = y + jnp.dot(wj, xj, preferred_element_type=F32)
    cum_all = jnp.concatenate(cum_cols, axis=1)
    y = y + jnp.exp(_expand_heads(cum_all)) * inter
    y_ref[0, :, xsl] = y.astype(y_ref.dtype)

    w_end = jnp.exp(total_c - cum_all)
    xw = (xdt * _expand_heads(w_end)).astype(BF16)
    upd = lax.dot_general(bm, xw, (((0,), (0,)), ((), ())), preferred_element_type=F32)
    h_ref[0, g] = jnp.exp(_expand_heads(total_c)) * hst + upd


def ssd_scan(xbc, dt_col, dt_row, par_col, par_row, states, *, name):
    b, t, _ = xbc.shape
    l = SSM_CHUNK
    nc = t // l
    gp = SSD_GROUPS_PER_STEP
    xw, nw = gp * SSM_GROUP_W, gp * SSM_STATE
    bmb = SSM_INNER // nw
    cmb = bmb + SSM_GROUPS // gp
    st_spec = pl.BlockSpec((1, gp, SSM_STATE, SSM_GROUP_W), lambda i, g, c: (i, g, 0, 0))
    in_specs, args, out_specs, out_shape = [], [], [], []
    for d in range(2):
        ci = functools.partial(_chunk_index, nc=nc, reverse=d == 1)
        in_specs += [
            pl.BlockSpec((1, l, xw), lambda i, g, c, ci=ci: (i, ci(c), g)),
            pl.BlockSpec((1, l, nw), lambda i, g, c, ci=ci: (i, ci(c), bmb + g)),
            pl.BlockSpec((1, l, nw), lambda i, g, c, ci=ci: (i, ci(c), cmb + g)),
            pl.BlockSpec((1, 1, gp, l, SSM_HPG), lambda i, g, c, ci=ci, d=d: (d, i, g, ci(c), 0)),
            pl.BlockSpec((1, 1, gp, SSM_HPG, l), lambda i, g, c, ci=ci, d=d: (d, i, g, 0, ci(c))),
            pl.BlockSpec((1, gp, 2, SSM_HPG), lambda i, g, c, d=d: (d, g, 0, 0)),
            pl.BlockSpec((1, gp, SSM_HPG, 2), lambda i, g, c, d=d: (d, g, 0, 0)),
            st_spec,
        ]
        args += [xbc, xbc, xbc, dt_col, dt_row, par_col, par_row, states[d]]
        out_specs += [pl.BlockSpec((1, l, xw), lambda i, g, c, ci=ci: (i, ci(c), g)), st_spec]
        out_shape += [jax.ShapeDtypeStruct((b, t, SSM_INNER), BF16),
                      jax.ShapeDtypeStruct(states[d].shape, F32)]
    yf, hf, yb, hb = pl.pallas_call(
        _ssd_kernel,
        grid=(b, SSM_GROUPS // gp, nc),
        in_specs=in_specs,
        out_specs=out_specs,
        out_shape=out_shape,
        compiler_params=_cparams(("parallel", "parallel", "arbitrary")),
        name=name,
    )(*args)
    return (yf, yb), (hf, hb)


def _even_out_kernel(mlf_ref, mlb_ref, ssdf_ref, ssdb_ref, o_ref, z_ref, xs_ref, mlg_ref, dsk_ref, ssg_ref,
                     out_ref):
    for h in range(ML_HEADS):
        sl = slice(h * ML_V_DIM, (h + 1) * ML_V_DIM)
        y = mlf_ref[0, :, sl].astype(F32) + mlb_ref[0, :, sl].astype(F32)
        og = o_ref[0, :, sl].astype(F32)
        out_ref[0, :, sl] = (_rms(y) * mlg_ref[:, sl] * jax.nn.sigmoid(og)).astype(out_ref.dtype)
    for g in range(SSM_GROUPS):
        sl = slice(g * SSM_GROUP_W, (g + 1) * SSM_GROUP_W)
        z = z_ref[0, :, sl].astype(F32)
        y = ((ssdf_ref[0, :, sl].astype(F32) + ssdb_ref[0, :, sl].astype(F32))
             + dsk_ref[:, sl] * xs_ref[0, :, sl].astype(F32))
        y = y * (z * jax.nn.sigmoid(z))
        so = slice(ML_V + g * SSM_GROUP_W, ML_V + (g + 1) * SSM_GROUP_W)
        out_ref[0, :, so] = (_rms(y) * ssg_ref[:, sl]).astype(out_ref.dtype)


def even_out(ml_y, ssd_y, proj, xbc, ml_norm_g, d_skip_full, ssm_norm_g, *, bt=128):
    b, t, _ = ml_y[0].shape
    bt = min(bt, t)
    w = ML_V
    tok = lambda blk: pl.BlockSpec((1, bt, w), lambda i, j: (i, j, blk))
    vec = pl.BlockSpec((1, w), lambda i, j: (0, 0))
    return pl.pallas_call(
        _even_out_kernel,
        grid=(b, t // bt),
        in_specs=[tok(0), tok(0), tok(0), tok(0), tok(E_O // w), tok(E_Z // w), tok(0), vec, vec, vec],
        out_specs=pl.BlockSpec((1, bt, 2 * w), lambda i, j: (i, j, 0)),
        out_shape=jax.ShapeDtypeStruct((b, t, 2 * w), BF16),
        compiler_params=_cparams(("parallel", "parallel")),
        name="even_gate",
    )(ml_y[0], ml_y[1], ssd_y[0], ssd_y[1], proj, proj, xbc, ml_norm_g.reshape(1, w), d_skip_full.reshape(1, w),
      ssm_norm_g.reshape(1, w))


def _hg_constants(reverse):
    l = HG_CHUNK
    pos = np.arange(l)
    tri = (pos[None, :] <= pos[:, None]).astype(np.float32)
    masks = []
    for lv in range(HG_LEVELS):
        m = 1 << lv
        blk = pos // (2 * m)
        later = (pos % (2 * m)) >= m
        masks.append(((blk[:, None] == blk[None, :]) & later[:, None] & (~later)[None, :]).astype(np.float32))
    masks = np.stack(masks)
    if reverse:
        tri = tri[::-1, ::-1]
        masks = masks[:, ::-1, ::-1]
    return (jnp.asarray(tri, BF16), jnp.asarray(masks.reshape(-1, l), F32))


def _split3(x):
    hi = x.astype(BF16)
    r1 = x - hi.astype(F32)
    mid = r1.astype(BF16)
    lo = (r1 - mid.astype(F32)).astype(BF16)
    return hi, mid, lo


def _hg_level_factors(q, k, f, cum, reverse):
    l, w = q.shape
    row = lax.broadcasted_iota(jnp.int32, (l, w), 0)
    sub = lax.broadcasted_iota(jnp.int32, (8, w), 0)
    zs = []
    for lv in range(HG_LEVELS):
        m = 1 << lv
        late = ((row & m) == 0) if reverse else ((row & m) != 0)
        if m >= 8:
            pieces = []
            for b0 in range(0, l, 2 * m):
                lo_rows, hi_rows = slice(b0, b0 + m), slice(b0 + m, b0 + 2 * m)
                if reverse:
                    mid = cum[b0 + m:b0 + m + 1]
                    pieces.append(q[lo_rows] * jnp.exp(cum[lo_rows] - mid))
                    pieces.append(k[hi_rows] * jnp.exp(mid - cum[hi_rows]))
                else:
                    mid = cum[b0 + m - 1:b0 + m]
                    pieces.append(k[lo_rows] * jnp.exp(mid - cum[lo_rows]))
                    pieces.append(q[hi_rows] * jnp.exp(cum[hi_rows] - mid))
            zs.append(jnp.concatenate(pieces, axis=0).astype(BF16))
            continue
        if m == 1:
            ex = jnp.where(late, f, 1.0)
        else:
            pieces = []
            if m >= 4:
                for b0 in range(0, l, 2 * m):
                    r = b0 + (m if reverse else m - 1)
                    pieces.append(cum[b0:b0 + 2 * m] - cum[r:r + 1])
            else:
                for b0 in range(0, l, 8):
                    ra, rb = (b0 + 2, b0 + 6) if reverse else (b0 + 1, b0 + 5)
                    mid = jnp.where(sub < 4, cum[ra:ra + 1], cum[rb:rb + 1])
                    pieces.append(cum[b0:b0 + 8] - mid)
            dd = pieces[0] if len(pieces) == 1 else jnp.concatenate(pieces, axis=0)
            ex = jnp.exp(jnp.where(late, dd, -dd))
        zs.append((jnp.where(late, q, k) * ex).astype(BF16))
    return zs


def _hg_chunk(q_ref, f_ref, i_ref, lb, tri, masks_ref, y_ref, s_ref, r0, reverse):
    l, e = HG_CHUNK, HG_EXPAND
    q_raw = q_ref[0, pl.ds(r0, l), :].astype(F32)
    f_raw = f_ref[0, pl.ds(r0, l), :].astype(F32)
    v = i_ref[0, pl.ds(r0, l), :]
    f = lb + (1.0 - lb) * jax.nn.sigmoid(f_raw)
    k = 1.0 - f
    lf = jnp.log(f)
    q = q_raw * jax.nn.sigmoid(q_raw) * (e ** -0.5)
    hi, mid, lo = _split3(lf)
    cum = (jnp.dot(tri, hi, preferred_element_type=F32)
           + jnp.dot(tri, mid, preferred_element_type=F32)
           + jnp.dot(tri, lo, preferred_element_type=F32))
    zs = _hg_level_factors(q, k, f, cum, reverse)
    end_row = 0 if reverse else l - 1
    cum_end = cum[end_row:end_row + 1, :]
    qc = (q * jnp.exp(cum)).astype(BF16)
    kt_end = (k * jnp.exp(cum_end - cum)).astype(BF16)
    s_decay = jnp.exp(cum_end)
    qk = q * k
    row = lax.broadcasted_iota(jnp.int32, (l, l), 0)
    col = lax.broadcasted_iota(jnp.int32, (l, l), 1)
    eye = row == col
    nt = (((1,), (1,)), ((), ()))
    tn = (((0,), (0,)), ((), ()))
    for hh in range(HG_HEADS_PER_STEP):
        sl = slice(hh * e, (hh + 1) * e)
        att = jnp.where(eye, jnp.sum(qk[:, sl], axis=1, keepdims=True), 0.0)
        for lv in range(HG_LEVELS):
            z = zs[lv][:, sl]
            a_lv = lax.dot_general(z, z, nt, preferred_element_type=F32)
            att = att + masks_ref[pl.ds(lv * l, l), :] * a_lv
        st = s_ref[0, hh]
        vh = v[:, sl]
        o = (jnp.dot(att.astype(BF16), vh, preferred_element_type=F32)
             + lax.dot_general(qc[:, sl], st.astype(BF16), nt, preferred_element_type=F32))
        y_ref[0, pl.ds(r0, l), sl] = o.astype(y_ref.dtype)
        upd = lax.dot_general(vh, kt_end[:, sl], tn, preferred_element_type=F32)
        s_ref[0, hh] = s_decay[:, sl] * st + upd


def _hg_kernel(qf_ref, ff_ref, if_ref, qb_ref, fb_ref, ib_ref, lbf_ref, lbb_ref, trif_ref, trib_ref,
               mf_ref, mb_ref, sf0_ref, sb0_ref, yf_ref, yb_ref, sf_ref, sb_ref, *, n_inner):
    l = HG_CHUNK

    @pl.when(pl.program_id(2) == 0)
    def _():
        sf_ref[...] = sf0_ref[...]
        sb_ref[...] = sb0_ref[...]

    lbf = lbf_ref[0]
    lbb = lbb_ref[0]
    trif = trif_ref[...]
    trib = trib_ref[...]

    def chunk(ci, carry):
        rf = pl.multiple_of(ci * l, l)
        rb = pl.multiple_of((n_inner - 1 - ci) * l, l)
        _hg_chunk(qf_ref, ff_ref, if_ref, lbf, trif, mf_ref, yf_ref, sf_ref, rf, False)
        _hg_chunk(qb_ref, fb_ref, ib_ref, lbb, trib, mb_ref, yb_ref, sb_ref, rb, True)
        return carry

    lax.fori_loop(0, n_inner, chunk, 0, unroll=4)


def hgrn2_scan(proj, lb, states, *, tb=512, name):
    b, t, _ = proj.shape
    tb = min(tb, t)
    nb = t // tb
    n_inner = tb // HG_CHUNK
    hw = HG_HEADS_PER_STEP * HG_EXPAND
    nh = HG_HEADS // HG_HEADS_PER_STEP
    qb_, ffb, fbb, ib_ = O_Q // hw, O_F // hw, (O_F + HG_F) // hw, O_I // hw
    trif, mf = _hg_constants(False)
    trib, mb = _hg_constants(True)
    fwd = lambda blk: pl.BlockSpec((1, tb, hw), lambda i, h, c: (i, c, blk + h))
    bwd = lambda blk: pl.BlockSpec((1, tb, hw), lambda i, h, c: (i, nb - 1 - c, blk + h))
    const = lambda a: pl.BlockSpec(a.shape, lambda i, h, c: (0,) * a.ndim)
    st_spec = pl.BlockSpec((1, HG_HEADS_PER_STEP, HG_V_DIM, HG_EXPAND), lambda i, h, c: (i, h, 0, 0))
    in_specs = [fwd(qb_), fwd(ffb), fwd(ib_), bwd(qb_), bwd(fbb), bwd(ib_),
                pl.BlockSpec((1, 1, hw), lambda i, h, c: (0, 0, h)),
                pl.BlockSpec((1, 1, hw), lambda i, h, c: (1, 0, h)),
                const(trif), const(trib), const(mf), const(mb), st_spec, st_spec]
    yf, yb, sf, sb = pl.pallas_call(
        functools.partial(_hg_kernel, n_inner=n_inner),
        grid=(b, nh, nb),
        in_specs=in_specs,
        out_specs=[fwd(0), bwd(0), st_spec, st_spec],
        out_shape=[jax.ShapeDtypeStruct((b, t, D_MODEL), BF16),
                   jax.ShapeDtypeStruct((b, t, D_MODEL), BF16),
                   jax.ShapeDtypeStruct(states[0].shape, F32),
                   jax.ShapeDtypeStruct(states[1].shape, F32)],
        compiler_params=_cparams(("parallel", "parallel", "arbitrary")),
        name=name,
    )(proj, proj, proj, proj, proj, proj, lb, lb, trif, trib, mf, mb, states[0], states[1])
    return (yf, yb), (sf, sb)


def _hg_out_kernel(of_ref, ob_ref, g_ref, ng_ref, out_ref):
    for h in range(of_ref.shape[-1] // HG_V_DIM):
        sl = slice(h * HG_V_DIM, (h + 1) * HG_V_DIM)
        g = g_ref[0, :, sl].astype(F32)
        o = of_ref[0, :, sl].astype(F32) + ob_ref[0, :, sl].astype(F32)
        out_ref[0, :, sl] = (_rms(o) * ng_ref[:, sl] * (g * jax.nn.sigmoid(g))).astype(out_ref.dtype)


def hg_out(o_f, o_b, proj, norm_g, *, bt=256, bc=1024, name):
    b, t, d = o_f.shape
    bt = min(bt, t)
    gb = O_G // bc
    tok = pl.BlockSpec((1, bt, bc), lambda i, j, c: (i, j, c))
    return pl.pallas_call(
        _hg_out_kernel,
        grid=(b, t // bt, d // bc),
        in_specs=[tok, tok,
                  pl.BlockSpec((1, bt, bc), lambda i, j, c: (i, j, gb + c)),
                  pl.BlockSpec((1, bc), lambda i, j, c: (0, c))],
        out_specs=tok,
        out_shape=jax.ShapeDtypeStruct((b, t, d), BF16),
        compiler_params=_cparams(("parallel", "parallel", "parallel")),
        name=name,
    )(o_f, o_b, proj, norm_g.reshape(1, d))


def even_mixer(a_c, a_l, w_main, w_small, w_out, layer, ml_gate_b, ml_norm_g, conv_w, conv_b,
               dt_bias, a_log, d_skip, ssm_norm_g, need_ctx):
    bsz = a_l.shape[0]

    def features(a, tag):
        b, t, d = a.shape
        bm = min(1024, b * t)
        a2 = a.reshape(b * t, d)
        proj = matmul(a2, w_main, bm=bm, bn=1024, out_dtype=BF16, name="ab_in_" + tag).reshape(b, t, E_MAIN)
        small = matmul(a2, w_small, bm=bm, bn=128, out_dtype=F32, name="ab_in_small_" + tag).reshape(b, t, 128)
        gates = small[..., :16].reshape(b, t, 2, 2, ML_HEADS) + ml_gate_b.astype(F32)
        g_col = gates.transpose(2, 0, 4, 1, 3)
        g_row = gates.transpose(2, 0, 4, 3, 1)
        dt = small[..., 16:16 + 2 * SSM_HEADS].reshape(b, t, 2, SSM_GROUPS, SSM_HPG)
        dt_col = dt.transpose(2, 0, 3, 1, 4)
        dt_row = dt.transpose(2, 0, 3, 4, 1)
        xbc = conv_silu(proj, conv_w, conv_b)
        return dict(proj=proj, xbc=xbc, g_col=g_col, g_row=g_row, dt_col=dt_col, dt_row=dt_row)

    fc, fl = features(a_c, "ctx"), features(a_l, "lat")
    neg_a = -jnp.exp(a_log.astype(F32))
    par = jnp.stack([dt_bias.astype(F32), neg_a], axis=1).reshape(2, 2, SSM_GROUPS, SSM_HPG)
    par_col = par.transpose(0, 2, 1, 3)
    par_row = par.transpose(0, 2, 3, 1)

    ml_init = (jnp.zeros((bsz, ML_HEADS, ML_QK_DIM, ML_V_DIM), F32),
               jnp.zeros((bsz, ML_HEADS, 1, ML_QK_DIM), F32),
               jnp.zeros((bsz, ML_HEADS, 1, 1), F32))
    ssd_init = jnp.zeros((bsz, SSM_GROUPS, SSM_STATE, SSM_GROUP_W), F32)

    def ml(f, st, tag):
        return mlstm_scan(f['proj'], f['g_col'], f['g_row'], st, name="mlstm_scan_" + tag)

    def ssd(f, st, tag):
        return ssd_scan(f['xbc'], f['dt_col'], f['dt_row'], par_col, par_row, st, name="ssd_scan_" + tag)

    ml_c, ml_st = ml(fc, (ml_init, ml_init), "ctx")
    ml_l, _ = ml(fl, ml_st, "lat")
    ssd_c, ssd_st = ssd(fc, (ssd_init, ssd_init), "ctx")
    ssd_l, _ = ssd(fl, ssd_st, "lat")
    d_full = jnp.repeat(d_skip.astype(F32), SSM_HEAD_DIM)

    def output(f, ml_y, ssd_y, tag):
        b, t, _ = ml_y[0].shape
        cat = even_out(ml_y, ssd_y, f['proj'], f['xbc'], ml_norm_g, d_full, ssm_norm_g)
        return dense(cat.reshape(b * t, D_MODEL), w_out, layer, out_dtype=F32,
                     name="ab_out_" + tag).reshape(b, t, D_MODEL)

    y_ctx = output(fc, ml_c, ssd_c, "ctx") if need_ctx else None
    return y_ctx, output(fl, ml_l, ssd_l, "lat")


def hgrn2_mixer(a_c, a_l, w_in, w_out, layer, lb, norm_g, need_ctx, lat_colmajor=True):
    bsz = a_l.shape[0]
    lb3 = lb.astype(F32).reshape(2, 1, HG_F)

    def features(a, tag):
        b, t, d = a.shape
        return dense(a.reshape(b * t, d), w_in, layer, out_dtype=BF16,
                     name="hg_in_" + tag).reshape(b, t, 5 * D_MODEL)

    pc, pl_ = features(a_c, "ctx"), features(a_l, "lat")
    init = jnp.zeros((bsz, HG_HEADS, HG_V_DIM, HG_EXPAND), F32)
    o_c, st = hgrn2_scan(pc, lb3, (init, init), name="hg_scan_ctx")
    o_l, _ = hgrn2_scan(pl_, lb3, st, name="hg_scan_lat")

    def output(p, o, colmajor, tag):
        b, t, _ = o[0].shape
        gated = hg_out(o[0], o[1], p, norm_g, name="hg_gate_" + tag)
        if colmajor:
            gated = grid_from_colmajor(gated)
        return dense(gated.reshape(b * t, D_MODEL), w_out, layer, out_dtype=F32,
                     name="hg_out_" + tag).reshape(b, t, D_MODEL)

    y_ctx = output(pc, o_c, False, "ctx") if need_ctx else None
    return y_ctx, output(pl_, o_l, lat_colmajor, "lat")


def ffn(a, wg, wu, wd, layer, tag):
    b, t, d = a.shape
    m = b * t
    a2 = a.reshape(m, d)
    if m > 2048:
        hid = matmul_stream(a2, (wg, wu), layer, bm=2048, bn=256, out_dtype=BF16, name="ffn_up_" + tag)
    else:
        hid = swiglu_single(a2, wg, wu, layer, bn=256, name="ffn_up_" + tag)
    return dense(hid, wd, layer, out_dtype=F32, name="ffn_down_" + tag, bm=512, bn=512).reshape(b, t, d)


def grid_to_colmajor(h):
    bsz, t, d = h.shape
    rows = t // GRID_W
    return h.reshape(bsz, rows, GRID_W, d).transpose(0, 2, 1, 3).reshape(bsz, t, d)


def grid_from_colmajor(h):
    bsz, t, d = h.shape
    rows = t // GRID_W
    return h.reshape(bsz, GRID_W, rows, d).transpose(0, 2, 1, 3).reshape(bsz, t, d)


def kernel(x, c, ctx, c_ctx, ada_w, ada_b, norm_g, ffn_w_gate, ffn_w_up, ffn_w_down,
           ab_w_in, ab_w_out, ml_gate_b, ml_norm_g, ssm_conv_w, ssm_conv_b, ssm_dt_bias,
           ssm_a_log, ssm_d, ssm_norm_g, hg_w_in, hg_w_out, hg_lb, hg_norm_g):
    depth = ada_w.shape[0]
    bsz = x.shape[0]
    d = D_MODEL
    lb_p = jax.nn.softmax(hg_lb.astype(F32), axis=1)
    lower_bounds = jnp.cumsum(lb_p, axis=1) - lb_p[:, :1]
    cond = jnp.concatenate([jax.nn.silu(c), jax.nn.silu(c_ctx)[None, :],
                            jnp.zeros((8 - bsz - 1, d), F32)], axis=0)
    h_lat, h_ctx = x, ctx
    y_l = y_c = f_l = f_c = None
    gate_l = gate_c = g_prev = None
    for layer in range(depth):
        need_ctx = layer < depth - 1
        j = layer // 2
        mod = matmul(cond, ada_w, bm=8, bn=512, out_dtype=F32, bias=ada_b[layer], layer=layer,
                     name="ada_mod")
        mod_l = [mod[:bsz, i * d:(i + 1) * d][:, None, :] for i in range(6)]
        mod_c = [jnp.broadcast_to(mod[bsz, i * d:(i + 1) * d][None, None, :], (bsz, 1, d)) for i in range(6)]
        g = norm_g[layer]
        h_lat, a_l = resid_norm(h_lat, f_l, gate_l, g_prev, g[0], mod_l[0], mod_l[1])
        h_ctx, a_c = resid_norm(h_ctx, f_c, gate_c, g_prev, g[0], mod_c[0], mod_c[1])
        if layer % 2 == 0:
            w_in = ab_w_in[j]
            w_main = jnp.concatenate(
                [w_in[:, :6144], w_in[:, 6160:6160 + 2048 + SSM_CONV_DIM]], axis=1).astype(BF16)
            w_small = jnp.concatenate(
                [w_in[:, 6144:6160], w_in[:, 12304:12368], jnp.zeros((d, 128 - 80), F32)], axis=1).astype(BF16)
            y_c, y_l = even_mixer(a_c, a_l, w_main, w_small, ab_w_out, j, ml_gate_b[j],
                                  ml_norm_g[j], ssm_conv_w[j], ssm_conv_b[j], ssm_dt_bias[j],
                                  ssm_a_log[j], ssm_d[j], ssm_norm_g[j], need_ctx)
        else:
            y_c, y_l = hgrn2_mixer(a_c, grid_to_colmajor(a_l), hg_w_in, hg_w_out, j,
                                   lower_bounds[:, layer], hg_norm_g[j], need_ctx)
        wd = ffn_w_down
        h_lat, a2_l = resid_norm(h_lat, y_l, mod_l[2], g[1], g[2], mod_l[3], mod_l[4])
        f_l = ffn(a2_l, ffn_w_gate, ffn_w_up, wd, layer, "lat")
        gate_l = mod_l[5]
        if need_ctx:
            h_ctx, a2_c = resid_norm(h_ctx, y_c, mod_c[2], g[1], g[2], mod_c[3], mod_c[4])
            f_c = ffn(a2_c, ffn_w_gate, ffn_w_up, wd, layer, "ctx")
            gate_c = mod_c[5]
        else:
            f_c = gate_c = None
        g_prev = g[3]
    h_lat, _ = resid_norm(h_lat, f_l, gate_l, g_prev)
    return h_lat
```

```python
import functools
import math

import jax
import jax.numpy as jnp
import numpy as np
from jax import lax
from jax.experimental import pallas as pl
from jax.experimental.pallas import tpu as pltpu

F32 = jnp.float32
BF16 = jnp.bfloat16

D_MODEL = 4096
GRID_W = 64
EPS = 1e-6

ML_HEADS = 4
ML_QK_DIM = 256
ML_V_DIM = 512
ML_QK = ML_HEADS * ML_QK_DIM
ML_V = ML_HEADS * ML_V_DIM
ML_CHUNK = 128
GATE_CAP = 15.0

SSM_HEAD_DIM = 64
SSM_INNER = 2048
SSM_HEADS = 32
SSM_GROUPS = 8
SSM_HPG = 4
SSM_STATE = 128
SSM_CONV = 5
SSM_CHUNK = 128
SSM_GROUP_W = SSM_HPG * SSM_HEAD_DIM
SSM_CONV_DIM = SSM_INNER + 2 * SSM_GROUPS * SSM_STATE
SSD_GROUPS_PER_STEP = 4

HG_EXPAND = 128
HG_HEADS = 32
HG_F = HG_HEADS * HG_EXPAND
HG_V_DIM = 128
HG_CHUNK = 64
HG_LEVELS = 6
HG_HEADS_PER_STEP = 4

FFN_HIDDEN = 11008

E_Q, E_K, E_V, E_O, E_Z, E_XBC = 0, 1024, 2048, 4096, 6144, 8192
E_MAIN = 12288
O_Q, O_F, O_I, O_G = 0, 4096, 12288, 16384

VMEM_LIMIT = 56 * 1024 * 1024


def _cparams(sem):
    return pltpu.CompilerParams(dimension_semantics=sem, vmem_limit_bytes=VMEM_LIMIT)


def _mm_kernel(*refs, has_bias, cast_w):
    x_ref, w_ref = refs[0], refs[1]
    b_ref = refs[2] if has_bias else None
    o_ref = refs[2 + has_bias]
    if cast_w:
        wb_ref = refs[3 + has_bias]

        @pl.when(pl.program_id(1) == 0)
        def _():
            wb_ref[...] = w_ref[...].astype(BF16)

        w = wb_ref[...]
    else:
        w = w_ref[...]
    acc = jnp.dot(x_ref[...].astype(BF16), w, preferred_element_type=F32)
    if has_bias:
        acc = acc + b_ref[...]
    o_ref[...] = acc.astype(o_ref.dtype)


def matmul(x, w, *, bm, bn, out_dtype, name, bias=None, layer=None):
    m, k = x.shape
    n = w.shape[-1]
    assert m % bm == 0 and n % bn == 0, (m, bm, n, bn)
    if layer is None:
        w_spec = pl.BlockSpec((k, bn), lambda j, i: (0, j))
    else:
        w_spec = pl.BlockSpec((None, k, bn), lambda j, i: (layer, 0, j))
    in_specs = [pl.BlockSpec((bm, k), lambda j, i: (i, 0)), w_spec]
    args = [x, w]
    if bias is not None:
        in_specs.append(pl.BlockSpec((1, bn), lambda j, i: (0, j)))
        args.append(bias.reshape(1, n))
    cast_w = w.dtype != BF16
    return pl.pallas_call(
        functools.partial(_mm_kernel, has_bias=bias is not None, cast_w=cast_w),
        grid=(n // bn, m // bm),
        in_specs=in_specs,
        out_specs=pl.BlockSpec((bm, bn), lambda j, i: (i, j)),
        out_shape=jax.ShapeDtypeStruct((m, n), out_dtype),
        scratch_shapes=[pltpu.VMEM((k, bn), BF16)] if cast_w else [],
        compiler_params=_cparams(("parallel", "arbitrary" if cast_w else "parallel")),
        name=name,
    )(*args)


def _swiglu(g, u):
    return g * jax.nn.sigmoid(g) * u


def _mm_stream_kernel(*refs, n_w, nj, kc):
    x_ref, w_refs, o_ref, wb_refs = refs[0], refs[1:1 + n_w], refs[1 + n_w], refs[2 + n_w:]
    j, i = pl.program_id(0), pl.program_id(1)
    slot = j % 2

    @pl.when(j < nj)
    def _():
        r0 = pl.multiple_of(i * kc, 16)
        for w_ref, wb_ref in zip(w_refs, wb_refs):
            wb_ref[slot, pl.ds(r0, kc), :] = w_ref[...].astype(BF16)

    @pl.when(j > 0)
    def _():
        x = x_ref[...]
        accs = [jnp.dot(x, wb_ref[1 - slot], preferred_element_type=F32) for wb_ref in wb_refs]
        out = _swiglu(*accs) if n_w == 2 else accs[0]
        o_ref[...] = out.astype(o_ref.dtype)


def matmul_stream(x, ws, layer, *, bm, bn, out_dtype, name):
    m, k = x.shape
    n = ws[0].shape[-1]
    mi, nj = m // bm, n // bn
    kc = k // mi
    assert m % bm == 0 and n % bn == 0 and k % mi == 0 and kc % 16 == 0 and mi > 1, (m, bm, n, bn, k)
    row = lambda j, i: jnp.where(j == 0, 0, i)
    w_spec = pl.BlockSpec((None, kc, bn),
                          lambda j, i: (layer, jnp.where(j < nj, i, mi - 1), jnp.minimum(j, nj - 1)))
    return pl.pallas_call(
        functools.partial(_mm_stream_kernel, n_w=len(ws), nj=nj, kc=kc),
        grid=(nj + 1, mi),
        in_specs=[pl.BlockSpec((bm, k), lambda j, i: (row(j, i), 0))] + [w_spec] * len(ws),
        out_specs=pl.BlockSpec((bm, bn), lambda j, i: (row(j, i), jnp.maximum(j - 1, 0))),
        out_shape=jax.ShapeDtypeStruct((m, n), out_dtype),
        scratch_shapes=[pltpu.VMEM((2, k, bn), BF16) for _ in ws],
        compiler_params=_cparams(("arbitrary", "arbitrary")),
        name=name,
    )(x, *ws)


def _swiglu_kernel(x_ref, wg_ref, wu_ref, o_ref):
    x = x_ref[...]
    g = jnp.dot(x, wg_ref[...].astype(BF16), preferred_element_type=F32)
    u = jnp.dot(x, wu_ref[...].astype(BF16), preferred_element_type=F32)
    o_ref[...] = _swiglu(g, u).astype(o_ref.dtype)


def swiglu_single(x, wg, wu, layer, *, bn, name):
    m, k = x.shape
    n = wg.shape[-1]
    assert n % bn == 0
    w_spec = pl.BlockSpec((None, k, bn), lambda j: (layer, 0, j))
    return pl.pallas_call(
        _swiglu_kernel,
        grid=(n // bn,),
        in_specs=[pl.BlockSpec((m, k), lambda j: (0, 0)), w_spec, w_spec],
        out_specs=pl.BlockSpec((m, bn), lambda j: (0, j)),
        out_shape=jax.ShapeDtypeStruct((m, n), BF16),
        compiler_params=_cparams(("parallel",)),
        name=name,
    )(x, wg, wu)


def dense(x, w, layer, *, out_dtype, name, bm=1024, bn=1024):
    m = x.shape[0]
    if m > bm:
        return matmul_stream(x, (w,), layer, bm=bm, bn=bn, out_dtype=out_dtype, name=name)
    return matmul(x, w, bm=m, bn=256, out_dtype=out_dtype, layer=layer, name=name)


def _rms(x):
    return x * lax.rsqrt(jnp.mean(x * x, axis=-1, keepdims=True) + EPS)


def _resid_norm_kernel(*refs, has_y, want_a):
    it = iter(refs)
    h_ref = next(it)
    if has_y:
        y_ref, gate_ref, g1_ref = next(it), next(it), next(it)
    if want_a:
        g2_ref, shift_ref, scale_ref = next(it), next(it), next(it)
    if has_y:
        hout_ref = next(it)
    if want_a:
        a_ref = next(it)
    h = h_ref[0]
    if has_y:
        h = h + gate_ref[0] * (_rms(y_ref[0].astype(F32)) * g1_ref[...])
        hout_ref[0] = h
    if want_a:
        a = (_rms(h) * g2_ref[...]) * (1.0 + scale_ref[0]) + shift_ref[0]
        a_ref[0] = a.astype(a_ref.dtype)


def resid_norm(h, y=None, gate=None, g1=None, g2=None, shift=None, scale=None, *, bt=128):
    b, t, d = h.shape
    has_y, want_a = y is not None, g2 is not None
    bt = min(bt, t)
    tok = pl.BlockSpec((1, bt, d), lambda i, j: (i, j, 0))
    per_b = pl.BlockSpec((1, 1, d), lambda i, j: (i, 0, 0))
    vec = pl.BlockSpec((1, d), lambda i, j: (0, 0))
    args, in_specs, out_shape, out_specs = [h], [tok], [], []
    if has_y:
        args += [y, gate, g1.reshape(1, d)]
        in_specs += [tok, per_b, vec]
        out_shape.append(jax.ShapeDtypeStruct((b, t, d), F32))
        out_specs.append(tok)
    if want_a:
        args += [g2.reshape(1, d), shift, scale]
        in_specs += [vec, per_b, per_b]
        out_shape.append(jax.ShapeDtypeStruct((b, t, d), BF16))
        out_specs.append(tok)
    outs = pl.pallas_call(
        functools.partial(_resid_norm_kernel, has_y=has_y, want_a=want_a),
        grid=(b, t // bt),
        in_specs=in_specs,
        out_specs=out_specs,
        out_shape=out_shape,
        compiler_params=_cparams(("parallel", "parallel")),
        name="resid_norm",
    )(*args)
    outs = list(outs)
    h_new = outs.pop(0) if has_y else h
    a = outs.pop(0) if want_a else None
    return h_new, a


def _softplus(x):
    return jnp.maximum(x, 0.0) + jnp.log1p(jnp.exp(-jnp.abs(x)))


def _log_sigmoid(x):
    return jnp.minimum(x, 0.0) - jnp.log1p(jnp.exp(-jnp.abs(x)))


def _chunk_index(c, nc, reverse):
    return (nc - 1 - c) if reverse else c


def _mlstm_kernel(*refs):
    ins, outs = refs[:16], refs[16:]

    @pl.when(pl.program_id(2) == 0)
    def _():
        for d in range(2):
            for s in range(3):
                outs[4 * d + 1 + s][...] = ins[8 * d + 5 + s][...]

    for d in range(2):
        _mlstm_chunk(*ins[8 * d:8 * d + 5], *outs[4 * d:4 * d + 4], reverse=d == 1)


def _mlstm_chunk(q_ref, k_ref, v_ref, gc_ref, gr_ref, y_ref, c_ref, n_ref, m_ref, *, reverse):
    l = ML_CHUNK
    row = lax.broadcasted_iota(jnp.int32, (l, l), 0)
    col = lax.broadcasted_iota(jnp.int32, (l, l), 1)
    mask = (col >= row) if reverse else (col <= row)
    mask_t = (col <= row) if reverse else (col >= row)

    gcol = gc_ref[0, 0, 0]
    grow = gr_ref[0, 0, 0]
    ig_c = GATE_CAP * jnp.tanh(gcol[:, 0:1] / GATE_CAP)
    lf_c = _log_sigmoid(GATE_CAP * jnp.tanh(gcol[:, 1:2] / GATE_CAP))
    ig_r = GATE_CAP * jnp.tanh(grow[0:1, :] / GATE_CAP)
    lf_r = _log_sigmoid(GATE_CAP * jnp.tanh(grow[1:2, :] / GATE_CAP))

    b_c = jnp.sum(jnp.where(mask, lf_r, 0.0), axis=1, keepdims=True)
    b_r = jnp.sum(jnp.where(mask_t, lf_c, 0.0), axis=0, keepdims=True)
    total = jnp.sum(lf_c, axis=0, keepdims=True)

    m_prev = m_ref[0, 0]
    logw = jnp.where(mask, b_c - b_r + ig_r, -jnp.inf)
    inter = b_c + m_prev
    m_t = jnp.maximum(jnp.max(logw, axis=1, keepdims=True), inter)

    q = q_ref[0]
    k = k_ref[0]
    v = v_ref[0]
    scale = ML_QK_DIM ** -0.5
    qk = lax.dot_general(q, k, (((1,), (1,)), ((), ())), preferred_element_type=F32)
    s = qk * scale * jnp.exp(logw - m_t)
    inter_w = jnp.exp(inter - m_t) * scale

    c_st = c_ref[0, 0]
    n_st = n_ref[0, 0]
    num = (jnp.dot(s.astype(BF16), v, preferred_element_type=F32)
           + inter_w * jnp.dot(q, c_st.astype(BF16), preferred_element_type=F32))
    qn = jnp.sum(q.astype(F32) * n_st, axis=1, keepdims=True)
    den = jnp.sum(s, axis=1, keepdims=True) + inter_w * qn
    y_ref[0] = (num * (1.0 / jnp.maximum(jnp.abs(den), jnp.exp(-m_t)))).astype(y_ref.dtype)

    g_c = total - b_c + ig_c
    g_r = total - b_r + ig_r
    m_new = jnp.maximum(total + m_prev, jnp.max(g_r, axis=1, keepdims=True))
    w_c = jnp.exp(g_c - m_new)
    decay = jnp.exp(total + m_prev - m_new)
    kw = k.astype(F32) * w_c
    kv = lax.dot_general(kw.astype(BF16), v, (((0,), (0,)), ((), ())), preferred_element_type=F32)
    c_ref[0, 0] = decay * c_st + kv
    n_ref[0, 0] = decay * n_st + jnp.sum(kw, axis=0, keepdims=True)
    m_ref[0, 0] = m_new


def mlstm_scan(proj, gates_col, gates_row, states, *, name):
    b, t, _ = proj.shape
    l = ML_CHUNK
    nc = t // l
    qb, kb, vb = E_Q // ML_QK_DIM, E_K // ML_QK_DIM, E_V // ML_V_DIM
    st_specs = [pl.BlockSpec((1, 1, ML_QK_DIM, ML_V_DIM), lambda i, h, c: (i, h, 0, 0)),
                pl.BlockSpec((1, 1, 1, ML_QK_DIM), lambda i, h, c: (i, h, 0, 0)),
                pl.BlockSpec((1, 1, 1, 1), lambda i, h, c: (i, h, 0, 0))]
    in_specs, args, out_specs, out_shape = [], [], [], []
    for d in range(2):
        ci = functools.partial(_chunk_index, nc=nc, reverse=d == 1)
        in_specs += [
            pl.BlockSpec((1, l, ML_QK_DIM), lambda i, h, c, ci=ci: (i, ci(c), qb + h)),
            pl.BlockSpec((1, l, ML_QK_DIM), lambda i, h, c, ci=ci: (i, ci(c), kb + h)),
            pl.BlockSpec((1, l, ML_V_DIM), lambda i, h, c, ci=ci: (i, ci(c), vb + h)),
            pl.BlockSpec((1, 1, 1, l, 2), lambda i, h, c, ci=ci, d=d: (d, i, h, ci(c), 0)),
            pl.BlockSpec((1, 1, 1, 2, l), lambda i, h, c, ci=ci, d=d: (d, i, h, 0, ci(c))),
        ] + st_specs
        args += [proj, proj, proj, gates_col, gates_row, *states[d]]
        out_specs += [pl.BlockSpec((1, l, ML_V_DIM), lambda i, h, c, ci=ci: (i, ci(c), h))] + st_specs
        out_shape += [jax.ShapeDtypeStruct((b, t, ML_V), BF16)]
        out_shape += [jax.ShapeDtypeStruct(s.shape, F32) for s in states[d]]
    outs = pl.pallas_call(
        _mlstm_kernel,
        grid=(b, ML_HEADS, nc),
        in_specs=in_specs,
        out_specs=out_specs,
        out_shape=out_shape,
        compiler_params=_cparams(("parallel", "parallel", "arbitrary")),
        name=name,
    )(*args)
    return (outs[0], outs[4]), (tuple(outs[1:4]), tuple(outs[5:8]))


CONV_ROWS = 256
CONV_HALO = 16


def _conv_kernel(x_ref, w_ref, b_ref, o_ref, *, t):
    w = w_ref[...]
    bias = b_ref[...]
    r = min(CONV_ROWS, t)
    half = SSM_CONV // 2
    cw = x_ref.shape[-1]
    for r0 in range(0, t, r):
        parts = []
        if r0 == 0:
            parts.append(jnp.zeros((CONV_HALO, cw), F32))
        else:
            parts.append(x_ref[0, pl.ds(r0 - CONV_HALO, CONV_HALO), :].astype(F32))
        parts.append(x_ref[0, pl.ds(r0, r), :].astype(F32))
        if r0 + r == t:
            parts.append(jnp.zeros((CONV_HALO, cw), F32))
        else:
            parts.append(x_ref[0, pl.ds(r0 + r, CONV_HALO), :].astype(F32))
        win = jnp.concatenate(parts, axis=0)
        n = r + 2 * CONV_HALO
        acc = jnp.zeros((r, cw), F32) + bias
        for kk in range(SSM_CONV):
            sh = (half - kk) % n
            rolled = win if sh == 0 else pltpu.roll(win, sh, 0)
            acc = acc + w[kk:kk + 1, :] * rolled[CONV_HALO:CONV_HALO + r, :]
        o_ref[0, pl.ds(r0, r), :] = (acc * jax.nn.sigmoid(acc)).astype(o_ref.dtype)


def conv_silu(proj, conv_w, conv_b, *, bc=512):
    b, t, _ = proj.shape
    c = SSM_CONV_DIM
    off = E_XBC // bc
    return pl.pallas_call(
        functools.partial(_conv_kernel, t=t),
        grid=(b, c // bc),
        in_specs=[pl.BlockSpec((1, t, bc), lambda i, j: (i, 0, off + j)),
                  pl.BlockSpec((SSM_CONV, bc), lambda i, j: (0, j)),
                  pl.BlockSpec((1, bc), lambda i, j: (0, j))],
        out_specs=pl.BlockSpec((1, t, bc), lambda i, j: (i, 0, j)),
        out_shape=jax.ShapeDtypeStruct((b, t, c), BF16),
        compiler_params=_cparams(("parallel", "parallel")),
        name="conv_silu",
    )(proj, conv_w, conv_b.reshape(1, c))


def _split3(x):
    hi = x.astype(BF16)
    r1 = x - hi.astype(F32)
    mid = r1.astype(BF16)
    lo = (r1 - mid.astype(F32)).astype(BF16)
    return hi, mid, lo


def _dot3(dims, parts_lhs, rhs):
    return sum(lax.dot_general(p, rhs, dims, preferred_element_type=F32) for p in parts_lhs)


def _head_selector(heads, width):
    rows = lax.broadcasted_iota(jnp.int32, (heads, heads * width), 0)
    lanes = lax.broadcasted_iota(jnp.int32, (heads, heads * width), 1)
    return (lanes // width == rows).astype(BF16)


def _ssd_kernel(*refs):
    ins, outs = refs[:12], refs[12:]

    @pl.when(pl.program_id(2) == 0)
    def _():
        for d in range(2):
            outs[2 * d + 1][...] = ins[6 * d + 5][...]

    for d in range(2):
        _ssd_chunk(*ins[6 * d:6 * d + 5], *outs[2 * d:2 * d + 2], reverse=d == 1)


def _ssd_chunk(xs_ref, bm_ref, cm_ref, dtr_ref, pr_ref, y_ref, h_ref, *, reverse):
    l = SSM_CHUNK
    gp = SSD_GROUPS_PER_STEP
    nh = gp * SSM_HPG
    row = lax.broadcasted_iota(jnp.int32, (l, l), 0)
    col = lax.broadcasted_iota(jnp.int32, (l, l), 1)
    mask = (col >= row) if reverse else (col <= row)
    mask_t = (col <= row) if reverse else (col >= row)

    nn = (((1,), (0,)), ((), ()))
    nt = (((1,), (1,)), ((), ()))
    tn = (((0,), (0,)), ((), ()))
    pr = pr_ref[0, 0]
    dt_r = _softplus(dtr_ref[0, 0, 0] + pr[:, 0:1])
    a_r = dt_r * pr[:, 1:2]
    cum_r = _dot3(nn, _split3(a_r), mask_t.astype(BF16))
    dt_parts, cum_parts = _split3(dt_r), _split3(cum_r)
    sel_ch = _head_selector(nh, SSM_HEAD_DIM)
    dt_ch = _dot3(tn, dt_parts, sel_ch)
    cum_ch = _dot3(tn, cum_parts, sel_ch)
    cum_bc = _dot3(tn, cum_parts, _head_selector(nh, l))
    end_row = 0 if reverse else l - 1
    total_ch = cum_ch[end_row:end_row + 1, :]

    xdt = xs_ref[0].astype(F32) * dt_ch
    e_cum = jnp.exp(cum_ch)
    xw = (xdt * jnp.exp(total_ch - cum_ch)).astype(BF16)
    s_decay = jnp.exp(total_ch)
    lane_head = lax.broadcasted_iota(jnp.int32, (l, SSM_GROUP_W), 1) // SSM_HEAD_DIM
    for g in range(gp):
        xsl = slice(g * SSM_GROUP_W, (g + 1) * SSM_GROUP_W)
        nsl = slice(g * SSM_STATE, (g + 1) * SSM_STATE)
        bm = bm_ref[0, :, nsl]
        cm = cm_ref[0, :, nsl]
        hst = h_ref[0, g]
        cb = lax.dot_general(cm, bm, nt, preferred_element_type=F32)
        inter = jnp.dot(cm, hst.astype(BF16), preferred_element_type=F32)
        xdt_g = xdt[:, xsl]
        ws, xjs = [], []
        for j in range(SSM_HPG):
            hd = g * SSM_HPG + j
            decay = jnp.exp(jnp.where(mask, cum_bc[:, hd * l:(hd + 1) * l] - cum_r[hd:hd + 1, :], -jnp.inf))
            ws.append((decay * cb).astype(BF16))
            xjs.append(jnp.where(lane_head == j, xdt_g, 0.0).astype(BF16))
        y = jnp.dot(jnp.concatenate(ws, axis=1), jnp.concatenate(xjs, axis=0), preferred_element_type=F32)
        y_ref[0, :, xsl] = (y + e_cum[:, xsl] * inter).astype(y_ref.dtype)
        upd = lax.dot_general(bm, xw[:, xsl], tn, preferred_element_type=F32)
        h_ref[0, g] = s_decay[:, xsl] * hst + upd


def ssd_scan(xbc, dt_row, par_row, states, *, name):
    b, t, _ = xbc.shape
    l = SSM_CHUNK
    nc = t // l
    gp = SSD_GROUPS_PER_STEP
    xw, nw = gp * SSM_GROUP_W, gp * SSM_STATE
    bmb = SSM_INNER // nw
    cmb = bmb + SSM_GROUPS // gp
    st_spec = pl.BlockSpec((1, gp, SSM_STATE, SSM_GROUP_W), lambda i, g, c: (i, g, 0, 0))
    dt_step = dt_row.reshape(2, b, SSM_GROUPS // gp, gp * SSM_HPG, t)
    par_step = par_row.reshape(2, SSM_GROUPS // gp, gp * SSM_HPG, 2)
    in_specs, args, out_specs, out_shape = [], [], [], []
    for d in range(2):
        ci = functools.partial(_chunk_index, nc=nc, reverse=d == 1)
        in_specs += [
            pl.BlockSpec((1, l, xw), lambda i, g, c, ci=ci: (i, ci(c), g)),
            pl.BlockSpec((1, l, nw), lambda i, g, c, ci=ci: (i, ci(c), bmb + g)),
            pl.BlockSpec((1, l, nw), lambda i, g, c, ci=ci: (i, ci(c), cmb + g)),
            pl.BlockSpec((1, 1, 1, gp * SSM_HPG, l), lambda i, g, c, ci=ci, d=d: (d, i, g, 0, ci(c))),
            pl.BlockSpec((1, 1, gp * SSM_HPG, 2), lambda i, g, c, d=d: (d, g, 0, 0)),
            st_spec,
        ]
        args += [xbc, xbc, xbc, dt_step, par_step, states[d]]
        out_specs += [pl.BlockSpec((1, l, xw), lambda i, g, c, ci=ci: (i, ci(c), g)), st_spec]
        out_shape += [jax.ShapeDtypeStruct((b, t, SSM_INNER), BF16),
                      jax.ShapeDtypeStruct(states[d].shape, F32)]
    yf, hf, yb, hb = pl.pallas_call(
        _ssd_kernel,
        grid=(b, SSM_GROUPS // gp, nc),
        in_specs=in_specs,
        out_specs=out_specs,
        out_shape=out_shape,
        compiler_params=_cparams(("parallel", "parallel", "arbitrary")),
        name=name,
    )(*args)
    return (yf, yb), (hf, hb)


def _even_out_kernel(mlf_ref, mlb_ref, ssdf_ref, ssdb_ref, o_ref, z_ref, xs_ref, mlg_ref, dsk_ref, ssg_ref,
                     out_ref):
    for h in range(ML_HEADS):
        sl = slice(h * ML_V_DIM, (h + 1) * ML_V_DIM)
        y = mlf_ref[0, :, sl].astype(F32) + mlb_ref[0, :, sl].astype(F32)
        og = o_ref[0, :, sl].astype(F32)
        out_ref[0, :, sl] = (_rms(y) * mlg_ref[:, sl] * jax.nn.sigmoid(og)).astype(out_ref.dtype)
    for g in range(SSM_GROUPS):
        sl = slice(g * SSM_GROUP_W, (g + 1) * SSM_GROUP_W)
        z = z_ref[0, :, sl].astype(F32)
        y = ((ssdf_ref[0, :, sl].astype(F32) + ssdb_ref[0, :, sl].astype(F32))
             + dsk_ref[:, sl] * xs_ref[0, :, sl].astype(F32))
        y = y * (z * jax.nn.sigmoid(z))
        so = slice(ML_V + g * SSM_GROUP_W, ML_V + (g + 1) * SSM_GROUP_W)
        out_ref[0, :, so] = (_rms(y) * ssg_ref[:, sl]).astype(out_ref.dtype)


def even_out(ml_y, ssd_y, proj, xbc, ml_norm_g, d_skip_full, ssm_norm_g, *, bt=128):
    b, t, _ = ml_y[0].shape
    bt = min(bt, t)
    w = ML_V
    tok = lambda blk: pl.BlockSpec((1, bt, w), lambda i, j: (i, j, blk))
    vec = pl.BlockSpec((1, w), lambda i, j: (0, 0))
    return pl.pallas_call(
        _even_out_kernel,
        grid=(b, t // bt),
        in_specs=[tok(0), tok(0), tok(0), tok(0), tok(E_O // w), tok(E_Z // w), tok(0), vec, vec, vec],
        out_specs=pl.BlockSpec((1, bt, 2 * w), lambda i, j: (i, j, 0)),
        out_shape=jax.ShapeDtypeStruct((b, t, 2 * w), BF16),
        compiler_params=_cparams(("parallel", "parallel")),
        name="even_gate",
    )(ml_y[0], ml_y[1], ssd_y[0], ssd_y[1], proj, proj, xbc, ml_norm_g.reshape(1, w), d_skip_full.reshape(1, w),
      ssm_norm_g.reshape(1, w))


def _hg_constants(reverse):
    l = HG_CHUNK
    pos = np.arange(l)
    tri = (pos[None, :] <= pos[:, None]).astype(np.float32)
    masks = []
    for lv in range(HG_LEVELS):
        m = 1 << lv
        blk = pos // (2 * m)
        later = (pos % (2 * m)) >= m
        masks.append(((blk[:, None] == blk[None, :]) & later[:, None] & (~later)[None, :]).astype(np.float32))
    masks = np.stack(masks)
    if reverse:
        tri = tri[::-1, ::-1]
        masks = masks[:, ::-1, ::-1]
    return (jnp.asarray(tri, BF16), jnp.asarray(masks.reshape(-1, l), F32))


def _hg_level_factors(q, k, f, cum, reverse):
    l, w = q.shape
    row = lax.broadcasted_iota(jnp.int32, (l, w), 0)
    sub = lax.broadcasted_iota(jnp.int32, (8, w), 0)
    zs = []
    for lv in range(HG_LEVELS):
        m = 1 << lv
        late = ((row & m) == 0) if reverse else ((row & m) != 0)
        if m >= 8:
            pieces = []
            for b0 in range(0, l, 2 * m):
                lo_rows, hi_rows = slice(b0, b0 + m), slice(b0 + m, b0 + 2 * m)
                if reverse:
                    mid = cum[b0 + m:b0 + m + 1]
                    pieces.append(q[lo_rows] * jnp.exp(cum[lo_rows] - mid))
                    pieces.append(k[hi_rows] * jnp.exp(mid - cum[hi_rows]))
                else:
                    mid = cum[b0 + m - 1:b0 + m]
                    pieces.append(k[lo_rows] * jnp.exp(mid - cum[lo_rows]))
                    pieces.append(q[hi_rows] * jnp.exp(cum[hi_rows] - mid))
            zs.append(jnp.concatenate(pieces, axis=0).astype(BF16))
            continue
        if m == 1:
            ex = jnp.where(late, f, 1.0)
        else:
            pieces = []
            if m >= 4:
                for b0 in range(0, l, 2 * m):
                    r = b0 + (m if reverse else m - 1)
                    pieces.append(cum[b0:b0 + 2 * m] - cum[r:r + 1])
            else:
                for b0 in range(0, l, 8):
                    ra, rb = (b0 + 2, b0 + 6) if reverse else (b0 + 1, b0 + 5)
                    mid = jnp.where(sub < 4, cum[ra:ra + 1], cum[rb:rb + 1])
                    pieces.append(cum[b0:b0 + 8] - mid)
            dd = pieces[0] if len(pieces) == 1 else jnp.concatenate(pieces, axis=0)
            ex = jnp.exp(jnp.where(late, dd, -dd))
        zs.append((jnp.where(late, q, k) * ex).astype(BF16))
    return zs


def _hg_chunk(q_ref, f_ref, i_ref, lb, tri, masks_ref, y_ref, s_ref, r0, reverse):
    l, e = HG_CHUNK, HG_EXPAND
    q_raw = q_ref[0, pl.ds(r0, l), :].astype(F32)
    f_raw = f_ref[0, pl.ds(r0, l), :].astype(F32)
    v = i_ref[0, pl.ds(r0, l), :]
    f = lb + (1.0 - lb) * jax.nn.sigmoid(f_raw)
    k = 1.0 - f
    lf = jnp.log(f)
    q = q_raw * jax.nn.sigmoid(q_raw) * (e ** -0.5)
    hi, mid, lo = _split3(lf)
    cum = (jnp.dot(tri, hi, preferred_element_type=F32)
           + jnp.dot(tri, mid, preferred_element_type=F32)
           + jnp.dot(tri, lo, preferred_element_type=F32))
    zs = _hg_level_factors(q, k, f, cum, reverse)
    end_row = 0 if reverse else l - 1
    cum_end = cum[end_row:end_row + 1, :]
    qc = (q * jnp.exp(cum)).astype(BF16)
    kt_end = (k * jnp.exp(cum_end - cum)).astype(BF16)
    s_decay = jnp.exp(cum_end)
    qk = q * k
    row = lax.broadcasted_iota(jnp.int32, (l, l), 0)
    col = lax.broadcasted_iota(jnp.int32, (l, l), 1)
    eye = row == col
    nt = (((1,), (1,)), ((), ()))
    tn = (((0,), (0,)), ((), ()))
    for hh in range(HG_HEADS_PER_STEP):
        sl = slice(hh * e, (hh + 1) * e)
        att = jnp.where(eye, jnp.sum(qk[:, sl], axis=1, keepdims=True), 0.0)
        for lv in range(HG_LEVELS):
            z = zs[lv][:, sl]
            a_lv = lax.dot_general(z, z, nt, preferred_element_type=F32)
            att = att + masks_ref[pl.ds(lv * l, l), :] * a_lv
        st = s_ref[0, hh]
        vh = v[:, sl]
        o = (jnp.dot(att.astype(BF16), vh, preferred_element_type=F32)
             + lax.dot_general(qc[:, sl], st.astype(BF16), nt, preferred_element_type=F32))
        y_ref[0, pl.ds(r0, l), sl] = o.astype(y_ref.dtype)
        upd = lax.dot_general(vh, kt_end[:, sl], tn, preferred_element_type=F32)
        s_ref[0, hh] = s_decay[:, sl] * st + upd


def _hg_kernel(qf_ref, ff_ref, if_ref, qb_ref, fb_ref, ib_ref, lbf_ref, lbb_ref, trif_ref, trib_ref,
               mf_ref, mb_ref, sf0_ref, sb0_ref, yf_ref, yb_ref, sf_ref, sb_ref, *, n_inner):
    l = HG_CHUNK

    @pl.when(pl.program_id(2) == 0)
    def _():
        sf_ref[...] = sf0_ref[...]
        sb_ref[...] = sb0_ref[...]

    lbf = lbf_ref[0]
    lbb = lbb_ref[0]
    trif = trif_ref[...]
    trib = trib_ref[...]

    def chunk(ci, carry):
        rf = pl.multiple_of(ci * l, l)
        rb = pl.multiple_of((n_inner - 1 - ci) * l, l)
        _hg_chunk(qf_ref, ff_ref, if_ref, lbf, trif, mf_ref, yf_ref, sf_ref, rf, False)
        _hg_chunk(qb_ref, fb_ref, ib_ref, lbb, trib, mb_ref, yb_ref, sb_ref, rb, True)
        return carry

    lax.fori_loop(0, n_inner, chunk, 0, unroll=4)


def hgrn2_scan(proj, lb, states, *, tb=512, name):
    b, t, _ = proj.shape
    tb = min(tb, t)
    nb = t // tb
    n_inner = tb // HG_CHUNK
    hw = HG_HEADS_PER_STEP * HG_EXPAND
    nh = HG_HEADS // HG_HEADS_PER_STEP
    qb_, ffb, fbb, ib_ = O_Q // hw, O_F // hw, (O_F + HG_F) // hw, O_I // hw
    trif, mf = _hg_constants(False)
    trib, mb = _hg_constants(True)
    fwd = lambda blk: pl.BlockSpec((1, tb, hw), lambda i, h, c: (i, c, blk + h))
    bwd = lambda blk: pl.BlockSpec((1, tb, hw), lambda i, h, c: (i, nb - 1 - c, blk + h))
    const = lambda a: pl.BlockSpec(a.shape, lambda i, h, c: (0,) * a.ndim)
    st_spec = pl.BlockSpec((1, HG_HEADS_PER_STEP, HG_V_DIM, HG_EXPAND), lambda i, h, c: (i, h, 0, 0))
    in_specs = [fwd(qb_), fwd(ffb), fwd(ib_), bwd(qb_), bwd(fbb), bwd(ib_),
                pl.BlockSpec((1, 1, hw), lambda i, h, c: (0, 0, h)),
                pl.BlockSpec((1, 1, hw), lambda i, h, c: (1, 0, h)),
                const(trif), const(trib), const(mf), const(mb), st_spec, st_spec]
    yf, yb, sf, sb = pl.pallas_call(
        functools.partial(_hg_kernel, n_inner=n_inner),
        grid=(b, nh, nb),
        in_specs=in_specs,
        out_specs=[fwd(0), bwd(0), st_spec, st_spec],
        out_shape=[jax.ShapeDtypeStruct((b, t, D_MODEL), BF16),
                   jax.ShapeDtypeStruct((b, t, D_MODEL), BF16),
                   jax.ShapeDtypeStruct(states[0].shape, F32),
                   jax.ShapeDtypeStruct(states[1].shape, F32)],
        compiler_params=_cparams(("parallel", "parallel", "arbitrary")),
        name=name,
    )(proj, proj, proj, proj, proj, proj, lb, lb, trif, trib, mf, mb, states[0], states[1])
    return (yf, yb), (sf, sb)


def _hg_out_kernel(of_ref, ob_ref, g_ref, ng_ref, out_ref):
    for h in range(of_ref.shape[-1] // HG_V_DIM):
        sl = slice(h * HG_V_DIM, (h + 1) * HG_V_DIM)
        g = g_ref[0, :, sl].astype(F32)
        o = of_ref[0, :, sl].astype(F32) + ob_ref[0, :, sl].astype(F32)
        out_ref[0, :, sl] = (_rms(o) * ng_ref[:, sl] * (g * jax.nn.sigmoid(g))).astype(out_ref.dtype)


def hg_out(o_f, o_b, proj, norm_g, *, bt=256, bc=1024, name):
    b, t, d = o_f.shape
    bt = min(bt, t)
    gb = O_G // bc
    tok = pl.BlockSpec((1, bt, bc), lambda i, j, c: (i, j, c))
    return pl.pallas_call(
        _hg_out_kernel,
        grid=(b, t // bt, d // bc),
        in_specs=[tok, tok,
                  pl.BlockSpec((1, bt, bc), lambda i, j, c: (i, j, gb + c)),
                  pl.BlockSpec((1, bc), lambda i, j, c: (0, c))],
        out_specs=tok,
        out_shape=jax.ShapeDtypeStruct((b, t, d), BF16),
        compiler_params=_cparams(("parallel", "parallel", "parallel")),
        name=name,
    )(o_f, o_b, proj, norm_g.reshape(1, d))


def even_mixer(a_c, a_l, w_main, w_small, w_out, layer, ml_gate_b, ml_norm_g, conv_w, conv_b,
               dt_bias, a_log, d_skip, ssm_norm_g, need_ctx):
    bsz = a_l.shape[0]

    def features(a, tag):
        b, t, d = a.shape
        bm = min(1024, b * t)
        a2 = a.reshape(b * t, d)
        proj = matmul(a2, w_main, bm=bm, bn=1024, out_dtype=BF16, name="ab_in_" + tag).reshape(b, t, E_MAIN)
        small = matmul(a2, w_small, bm=bm, bn=128, out_dtype=F32, name="ab_in_small_" + tag).reshape(b, t, 128)
        gates = small[..., :16].reshape(b, t, 2, 2, ML_HEADS) + ml_gate_b.astype(F32)
        g_col = gates.transpose(2, 0, 4, 1, 3)
        g_row = gates.transpose(2, 0, 4, 3, 1)
        dt = small[..., 16:16 + 2 * SSM_HEADS].reshape(b, t, 2, SSM_GROUPS, SSM_HPG)
        dt_row = dt.transpose(2, 0, 3, 4, 1)
        xbc = conv_silu(proj, conv_w, conv_b)
        return dict(proj=proj, xbc=xbc, g_col=g_col, g_row=g_row, dt_row=dt_row)

    fc, fl = features(a_c, "ctx"), features(a_l, "lat")
    neg_a = -jnp.exp(a_log.astype(F32))
    par = jnp.stack([dt_bias.astype(F32), neg_a], axis=1).reshape(2, 2, SSM_GROUPS, SSM_HPG)
    par_row = par.transpose(0, 2, 3, 1)

    ml_init = (jnp.zeros((bsz, ML_HEADS, ML_QK_DIM, ML_V_DIM), F32),
               jnp.zeros((bsz, ML_HEADS, 1, ML_QK_DIM), F32),
               jnp.zeros((bsz, ML_HEADS, 1, 1), F32))
    ssd_init = jnp.zeros((bsz, SSM_GROUPS, SSM_STATE, SSM_GROUP_W), F32)

    def ml(f, st, tag):
        return mlstm_scan(f['proj'], f['g_col'], f['g_row'], st, name="mlstm_scan_" + tag)

    def ssd(f, st, tag):
        return ssd_scan(f['xbc'], f['dt_row'], par_row, st, name="ssd_scan_" + tag)

    ml_c, ml_st = ml(fc, (ml_init, ml_init), "ctx")
    ml_l, _ = ml(fl, ml_st, "lat")
    ssd_c, ssd_st = ssd(fc, (ssd_init, ssd_init), "ctx")
    ssd_l, _ = ssd(fl, ssd_st, "lat")
    d_full = jnp.repeat(d_skip.astype(F32), SSM_HEAD_DIM)

    def output(f, ml_y, ssd_y, tag):
        b, t, _ = ml_y[0].shape
        cat = even_out(ml_y, ssd_y, f['proj'], f['xbc'], ml_norm_g, d_full, ssm_norm_g)
        return dense(cat.reshape(b * t, D_MODEL), w_out, layer, out_dtype=BF16,
                     name="ab_out_" + tag).reshape(b, t, D_MODEL)

    y_ctx = output(fc, ml_c, ssd_c, "ctx") if need_ctx else None
    return y_ctx, output(fl, ml_l, ssd_l, "lat")


def hgrn2_mixer(a_c, a_l, w_in, w_out, layer, lb, norm_g, need_ctx, lat_colmajor=True):
    bsz = a_l.shape[0]
    lb3 = lb.astype(F32).reshape(2, 1, HG_F)

    def features(a, tag):
        b, t, d = a.shape
        return dense(a.reshape(b * t, d), w_in, layer, out_dtype=BF16,
                     name="hg_in_" + tag).reshape(b, t, 5 * D_MODEL)

    pc, pl_ = features(a_c, "ctx"), features(a_l, "lat")
    init = jnp.zeros((bsz, HG_HEADS, HG_V_DIM, HG_EXPAND), F32)
    o_c, st = hgrn2_scan(pc, lb3, (init, init), name="hg_scan_ctx")
    o_l, _ = hgrn2_scan(pl_, lb3, st, name="hg_scan_lat")

    def output(p, o, colmajor, tag):
        b, t, _ = o[0].shape
        gated = hg_out(o[0], o[1], p, norm_g, name="hg_gate_" + tag)
        if colmajor:
            gated = grid_from_colmajor(gated)
        return dense(gated.reshape(b * t, D_MODEL), w_out, layer, out_dtype=BF16,
                     name="hg_out_" + tag).reshape(b, t, D_MODEL)

    y_ctx = output(pc, o_c, False, "ctx") if need_ctx else None
    return y_ctx, output(pl_, o_l, lat_colmajor, "lat")


def ffn(a, wg, wu, wd, layer, tag):
    b, t, d = a.shape
    m = b * t
    a2 = a.reshape(m, d)
    if m > 2048:
        hid = matmul_stream(a2, (wg, wu), layer, bm=2048, bn=256, out_dtype=BF16, name="ffn_up_" + tag)
    else:
        hid = swiglu_single(a2, wg, wu, layer, bn=256, name="ffn_up_" + tag)
    return dense(hid, wd, layer, out_dtype=BF16, name="ffn_down_" + tag, bm=512, bn=512).reshape(b, t, d)


def grid_to_colmajor(h):
    bsz, t, d = h.shape
    rows = t // GRID_W
    return h.reshape(bsz, rows, GRID_W, d).transpose(0, 2, 1, 3).reshape(bsz, t, d)


def grid_from_colmajor(h):
    bsz, t, d = h.shape
    rows = t // GRID_W
    return h.reshape(bsz, GRID_W, rows, d).transpose(0, 2, 1, 3).reshape(bsz, t, d)


def kernel(x, c, ctx, c_ctx, ada_w, ada_b, norm_g, ffn_w_gate, ffn_w_up, ffn_w_down,
           ab_w_in, ab_w_out, ml_gate_b, ml_norm_g, ssm_conv_w, ssm_conv_b, ssm_dt_bias,
           ssm_a_log, ssm_d, ssm_norm_g, hg_w_in, hg_w_out, hg_lb, hg_norm_g):
    depth = ada_w.shape[0]
    bsz = x.shape[0]
    d = D_MODEL
    lb_p = jax.nn.softmax(hg_lb.astype(F32), axis=1)
    lower_bounds = jnp.cumsum(lb_p, axis=1) - lb_p[:, :1]
    cond = jnp.concatenate([jax.nn.silu(c), jax.nn.silu(c_ctx)[None, :],
                            jnp.zeros((8 - bsz - 1, d), F32)], axis=0)
    h_lat, h_ctx = x, ctx
    y_l = y_c = f_l = f_c = None
    gate_l = gate_c = g_prev = None
    for layer in range(depth):
        need_ctx = layer < depth - 1
        j = layer // 2
        mod = matmul(cond, ada_w, bm=8, bn=512, out_dtype=F32, bias=ada_b[layer], layer=layer,
                     name="ada_mod")
        mod_l = [mod[:bsz, i * d:(i + 1) * d][:, None, :] for i in range(6)]
        mod_c = [jnp.broadcast_to(mod[bsz, i * d:(i + 1) * d][None, None, :], (bsz, 1, d)) for i in range(6)]
        g = norm_g[layer]
        h_lat, a_l = resid_norm(h_lat, f_l, gate_l, g_prev, g[0], mod_l[0], mod_l[1])
        h_ctx, a_c = resid_norm(h_ctx, f_c, gate_c, g_prev, g[0], mod_c[0], mod_c[1])
        if layer % 2 == 0:
            w_in = ab_w_in[j]
            w_main = jnp.concatenate(
                [w_in[:, :6144], w_in[:, 6160:6160 + 2048 + SSM_CONV_DIM]], axis=1).astype(BF16)
            w_small = jnp.concatenate(
                [w_in[:, 6144:6160], w_in[:, 12304:12368], jnp.zeros((d, 128 - 80), F32)], axis=1).astype(BF16)
            y_c, y_l = even_mixer(a_c, a_l, w_main, w_small, ab_w_out, j, ml_gate_b[j],
                                  ml_norm_g[j], ssm_conv_w[j], ssm_conv_b[j], ssm_dt_bias[j],
                                  ssm_a_log[j], ssm_d[j], ssm_norm_g[j], need_ctx)
        else:
            y_c, y_l = hgrn2_mixer(a_c, grid_to_colmajor(a_l), hg_w_in, hg_w_out, j,
                                   lower_bounds[:, layer], hg_norm_g[j], need_ctx)
        wd = ffn_w_down
        h_lat, a2_l = resid_norm(h_lat, y_l, mod_l[2], g[1], g[2], mod_l[3], mod_l[4])
        f_l = ffn(a2_l, ffn_w_gate, ffn_w_up, wd, layer, "lat")
        gate_l = mod_l[5]
        if need_ctx:
            h_ctx, a2_c = resid_norm(h_ctx, y_c, mod_c[2], g[1], g[2], mod_c[3], mod_c[4])
            f_c = ffn(a2_c, ffn_w_gate, ffn_w_up, wd, layer, "ctx")
            gate_c = mod_c[5]
        else:
            f_c = gate_c = None
        g_prev = g[3]
    h_lat, _ = resid_norm(h_lat, f_l, gate_l, g_prev)
    return h_lat
```

```python
import functools
import math

import jax
import jax.numpy as jnp
import numpy as np
from jax import lax
from jax.experimental import pallas as pl
from jax.experimental.pallas import tpu as pltpu

F32 = jnp.float32
BF16 = jnp.bfloat16

D_MODEL = 4096
GRID_W = 64
EPS = 1e-6

ML_HEADS = 4
ML_QK_DIM = 256
ML_V_DIM = 512
ML_QK = ML_HEADS * ML_QK_DIM
ML_V = ML_HEADS * ML_V_DIM
ML_CHUNK = 128
GATE_CAP = 15.0

SSM_HEAD_DIM = 64
SSM_INNER = 2048
SSM_HEADS = 32
SSM_GROUPS = 8
SSM_HPG = 4
SSM_STATE = 128
SSM_CONV = 5
SSM_CHUNK = 128
SSM_GROUP_W = SSM_HPG * SSM_HEAD_DIM
SSM_CONV_DIM = SSM_INNER + 2 * SSM_GROUPS * SSM_STATE
SSD_GROUPS_PER_STEP = 4

HG_EXPAND = 128
HG_HEADS = 32
HG_F = HG_HEADS * HG_EXPAND
HG_V_DIM = 128
HG_CHUNK = 64
HG_LEVELS = 6
HG_HEADS_PER_STEP = 4

FFN_HIDDEN = 11008

E_Q, E_K, E_V, E_O, E_Z, E_XBC = 0, 1024, 2048, 4096, 6144, 8192
E_MAIN = 12288
O_Q, O_F, O_I, O_G = 0, 4096, 12288, 16384

VMEM_LIMIT = 56 * 1024 * 1024


def _cparams(sem):
    return pltpu.CompilerParams(dimension_semantics=sem, vmem_limit_bytes=VMEM_LIMIT)


def _mm_kernel(*refs, has_bias, cast_w):
    x_ref, w_ref = refs[0], refs[1]
    b_ref = refs[2] if has_bias else None
    o_ref = refs[2 + has_bias]
    if cast_w:
        wb_ref = refs[3 + has_bias]

        @pl.when(pl.program_id(1) == 0)
        def _():
            wb_ref[...] = w_ref[...].astype(BF16)

        w = wb_ref[...]
    else:
        w = w_ref[...]
    acc = jnp.dot(x_ref[...].astype(BF16), w, preferred_element_type=F32)
    if has_bias:
        acc = acc + b_ref[...]
    o_ref[...] = acc.astype(o_ref.dtype)


def matmul(x, w, *, bm, bn, out_dtype, name, bias=None, layer=None):
    m, k = x.shape
    n = w.shape[-1]
    assert m % bm == 0 and n % bn == 0, (m, bm, n, bn)
    if layer is None:
        w_spec = pl.BlockSpec((k, bn), lambda j, i: (0, j))
    else:
        w_spec = pl.BlockSpec((None, k, bn), lambda j, i: (layer, 0, j))
    in_specs = [pl.BlockSpec((bm, k), lambda j, i: (i, 0)), w_spec]
    args = [x, w]
    if bias is not None:
        in_specs.append(pl.BlockSpec((1, bn), lambda j, i: (0, j)))
        args.append(bias.reshape(1, n))
    cast_w = w.dtype != BF16
    return pl.pallas_call(
        functools.partial(_mm_kernel, has_bias=bias is not None, cast_w=cast_w),
        grid=(n // bn, m // bm),
        in_specs=in_specs,
        out_specs=pl.BlockSpec((bm, bn), lambda j, i: (i, j)),
        out_shape=jax.ShapeDtypeStruct((m, n), out_dtype),
        scratch_shapes=[pltpu.VMEM((k, bn), BF16)] if cast_w else [],
        compiler_params=_cparams(("parallel", "arbitrary" if cast_w else "parallel")),
        name=name,
    )(*args)


def _swiglu(g, u):
    return g * jax.nn.sigmoid(g) * u


def _mm_stream_kernel(*refs, n_w, n_x, nj, mi, kc):
    x_refs, w_refs = refs[:n_x], refs[n_x:n_x + n_w]
    o_refs, wb_refs = refs[n_x + n_w:2 * n_x + n_w], refs[2 * n_x + n_w:]
    j, i = pl.program_id(0), pl.program_id(1)
    slot = j % 2

    @pl.when((j < nj) & (i < mi))
    def _():
        r0 = pl.multiple_of(i * kc, 16)
        for w_ref, wb_ref in zip(w_refs, wb_refs):
            wb_ref[slot, pl.ds(r0, kc), :] = w_ref[...].astype(BF16)

    def product(x_ref, o_ref):
        x = x_ref[...]
        accs = [jnp.dot(x, wb_ref[1 - slot], preferred_element_type=F32) for wb_ref in wb_refs]
        out = _swiglu(*accs) if n_w == 2 else accs[0]
        o_ref[...] = out.astype(o_ref.dtype)

    @pl.when((j > 0) & (i < mi))
    def _():
        product(x_refs[0], o_refs[0])

    if n_x == 2:
        @pl.when((j > 0) & (i == mi))
        def _():
            product(x_refs[1], o_refs[1])


def matmul_stream(x, ws, layer, *, bm, bn, out_dtype, name, x2=None):
    m, k = x.shape
    n = ws[0].shape[-1]
    mi, nj = m // bm, n // bn
    kc = k // mi
    assert m % bm == 0 and n % bn == 0 and k % mi == 0 and kc % 16 == 0 and mi > 1, (m, bm, n, bn, k)
    n_x = 1 if x2 is None else 2
    row = lambda j, i: jnp.where(j == 0, 0, jnp.minimum(i, mi - 1))
    col = lambda j: jnp.maximum(j - 1, 0)
    w_spec = pl.BlockSpec(
        (None, kc, bn),
        lambda j, i: (layer, jnp.where(j < nj, jnp.minimum(i, mi - 1), mi - 1), jnp.minimum(j, nj - 1)))
    in_specs = [pl.BlockSpec((bm, k), lambda j, i: (row(j, i), 0))]
    out_specs = [pl.BlockSpec((bm, bn), lambda j, i: (row(j, i), col(j)))]
    out_shape = [jax.ShapeDtypeStruct((m, n), out_dtype)]
    args = [x]
    if x2 is not None:
        m2 = x2.shape[0]
        assert m2 <= bm and x2.shape[1] == k
        in_specs.append(pl.BlockSpec((m2, k), lambda j, i: (0, 0)))
        out_specs.append(pl.BlockSpec((m2, bn), lambda j, i: (0, col(j))))
        out_shape.append(jax.ShapeDtypeStruct((m2, n), out_dtype))
        args.append(x2)
    outs = pl.pallas_call(
        functools.partial(_mm_stream_kernel, n_w=len(ws), n_x=n_x, nj=nj, mi=mi, kc=kc),
        grid=(nj + 1, mi + n_x - 1),
        in_specs=in_specs + [w_spec] * len(ws),
        out_specs=out_specs,
        out_shape=out_shape,
        scratch_shapes=[pltpu.VMEM((2, k, bn), BF16) for _ in ws],
        compiler_params=_cparams(("arbitrary", "arbitrary")),
        name=name,
    )(*args, *ws)
    return outs[0] if x2 is None else (outs[0], outs[1])


def dense2(x, x2, w, layer, *, out_dtype, name, bm=1024, bn=1024):
    if x.shape[0] <= bm:
        y2 = None if x2 is None else dense(x2, w, layer, out_dtype=out_dtype, name=name + "_ctx")
        return dense(x, w, layer, out_dtype=out_dtype, name=name), y2
    if x2 is None:
        return matmul_stream(x, (w,), layer, bm=bm, bn=bn, out_dtype=out_dtype, name=name), None
    return matmul_stream(x, (w,), layer, bm=bm, bn=bn, out_dtype=out_dtype, name=name, x2=x2)


def dense(x, w, layer, *, out_dtype, name, bm=1024, bn=1024):
    m = x.shape[0]
    if m > bm:
        return matmul_stream(x, (w,), layer, bm=bm, bn=bn, out_dtype=out_dtype, name=name)
    return matmul(x, w, bm=m, bn=256, out_dtype=out_dtype, layer=layer, name=name)


def _rms(x):
    return x * lax.rsqrt(jnp.mean(x * x, axis=-1, keepdims=True) + EPS)


def _resid_norm_kernel(*refs, has_y, want_a):
    it = iter(refs)
    h_ref = next(it)
    if has_y:
        y_ref, gate_ref, g1_ref = next(it), next(it), next(it)
    if want_a:
        g2_ref, shift_ref, scale_ref = next(it), next(it), next(it)
    if has_y:
        hout_ref = next(it)
    if want_a:
        a_ref = next(it)
    h = h_ref[0]
    if has_y:
        h = h + gate_ref[0] * (_rms(y_ref[0].astype(F32)) * g1_ref[...])
        hout_ref[0] = h
    if want_a:
        a = (_rms(h) * g2_ref[...]) * (1.0 + scale_ref[0]) + shift_ref[0]
        a_ref[0] = a.astype(a_ref.dtype)


def resid_norm(h, y=None, gate=None, g1=None, g2=None, shift=None, scale=None, *, bt=128,
               a_colmajor=False):
    b, t, d = h.shape
    has_y, want_a = y is not None, g2 is not None
    bt = GRID_W if a_colmajor else min(bt, t)
    tok = pl.BlockSpec((1, bt, d), lambda i, j: (i, j, 0))
    per_b = pl.BlockSpec((1, 1, d), lambda i, j: (i, 0, 0))
    vec = pl.BlockSpec((1, d), lambda i, j: (0, 0))
    args, in_specs, out_shape, out_specs = [h], [tok], [], []
    if has_y:
        args += [y, gate, g1.reshape(1, d)]
        in_specs += [tok, per_b, vec]
        out_shape.append(jax.ShapeDtypeStruct((b, t, d), F32))
        out_specs.append(tok)
    if want_a:
        args += [g2.reshape(1, d), shift, scale]
        in_specs += [vec, per_b, per_b]
        if a_colmajor:
            out_shape.append(jax.ShapeDtypeStruct((b, GRID_W, (t // GRID_W) * d), BF16))
            out_specs.append(pl.BlockSpec((1, GRID_W, d), lambda i, j: (i, 0, j)))
        else:
            out_shape.append(jax.ShapeDtypeStruct((b, t, d), BF16))
            out_specs.append(tok)
    outs = pl.pallas_call(
        functools.partial(_resid_norm_kernel, has_y=has_y, want_a=want_a),
        grid=(b, t // bt),
        in_specs=in_specs,
        out_specs=out_specs,
        out_shape=out_shape,
        compiler_params=_cparams(("parallel", "parallel")),
        name="resid_norm",
    )(*args)
    outs = list(outs)
    h_new = outs.pop(0) if has_y else h
    a = outs.pop(0).reshape(b, t, d) if want_a else None
    return h_new, a


def _softplus(x):
    return jnp.maximum(x, 0.0) + jnp.log1p(jnp.exp(-jnp.abs(x)))


def _log_sigmoid(x):
    return jnp.minimum(x, 0.0) - jnp.log1p(jnp.exp(-jnp.abs(x)))


def _chunk_index(c, nc, reverse):
    return (nc - 1 - c) if reverse else c


def _mlstm_kernel(*refs):
    ins, outs = refs[:16], refs[16:]

    @pl.when(pl.program_id(2) == 0)
    def _():
        for d in range(2):
            for s in range(3):
                outs[4 * d + 1 + s][...] = ins[8 * d + 5 + s][...]

    for d in range(2):
        _mlstm_chunk(*ins[8 * d:8 * d + 5], *outs[4 * d:4 * d + 4], reverse=d == 1)


def _mlstm_chunk(q_ref, k_ref, v_ref, gc_ref, gr_ref, y_ref, c_ref, n_ref, m_ref, *, reverse):
    l = ML_CHUNK
    row = lax.broadcasted_iota(jnp.int32, (l, l), 0)
    col = lax.broadcasted_iota(jnp.int32, (l, l), 1)
    mask = (col >= row) if reverse else (col <= row)
    mask_t = (col <= row) if reverse else (col >= row)

    gcol = gc_ref[0, 0, 0]
    grow = gr_ref[0, 0, 0]
    ig_c = GATE_CAP * jnp.tanh(gcol[:, 0:1] / GATE_CAP)
    lf_c = _log_sigmoid(GATE_CAP * jnp.tanh(gcol[:, 1:2] / GATE_CAP))
    ig_r = GATE_CAP * jnp.tanh(grow[0:1, :] / GATE_CAP)
    lf_r = _log_sigmoid(GATE_CAP * jnp.tanh(grow[1:2, :] / GATE_CAP))

    b_c = jnp.sum(jnp.where(mask, lf_r, 0.0), axis=1, keepdims=True)
    b_r = jnp.sum(jnp.where(mask_t, lf_c, 0.0), axis=0, keepdims=True)
    total = jnp.sum(lf_c, axis=0, keepdims=True)

    m_prev = m_ref[0, 0]
    logw = jnp.where(mask, b_c - b_r + ig_r, -jnp.inf)
    inter = b_c + m_prev
    m_t = jnp.maximum(jnp.max(logw, axis=1, keepdims=True), inter)

    q = q_ref[0]
    k = k_ref[0]
    v = v_ref[0]
    scale = ML_QK_DIM ** -0.5
    qk = lax.dot_general(q, k, (((1,), (1,)), ((), ())), preferred_element_type=F32)
    s = qk * scale * jnp.exp(logw - m_t)
    inter_w = jnp.exp(inter - m_t) * scale

    c_st = c_ref[0, 0]
    n_st = n_ref[0, 0]
    num = (jnp.dot(s.astype(BF16), v, preferred_element_type=F32)
           + inter_w * jnp.dot(q, c_st.astype(BF16), preferred_element_type=F32))
    qn = jnp.sum(q.astype(F32) * n_st, axis=1, keepdims=True)
    den = jnp.sum(s, axis=1, keepdims=True) + inter_w * qn
    y_ref[0] = (num * (1.0 / jnp.maximum(jnp.abs(den), jnp.exp(-m_t)))).astype(y_ref.dtype)

    g_c = total - b_c + ig_c
    g_r = total - b_r + ig_r
    m_new = jnp.maximum(total + m_prev, jnp.max(g_r, axis=1, keepdims=True))
    w_c = jnp.exp(g_c - m_new)
    decay = jnp.exp(total + m_prev - m_new)
    kw = k.astype(F32) * w_c
    kv = lax.dot_general(kw.astype(BF16), v, (((0,), (0,)), ((), ())), preferred_element_type=F32)
    c_ref[0, 0] = decay * c_st + kv
    n_ref[0, 0] = decay * n_st + jnp.sum(kw, axis=0, keepdims=True)
    m_ref[0, 0] = m_new


def mlstm_scan(proj, gates_col, gates_row, states, *, name):
    b, t, _ = proj.shape
    l = ML_CHUNK
    nc = t // l
    qb, kb, vb = E_Q // ML_QK_DIM, E_K // ML_QK_DIM, E_V // ML_V_DIM
    st_specs = [pl.BlockSpec((1, 1, ML_QK_DIM, ML_V_DIM), lambda i, h, c: (i, h, 0, 0)),
                pl.BlockSpec((1, 1, 1, ML_QK_DIM), lambda i, h, c: (i, h, 0, 0)),
                pl.BlockSpec((1, 1, 1, 1), lambda i, h, c: (i, h, 0, 0))]
    in_specs, args, out_specs, out_shape = [], [], [], []
    for d in range(2):
        ci = functools.partial(_chunk_index, nc=nc, reverse=d == 1)
        in_specs += [
            pl.BlockSpec((1, l, ML_QK_DIM), lambda i, h, c, ci=ci: (i, ci(c), qb + h)),
            pl.BlockSpec((1, l, ML_QK_DIM), lambda i, h, c, ci=ci: (i, ci(c), kb + h)),
            pl.BlockSpec((1, l, ML_V_DIM), lambda i, h, c, ci=ci: (i, ci(c), vb + h)),
            pl.BlockSpec((1, 1, 1, l, 2), lambda i, h, c, ci=ci, d=d: (d, i, h, ci(c), 0)),
            pl.BlockSpec((1, 1, 1, 2, l), lambda i, h, c, ci=ci, d=d: (d, i, h, 0, ci(c))),
        ] + st_specs
        args += [proj, proj, proj, gates_col, gates_row, *states[d]]
        out_specs += [pl.BlockSpec((1, l, ML_V_DIM), lambda i, h, c, ci=ci: (i, ci(c), h))] + st_specs
        out_shape += [jax.ShapeDtypeStruct((b, t, ML_V), BF16)]
        out_shape += [jax.ShapeDtypeStruct(s.shape, F32) for s in states[d]]
    outs = pl.pallas_call(
        _mlstm_kernel,
        grid=(b, ML_HEADS, nc),
        in_specs=in_specs,
        out_specs=out_specs,
        out_shape=out_shape,
        compiler_params=_cparams(("parallel", "parallel", "arbitrary")),
        name=name,
    )(*args)
    return (outs[0], outs[4]), (tuple(outs[1:4]), tuple(outs[5:8]))


CONV_ROWS = 256
CONV_HALO = 16


def _conv_kernel(x_ref, w_ref, b_ref, o_ref, *, t):
    w = w_ref[...]
    bias = b_ref[...]
    r = min(CONV_ROWS, t)
    half = SSM_CONV // 2
    cw = x_ref.shape[-1]
    for r0 in range(0, t, r):
        parts = []
        if r0 == 0:
            parts.append(jnp.zeros((CONV_HALO, cw), F32))
        else:
            parts.append(x_ref[0, pl.ds(r0 - CONV_HALO, CONV_HALO), :].astype(F32))
        parts.append(x_ref[0, pl.ds(r0, r), :].astype(F32))
        if r0 + r == t:
            parts.append(jnp.zeros((CONV_HALO, cw), F32))
        else:
            parts.append(x_ref[0, pl.ds(r0 + r, CONV_HALO), :].astype(F32))
        win = jnp.concatenate(parts, axis=0)
        n = r + 2 * CONV_HALO
        acc = jnp.zeros((r, cw), F32) + bias
        for kk in range(SSM_CONV):
            sh = (half - kk) % n
            rolled = win if sh == 0 else pltpu.roll(win, sh, 0)
            acc = acc + w[kk:kk + 1, :] * rolled[CONV_HALO:CONV_HALO + r, :]
        o_ref[0, pl.ds(r0, r), :] = (acc * jax.nn.sigmoid(acc)).astype(o_ref.dtype)


def conv_silu(proj, conv_w, conv_b, *, bc=512):
    b, t, _ = proj.shape
    c = SSM_CONV_DIM
    off = E_XBC // bc
    return pl.pallas_call(
        functools.partial(_conv_kernel, t=t),
        grid=(b, c // bc),
        in_specs=[pl.BlockSpec((1, t, bc), lambda i, j: (i, 0, off + j)),
                  pl.BlockSpec((SSM_CONV, bc), lambda i, j: (0, j)),
                  pl.BlockSpec((1, bc), lambda i, j: (0, j))],
        out_specs=pl.BlockSpec((1, t, bc), lambda i, j: (i, 0, j)),
        out_shape=jax.ShapeDtypeStruct((b, t, c), BF16),
        compiler_params=_cparams(("parallel", "parallel")),
        name="conv_silu",
    )(proj, conv_w, conv_b.reshape(1, c))


def _split3(x):
    hi = x.astype(BF16)
    r1 = x - hi.astype(F32)
    mid = r1.astype(BF16)
    lo = (r1 - mid.astype(F32)).astype(BF16)
    return hi, mid, lo


def _dot3(dims, parts_lhs, rhs):
    return sum(lax.dot_general(p, rhs, dims, preferred_element_type=F32) for p in parts_lhs)


def _head_selector(heads, width):
    rows = lax.broadcasted_iota(jnp.int32, (heads, heads * width), 0)
    lanes = lax.broadcasted_iota(jnp.int32, (heads, heads * width), 1)
    return (lanes // width == rows).astype(BF16)


def _ssd_kernel(*refs):
    ins, outs = refs[:12], refs[12:]

    @pl.when(pl.program_id(2) == 0)
    def _():
        for d in range(2):
            outs[2 * d + 1][...] = ins[6 * d + 5][...]

    for d in range(2):
        _ssd_chunk(*ins[6 * d:6 * d + 5], *outs[2 * d:2 * d + 2], reverse=d == 1)


def _ssd_chunk(xs_ref, bm_ref, cm_ref, dtr_ref, pr_ref, y_ref, h_ref, *, reverse):
    l = SSM_CHUNK
    gp = SSD_GROUPS_PER_STEP
    nh = gp * SSM_HPG
    row = lax.broadcasted_iota(jnp.int32, (l, l), 0)
    col = lax.broadcasted_iota(jnp.int32, (l, l), 1)
    mask = (col >= row) if reverse else (col <= row)
    mask_t = (col <= row) if reverse else (col >= row)

    nn = (((1,), (0,)), ((), ()))
    nt = (((1,), (1,)), ((), ()))
    tn = (((0,), (0,)), ((), ()))
    pr = pr_ref[0, 0]
    dt_r = _softplus(dtr_ref[0, 0, 0] + pr[:, 0:1])
    a_r = dt_r * pr[:, 1:2]
    cum_r = _dot3(nn, _split3(a_r), mask_t.astype(BF16))
    dt_parts, cum_parts = _split3(dt_r), _split3(cum_r)
    sel_ch = _head_selector(nh, SSM_HEAD_DIM)
    dt_ch = _dot3(tn, dt_parts, sel_ch)
    cum_ch = _dot3(tn, cum_parts, sel_ch)
    cum_bc = _dot3(tn, cum_parts, _head_selector(nh, l))
    end_row = 0 if reverse else l - 1
    total_ch = cum_ch[end_row:end_row + 1, :]

    xdt = xs_ref[0].astype(F32) * dt_ch
    e_cum = jnp.exp(cum_ch)
    xw = (xdt * jnp.exp(total_ch - cum_ch)).astype(BF16)
    s_decay = jnp.exp(total_ch)
    lane_head = lax.broadcasted_iota(jnp.int32, (l, SSM_GROUP_W), 1) // SSM_HEAD_DIM
    for g in range(gp):
        xsl = slice(g * SSM_GROUP_W, (g + 1) * SSM_GROUP_W)
        nsl = slice(g * SSM_STATE, (g + 1) * SSM_STATE)
        bm = bm_ref[0, :, nsl]
        cm = cm_ref[0, :, nsl]
        hst = h_ref[0, g]
        cb = lax.dot_general(cm, bm, nt, preferred_element_type=F32)
        inter = jnp.dot(cm, hst.astype(BF16), preferred_element_type=F32)
        xdt_g = xdt[:, xsl]
        ws, xjs = [], []
        for j in range(SSM_HPG):
            hd = g * SSM_HPG + j
            decay = jnp.exp(jnp.where(mask, cum_bc[:, hd * l:(hd + 1) * l] - cum_r[hd:hd + 1, :], -jnp.inf))
            ws.append((decay * cb).astype(BF16))
            xjs.append(jnp.where(lane_head == j, xdt_g, 0.0).astype(BF16))
        y = jnp.dot(jnp.concatenate(ws, axis=1), jnp.concatenate(xjs, axis=0), preferred_element_type=F32)
        y_ref[0, :, xsl] = (y + e_cum[:, xsl] * inter).astype(y_ref.dtype)
        upd = lax.dot_general(bm, xw[:, xsl], tn, preferred_element_type=F32)
        h_ref[0, g] = s_decay[:, xsl] * hst + upd


def ssd_scan(xbc, dt_row, par_row, states, *, name):
    b, t, _ = xbc.shape
    l = SSM_CHUNK
    nc = t // l
    gp = SSD_GROUPS_PER_STEP
    xw, nw = gp * SSM_GROUP_W, gp * SSM_STATE
    bmb = SSM_INNER // nw
    cmb = bmb + SSM_GROUPS // gp
    st_spec = pl.BlockSpec((1, gp, SSM_STATE, SSM_GROUP_W), lambda i, g, c: (i, g, 0, 0))
    dt_step = dt_row.reshape(2, b, SSM_GROUPS // gp, gp * SSM_HPG, t)
    par_step = par_row.reshape(2, SSM_GROUPS // gp, gp * SSM_HPG, 2)
    in_specs, args, out_specs, out_shape = [], [], [], []
    for d in range(2):
        ci = functools.partial(_chunk_index, nc=nc, reverse=d == 1)
        in_specs += [
            pl.BlockSpec((1, l, xw), lambda i, g, c, ci=ci: (i, ci(c), g)),
            pl.BlockSpec((1, l, nw), lambda i, g, c, ci=ci: (i, ci(c), bmb + g)),
            pl.BlockSpec((1, l, nw), lambda i, g, c, ci=ci: (i, ci(c), cmb + g)),
            pl.BlockSpec((1, 1, 1, gp * SSM_HPG, l), lambda i, g, c, ci=ci, d=d: (d, i, g, 0, ci(c))),
            pl.BlockSpec((1, 1, gp * SSM_HPG, 2), lambda i, g, c, d=d: (d, g, 0, 0)),
            st_spec,
        ]
        args += [xbc, xbc, xbc, dt_step, par_step, states[d]]
        out_specs += [pl.BlockSpec((1, l, xw), lambda i, g, c, ci=ci: (i, ci(c), g)), st_spec]
        out_shape += [jax.ShapeDtypeStruct((b, t, SSM_INNER), BF16),
                      jax.ShapeDtypeStruct(states[d].shape, F32)]
    yf, hf, yb, hb = pl.pallas_call(
        _ssd_kernel,
        grid=(b, SSM_GROUPS // gp, nc),
        in_specs=in_specs,
        out_specs=out_specs,
        out_shape=out_shape,
        compiler_params=_cparams(("parallel", "parallel", "arbitrary")),
        name=name,
    )(*args)
    return (yf, yb), (hf, hb)


def _even_out_kernel(mlf_ref, mlb_ref, ssdf_ref, ssdb_ref, o_ref, z_ref, xs_ref, mlg_ref, dsk_ref, ssg_ref,
                     out_ref):
    for h in range(ML_HEADS):
        sl = slice(h * ML_V_DIM, (h + 1) * ML_V_DIM)
        y = mlf_ref[0, :, sl].astype(F32) + mlb_ref[0, :, sl].astype(F32)
        og = o_ref[0, :, sl].astype(F32)
        out_ref[0, :, sl] = (_rms(y) * mlg_ref[:, sl] * jax.nn.sigmoid(og)).astype(out_ref.dtype)
    for g in range(SSM_GROUPS):
        sl = slice(g * SSM_GROUP_W, (g + 1) * SSM_GROUP_W)
        z = z_ref[0, :, sl].astype(F32)
        y = ((ssdf_ref[0, :, sl].astype(F32) + ssdb_ref[0, :, sl].astype(F32))
             + dsk_ref[:, sl] * xs_ref[0, :, sl].astype(F32))
        y = y * (z * jax.nn.sigmoid(z))
        so = slice(ML_V + g * SSM_GROUP_W, ML_V + (g + 1) * SSM_GROUP_W)
        out_ref[0, :, so] = (_rms(y) * ssg_ref[:, sl]).astype(out_ref.dtype)


def even_out(ml_y, ssd_y, proj, xbc, ml_norm_g, d_skip_full, ssm_norm_g, *, bt=128):
    b, t, _ = ml_y[0].shape
    bt = min(bt, t)
    w = ML_V
    tok = lambda blk: pl.BlockSpec((1, bt, w), lambda i, j: (i, j, blk))
    vec = pl.BlockSpec((1, w), lambda i, j: (0, 0))
    return pl.pallas_call(
        _even_out_kernel,
        grid=(b, t // bt),
        in_specs=[tok(0), tok(0), tok(0), tok(0), tok(E_O // w), tok(E_Z // w), tok(0), vec, vec, vec],
        out_specs=pl.BlockSpec((1, bt, 2 * w), lambda i, j: (i, j, 0)),
        out_shape=jax.ShapeDtypeStruct((b, t, 2 * w), BF16),
        compiler_params=_cparams(("parallel", "parallel")),
        name="even_gate",
    )(ml_y[0], ml_y[1], ssd_y[0], ssd_y[1], proj, proj, xbc, ml_norm_g.reshape(1, w), d_skip_full.reshape(1, w),
      ssm_norm_g.reshape(1, w))


def _hg_constants(reverse):
    l = HG_CHUNK
    pos = np.arange(l)
    tri = (pos[None, :] <= pos[:, None]).astype(np.float32)
    masks = []
    for lv in range(HG_LEVELS):
        m = 1 << lv
        blk = pos // (2 * m)
        later = (pos % (2 * m)) >= m
        masks.append(((blk[:, None] == blk[None, :]) & later[:, None] & (~later)[None, :]).astype(np.float32))
    masks = np.stack(masks)
    if reverse:
        tri = tri[::-1, ::-1]
        masks = masks[:, ::-1, ::-1]
    return (jnp.asarray(tri, BF16), jnp.asarray(masks.reshape(-1, l), F32))


def _hg_level_factors(q, k, f, cum, reverse):
    l, w = q.shape
    row = lax.broadcasted_iota(jnp.int32, (l, w), 0)
    sub = lax.broadcasted_iota(jnp.int32, (8, w), 0)
    zs = []
    for lv in range(HG_LEVELS):
        m = 1 << lv
        late = ((row & m) == 0) if reverse else ((row & m) != 0)
        if m >= 8:
            pieces = []
            for b0 in range(0, l, 2 * m):
                lo_rows, hi_rows = slice(b0, b0 + m), slice(b0 + m, b0 + 2 * m)
                if reverse:
                    mid = cum[b0 + m:b0 + m + 1]
                    pieces.append(q[lo_rows] * jnp.exp(cum[lo_rows] - mid))
                    pieces.append(k[hi_rows] * jnp.exp(mid - cum[hi_rows]))
                else:
                    mid = cum[b0 + m - 1:b0 + m]
                    pieces.append(k[lo_rows] * jnp.exp(mid - cum[lo_rows]))
                    pieces.append(q[hi_rows] * jnp.exp(cum[hi_rows] - mid))
            zs.append(jnp.concatenate(pieces, axis=0).astype(BF16))
            continue
        if m == 1:
            ex = jnp.where(late, f, 1.0)
        else:
            pieces = []
            if m >= 4:
                for b0 in range(0, l, 2 * m):
                    r = b0 + (m if reverse else m - 1)
                    pieces.append(cum[b0:b0 + 2 * m] - cum[r:r + 1])
            else:
                for b0 in range(0, l, 8):
                    ra, rb = (b0 + 2, b0 + 6) if reverse else (b0 + 1, b0 + 5)
                    mid = jnp.where(sub < 4, cum[ra:ra + 1], cum[rb:rb + 1])
                    pieces.append(cum[b0:b0 + 8] - mid)
            dd = pieces[0] if len(pieces) == 1 else jnp.concatenate(pieces, axis=0)
            ex = jnp.exp(jnp.where(late, dd, -dd))
        zs.append((jnp.where(late, q, k) * ex).astype(BF16))
    return zs


def _hg_chunk(q_ref, f_ref, i_ref, lb, tri, masks_ref, y_ref, s_ref, r0, reverse):
    l, e = HG_CHUNK, HG_EXPAND
    q_raw = q_ref[0, pl.ds(r0, l), :].astype(F32)
    f_raw = f_ref[0, pl.ds(r0, l), :].astype(F32)
    v = i_ref[0, pl.ds(r0, l), :]
    f = lb + (1.0 - lb) * jax.nn.sigmoid(f_raw)
    k = 1.0 - f
    lf = jnp.log(f)
    q = q_raw * jax.nn.sigmoid(q_raw) * (e ** -0.5)
    hi, mid, lo = _split3(lf)
    cum = (jnp.dot(tri, hi, preferred_element_type=F32)
           + jnp.dot(tri, mid, preferred_element_type=F32)
           + jnp.dot(tri, lo, preferred_element_type=F32))
    zs = _hg_level_factors(q, k, f, cum, reverse)
    end_row = 0 if reverse else l - 1
    cum_end = cum[end_row:end_row + 1, :]
    qc = (q * jnp.exp(cum)).astype(BF16)
    kt_end = (k * jnp.exp(cum_end - cum)).astype(BF16)
    s_decay = jnp.exp(cum_end)
    qk = q * k
    row = lax.broadcasted_iota(jnp.int32, (l, l), 0)
    col = lax.broadcasted_iota(jnp.int32, (l, l), 1)
    eye = row == col
    nt = (((1,), (1,)), ((), ()))
    tn = (((0,), (0,)), ((), ()))
    for hh in range(HG_HEADS_PER_STEP):
        sl = slice(hh * e, (hh + 1) * e)
        att = jnp.where(eye, jnp.sum(qk[:, sl], axis=1, keepdims=True), 0.0)
        for lv in range(HG_LEVELS):
            z = zs[lv][:, sl]
            a_lv = lax.dot_general(z, z, nt, preferred_element_type=F32)
            att = att + masks_ref[pl.ds(lv * l, l), :] * a_lv
        st = s_ref[0, hh]
        vh = v[:, sl]
        o = (jnp.dot(att.astype(BF16), vh, preferred_element_type=F32)
             + lax.dot_general(qc[:, sl], st.astype(BF16), nt, preferred_element_type=F32))
        y_ref[0, pl.ds(r0, l), sl] = o.astype(y_ref.dtype)
        upd = lax.dot_general(vh, kt_end[:, sl], tn, preferred_element_type=F32)
        s_ref[0, hh] = s_decay[:, sl] * st + upd


def _hg_kernel(qf_ref, ff_ref, if_ref, qb_ref, fb_ref, ib_ref, lbf_ref, lbb_ref, trif_ref, trib_ref,
               mf_ref, mb_ref, sf0_ref, sb0_ref, yf_ref, yb_ref, sf_ref, sb_ref, *, n_inner):
    l = HG_CHUNK

    @pl.when(pl.program_id(2) == 0)
    def _():
        sf_ref[...] = sf0_ref[...]
        sb_ref[...] = sb0_ref[...]

    lbf = lbf_ref[0]
    lbb = lbb_ref[0]
    trif = trif_ref[...]
    trib = trib_ref[...]

    def chunk(ci, carry):
        rf = pl.multiple_of(ci * l, l)
        rb = pl.multiple_of((n_inner - 1 - ci) * l, l)
        _hg_chunk(qf_ref, ff_ref, if_ref, lbf, trif, mf_ref, yf_ref, sf_ref, rf, False)
        _hg_chunk(qb_ref, fb_ref, ib_ref, lbb, trib, mb_ref, yb_ref, sb_ref, rb, True)
        return carry

    lax.fori_loop(0, n_inner, chunk, 0, unroll=4)


def hgrn2_scan(proj, lb, states, *, tb=512, name):
    b, t, _ = proj.shape
    tb = min(tb, t)
    nb = t // tb
    n_inner = tb // HG_CHUNK
    hw = HG_HEADS_PER_STEP * HG_EXPAND
    nh = HG_HEADS // HG_HEADS_PER_STEP
    qb_, ffb, fbb, ib_ = O_Q // hw, O_F // hw, (O_F + HG_F) // hw, O_I // hw
    trif, mf = _hg_constants(False)
    trib, mb = _hg_constants(True)
    fwd = lambda blk: pl.BlockSpec((1, tb, hw), lambda i, h, c: (i, c, blk + h))
    bwd = lambda blk: pl.BlockSpec((1, tb, hw), lambda i, h, c: (i, nb - 1 - c, blk + h))
    const = lambda a: pl.BlockSpec(a.shape, lambda i, h, c: (0,) * a.ndim)
    st_spec = pl.BlockSpec((1, HG_HEADS_PER_STEP, HG_V_DIM, HG_EXPAND), lambda i, h, c: (i, h, 0, 0))
    in_specs = [fwd(qb_), fwd(ffb), fwd(ib_), bwd(qb_), bwd(fbb), bwd(ib_),
                pl.BlockSpec((1, 1, hw), lambda i, h, c: (0, 0, h)),
                pl.BlockSpec((1, 1, hw), lambda i, h, c: (1, 0, h)),
                const(trif), const(trib), const(mf), const(mb), st_spec, st_spec]
    yf, yb, sf, sb = pl.pallas_call(
        functools.partial(_hg_kernel, n_inner=n_inner),
        grid=(b, nh, nb),
        in_specs=in_specs,
        out_specs=[fwd(0), bwd(0), st_spec, st_spec],
        out_shape=[jax.ShapeDtypeStruct((b, t, D_MODEL), BF16),
                   jax.ShapeDtypeStruct((b, t, D_MODEL), BF16),
                   jax.ShapeDtypeStruct(states[0].shape, F32),
                   jax.ShapeDtypeStruct(states[1].shape, F32)],
        compiler_params=_cparams(("parallel", "parallel", "arbitrary")),
        name=name,
    )(proj, proj, proj, proj, proj, proj, lb, lb, trif, trib, mf, mb, states[0], states[1])
    return (yf, yb), (sf, sb)


def _hg_out_kernel(of_ref, ob_ref, g_ref, ng_ref, out_ref):
    for h in range(of_ref.shape[-1] // HG_V_DIM):
        sl = slice(h * HG_V_DIM, (h + 1) * HG_V_DIM)
        g = g_ref[0, :, sl].astype(F32)
        o = of_ref[0, :, sl].astype(F32) + ob_ref[0, :, sl].astype(F32)
        out_ref[0, :, sl] = (_rms(o) * ng_ref[:, sl] * (g * jax.nn.sigmoid(g))).astype(out_ref.dtype)


def hg_out(o_f, o_b, proj, norm_g, *, from_colmajor, bt=256, bc=2048, name):
    b, t, d = o_f.shape
    bt = GRID_W if from_colmajor else min(bt, t)
    gb = O_G // bc
    nb = d // bc
    tok = pl.BlockSpec((1, bt, bc), lambda i, j, c: (i, j, c))
    if from_colmajor:
        assert t == GRID_W * GRID_W
        out_shape = jax.ShapeDtypeStruct((b, GRID_W, GRID_W * d), BF16)
        out_spec = pl.BlockSpec((1, GRID_W, bc), lambda i, j, c: (i, 0, j * nb + c))
    else:
        out_shape, out_spec = jax.ShapeDtypeStruct((b, t, d), BF16), tok
    out = pl.pallas_call(
        _hg_out_kernel,
        grid=(b, t // bt, nb),
        in_specs=[tok, tok,
                  pl.BlockSpec((1, bt, bc), lambda i, j, c: (i, j, gb + c)),
                  pl.BlockSpec((1, bc), lambda i, j, c: (0, c))],
        out_specs=out_spec,
        out_shape=out_shape,
        compiler_params=_cparams(("parallel", "parallel", "parallel")),
        name=name,
    )(o_f, o_b, proj, norm_g.reshape(1, d))
    return out.reshape(b, t, d)


def even_mixer(a_c, a_l, w_main, w_small, w_out, layer, ml_gate_b, ml_norm_g, conv_w, conv_b,
               dt_bias, a_log, d_skip, ssm_norm_g, need_ctx):
    bsz = a_l.shape[0]

    def features(a, tag):
        b, t, d = a.shape
        bm = min(1024, b * t)
        a2 = a.reshape(b * t, d)
        proj = matmul(a2, w_main, bm=bm, bn=1024, out_dtype=BF16, name="ab_in_" + tag).reshape(b, t, E_MAIN)
        small = matmul(a2, w_small, bm=bm, bn=128, out_dtype=F32, name="ab_in_small_" + tag).reshape(b, t, 128)
        gates = small[..., :16].reshape(b, t, 2, 2, ML_HEADS) + ml_gate_b.astype(F32)
        g_col = gates.transpose(2, 0, 4, 1, 3)
        g_row = gates.transpose(2, 0, 4, 3, 1)
        dt = small[..., 16:16 + 2 * SSM_HEADS].reshape(b, t, 2, SSM_GROUPS, SSM_HPG)
        dt_row = dt.transpose(2, 0, 3, 4, 1)
        xbc = conv_silu(proj, conv_w, conv_b)
        return dict(proj=proj, xbc=xbc, g_col=g_col, g_row=g_row, dt_row=dt_row)

    fc, fl = features(a_c, "ctx"), features(a_l, "lat")
    neg_a = -jnp.exp(a_log.astype(F32))
    par = jnp.stack([dt_bias.astype(F32), neg_a], axis=1).reshape(2, 2, SSM_GROUPS, SSM_HPG)
    par_row = par.transpose(0, 2, 3, 1)

    ml_init = (jnp.zeros((bsz, ML_HEADS, ML_QK_DIM, ML_V_DIM), F32),
               jnp.zeros((bsz, ML_HEADS, 1, ML_QK_DIM), F32),
               jnp.zeros((bsz, ML_HEADS, 1, 1), F32))
    ssd_init = jnp.zeros((bsz, SSM_GROUPS, SSM_STATE, SSM_GROUP_W), F32)

    def ml(f, st, tag):
        return mlstm_scan(f['proj'], f['g_col'], f['g_row'], st, name="mlstm_scan_" + tag)

    def ssd(f, st, tag):
        return ssd_scan(f['xbc'], f['dt_row'], par_row, st, name="ssd_scan_" + tag)

    ml_c, ml_st = ml(fc, (ml_init, ml_init), "ctx")
    ml_l, _ = ml(fl, ml_st, "lat")
    ssd_c, ssd_st = ssd(fc, (ssd_init, ssd_init), "ctx")
    ssd_l, _ = ssd(fl, ssd_st, "lat")
    d_full = jnp.repeat(d_skip.astype(F32), SSM_HEAD_DIM)

    def gated(f, ml_y, ssd_y):
        b, t, _ = ml_y[0].shape
        cat = even_out(ml_y, ssd_y, f['proj'], f['xbc'], ml_norm_g, d_full, ssm_norm_g)
        return cat.reshape(b * t, D_MODEL)

    cat_c = gated(fc, ml_c, ssd_c) if need_ctx else None
    y_l, y_c = dense2(gated(fl, ml_l, ssd_l), cat_c, w_out, layer, out_dtype=BF16, name="ab_out")
    y_l = y_l.reshape(a_l.shape)
    return (y_c.reshape(a_c.shape) if need_ctx else None), y_l


def hgrn2_mixer(a_c, a_l, w_in, w_out, layer, lb, norm_g, need_ctx, lat_colmajor=True):
    bsz = a_l.shape[0]
    lb3 = lb.astype(F32).reshape(2, 1, HG_F)

    flat = lambda a: a.reshape(a.shape[0] * a.shape[1], a.shape[2])
    pl_, pc = dense2(flat(a_l), flat(a_c), w_in, layer, out_dtype=BF16, name="hg_in")
    pl_ = pl_.reshape(a_l.shape[:2] + (5 * D_MODEL,))
    pc = pc.reshape(a_c.shape[:2] + (5 * D_MODEL,))
    init = jnp.zeros((bsz, HG_HEADS, HG_V_DIM, HG_EXPAND), F32)
    o_c, st = hgrn2_scan(pc, lb3, (init, init), name="hg_scan_ctx")
    o_l, _ = hgrn2_scan(pl_, lb3, st, name="hg_scan_lat")

    def gated(p, o, colmajor, tag):
        return flat(hg_out(o[0], o[1], p, norm_g, from_colmajor=colmajor, name="hg_gate_" + tag))

    g_c = gated(pc, o_c, False, "ctx") if need_ctx else None
    y_l, y_c = dense2(gated(pl_, o_l, lat_colmajor, "lat"), g_c, w_out, layer, out_dtype=BF16, name="hg_out")
    return (y_c.reshape(a_c.shape) if need_ctx else None), y_l.reshape(a_l.shape)


def ffn(a_l, a_c, wg, wu, wd, layer):
    flat = lambda a: a.reshape(a.shape[0] * a.shape[1], a.shape[2])
    hid = matmul_stream(flat(a_l), (wg, wu), layer, bm=2048, bn=256, out_dtype=BF16, name="ffn_up",
                        x2=None if a_c is None else flat(a_c))
    hid_l, hid_c = (hid, None) if a_c is None else hid
    f_l = dense(hid_l, wd, layer, out_dtype=BF16, name="ffn_down_lat", bm=512, bn=512).reshape(a_l.shape)
    if a_c is None:
        return f_l, None
    f_c = dense(hid_c, wd, layer, out_dtype=BF16, name="ffn_down_ctx", bm=512, bn=512).reshape(a_c.shape)
    return f_l, f_c


def kernel(x, c, ctx, c_ctx, ada_w, ada_b, norm_g, ffn_w_gate, ffn_w_up, ffn_w_down,
           ab_w_in, ab_w_out, ml_gate_b, ml_norm_g, ssm_conv_w, ssm_conv_b, ssm_dt_bias,
           ssm_a_log, ssm_d, ssm_norm_g, hg_w_in, hg_w_out, hg_lb, hg_norm_g):
    depth = ada_w.shape[0]
    bsz = x.shape[0]
    d = D_MODEL
    lb_p = jax.nn.softmax(hg_lb.astype(F32), axis=1)
    lower_bounds = jnp.cumsum(lb_p, axis=1) - lb_p[:, :1]
    cond = jnp.concatenate([jax.nn.silu(c), jax.nn.silu(c_ctx)[None, :],
                            jnp.zeros((8 - bsz - 1, d), F32)], axis=0)
    h_lat, h_ctx = x, ctx
    y_l = y_c = f_l = f_c = None
    gate_l = gate_c = g_prev = None
    for layer in range(depth):
        need_ctx = layer < depth - 1
        j = layer // 2
        mod = matmul(cond, ada_w, bm=8, bn=512, out_dtype=F32, bias=ada_b[layer], layer=layer,
                     name="ada_mod")
        mod_l = [mod[:bsz, i * d:(i + 1) * d][:, None, :] for i in range(6)]
        mod_c = [jnp.broadcast_to(mod[bsz, i * d:(i + 1) * d][None, None, :], (bsz, 1, d)) for i in range(6)]
        g = norm_g[layer]
        h_lat, a_l = resid_norm(h_lat, f_l, gate_l, g_prev, g[0], mod_l[0], mod_l[1],
                                a_colmajor=layer % 2 == 1)
        h_ctx, a_c = resid_norm(h_ctx, f_c, gate_c, g_prev, g[0], mod_c[0], mod_c[1])
        if layer % 2 == 0:
            w_in = ab_w_in[j]
            w_main = jnp.concatenate(
                [w_in[:, :6144], w_in[:, 6160:6160 + 2048 + SSM_CONV_DIM]], axis=1).astype(BF16)
            w_small = jnp.concatenate(
                [w_in[:, 6144:6160], w_in[:, 12304:12368], jnp.zeros((d, 128 - 80), F32)], axis=1).astype(BF16)
            y_c, y_l = even_mixer(a_c, a_l, w_main, w_small, ab_w_out, j, ml_gate_b[j],
                                  ml_norm_g[j], ssm_conv_w[j], ssm_conv_b[j], ssm_dt_bias[j],
                                  ssm_a_log[j], ssm_d[j], ssm_norm_g[j], need_ctx)
        else:
            y_c, y_l = hgrn2_mixer(a_c, a_l, hg_w_in, hg_w_out, j,
                                   lower_bounds[:, layer], hg_norm_g[j], need_ctx)
        h_lat, a2_l = resid_norm(h_lat, y_l, mod_l[2], g[1], g[2], mod_l[3], mod_l[4])
        a2_c = None
        if need_ctx:
            h_ctx, a2_c = resid_norm(h_ctx, y_c, mod_c[2], g[1], g[2], mod_c[3], mod_c[4])
        f_l, f_c = ffn(a2_l, a2_c, ffn_w_gate, ffn_w_up, ffn_w_down, layer)
        gate_l = mod_l[5]
        gate_c = mod_c[5] if need_ctx else None
        g_prev = g[3]
    h_lat, _ = resid_norm(h_lat, f_l, gate_l, g_prev)
    return h_lat
```

```python
import functools
import math

import jax
import jax.numpy as jnp
import numpy as np
from jax import lax
from jax.experimental import pallas as pl
from jax.experimental.pallas import tpu as pltpu

F32 = jnp.float32
BF16 = jnp.bfloat16

D_MODEL = 4096
GRID_W = 64
EPS = 1e-6

ML_HEADS = 4
ML_QK_DIM = 256
ML_V_DIM = 512
ML_QK = ML_HEADS * ML_QK_DIM
ML_V = ML_HEADS * ML_V_DIM
ML_CHUNK = 128
GATE_CAP = 15.0

SSM_HEAD_DIM = 64
SSM_INNER = 2048
SSM_HEADS = 32
SSM_GROUPS = 8
SSM_HPG = 4
SSM_STATE = 128
SSM_CONV = 5
SSM_CHUNK = 128
SSM_GROUP_W = SSM_HPG * SSM_HEAD_DIM
SSM_CONV_DIM = SSM_INNER + 2 * SSM_GROUPS * SSM_STATE
SSD_GROUPS_PER_STEP = 4

HG_EXPAND = 128
HG_HEADS = 32
HG_F = HG_HEADS * HG_EXPAND
HG_V_DIM = 128
HG_CHUNK = 64
HG_LEVELS = 6
HG_HEADS_PER_STEP = 4

FFN_HIDDEN = 11008

E_Q, E_K, E_V, E_O = 0, 1024, 2048, 4096
E_A = 6144
E_GATES = 16
E_Z, E_XBC = 0, 2048
E_B = SSM_INNER + SSM_CONV_DIM
O_Q, O_F, O_I, O_G = 0, 4096, 12288, 16384

VMEM_LIMIT = 56 * 1024 * 1024


def _cparams(sem):
    return pltpu.CompilerParams(dimension_semantics=sem, vmem_limit_bytes=VMEM_LIMIT)


def _mm_kernel(*refs, has_bias, cast_w):
    x_ref, w_ref = refs[0], refs[1]
    b_ref = refs[2] if has_bias else None
    o_ref = refs[2 + has_bias]
    if cast_w:
        wb_ref = refs[3 + has_bias]

        @pl.when(pl.program_id(1) == 0)
        def _():
            wb_ref[...] = w_ref[...].astype(BF16)

        w = wb_ref[...]
    else:
        w = w_ref[...]
    acc = jnp.dot(x_ref[...].astype(BF16), w, preferred_element_type=F32)
    if has_bias:
        acc = acc + b_ref[...]
    o_ref[...] = acc.astype(o_ref.dtype)


def matmul(x, w, *, bm, bn, out_dtype, name, bias=None, layer=None, n=None):
    m, k = x.shape
    n = w.shape[-1] if n is None else n
    assert m % bm == 0 and n % bn == 0, (m, bm, n, bn)
    if layer is None:
        w_spec = pl.BlockSpec((k, bn), lambda j, i: (0, j))
    else:
        w_spec = pl.BlockSpec((None, k, bn), lambda j, i: (layer, 0, j))
    in_specs = [pl.BlockSpec((bm, k), lambda j, i: (i, 0)), w_spec]
    args = [x, w]
    if bias is not None:
        in_specs.append(pl.BlockSpec((1, bn), lambda j, i: (0, j)))
        args.append(bias.reshape(1, n))
    cast_w = w.dtype != BF16
    return pl.pallas_call(
        functools.partial(_mm_kernel, has_bias=bias is not None, cast_w=cast_w),
        grid=(n // bn, m // bm),
        in_specs=in_specs,
        out_specs=pl.BlockSpec((bm, bn), lambda j, i: (i, j)),
        out_shape=jax.ShapeDtypeStruct((m, n), out_dtype),
        scratch_shapes=[pltpu.VMEM((k, bn), BF16)] if cast_w else [],
        compiler_params=_cparams(("parallel", "arbitrary" if cast_w else "parallel")),
        name=name,
    )(*args)


def _swiglu(g, u):
    return g * jax.nn.sigmoid(g) * u


def _mm_stream_kernel(*refs, n_w, n_x, nj, mi, kc):
    x_refs, w_refs = refs[:n_x], refs[n_x:n_x + n_w]
    o_refs, wb_refs = refs[n_x + n_w:2 * n_x + n_w], refs[2 * n_x + n_w:]
    j, i = pl.program_id(0), pl.program_id(1)
    slot = j % 2

    @pl.when((j < nj) & (i < mi))
    def _():
        r0 = pl.multiple_of(i * kc, 16)
        for w_ref, wb_ref in zip(w_refs, wb_refs):
            wb_ref[slot, pl.ds(r0, kc), :] = w_ref[...].astype(BF16)

    def product(x_ref, o_ref):
        x = x_ref[...]
        accs = [jnp.dot(x, wb_ref[1 - slot], preferred_element_type=F32) for wb_ref in wb_refs]
        out = _swiglu(*accs) if n_w == 2 else accs[0]
        o_ref[...] = out.astype(o_ref.dtype)

    @pl.when((j > 0) & (i < mi))
    def _():
        product(x_refs[0], o_refs[0])

    if n_x == 2:
        @pl.when((j > 0) & (i == mi))
        def _():
            product(x_refs[1], o_refs[1])


def matmul_stream(x, ws, layer, *, bm, bn, out_dtype, name, x2=None, n=None):
    m, k = x.shape
    n = ws[0].shape[-1] if n is None else n
    mi, nj = m // bm, n // bn
    kc = k // mi
    assert m % bm == 0 and n % bn == 0 and k % mi == 0 and kc % 16 == 0 and mi > 1, (m, bm, n, bn, k)
    n_x = 1 if x2 is None else 2
    row = lambda j, i: jnp.where(j == 0, 0, jnp.minimum(i, mi - 1))
    col = lambda j: jnp.maximum(j - 1, 0)
    w_spec = pl.BlockSpec(
        (None, kc, bn),
        lambda j, i: (layer, jnp.where(j < nj, jnp.minimum(i, mi - 1), mi - 1), jnp.minimum(j, nj - 1)))
    in_specs = [pl.BlockSpec((bm, k), lambda j, i: (row(j, i), 0))]
    out_specs = [pl.BlockSpec((bm, bn), lambda j, i: (row(j, i), col(j)))]
    out_shape = [jax.ShapeDtypeStruct((m, n), out_dtype)]
    args = [x]
    if x2 is not None:
        m2 = x2.shape[0]
        assert m2 <= bm and x2.shape[1] == k
        in_specs.append(pl.BlockSpec((m2, k), lambda j, i: (0, 0)))
        out_specs.append(pl.BlockSpec((m2, bn), lambda j, i: (0, col(j))))
        out_shape.append(jax.ShapeDtypeStruct((m2, n), out_dtype))
        args.append(x2)
    outs = pl.pallas_call(
        functools.partial(_mm_stream_kernel, n_w=len(ws), n_x=n_x, nj=nj, mi=mi, kc=kc),
        grid=(nj + 1, mi + n_x - 1),
        in_specs=in_specs + [w_spec] * len(ws),
        out_specs=out_specs,
        out_shape=out_shape,
        scratch_shapes=[pltpu.VMEM((2, k, bn), BF16) for _ in ws],
        compiler_params=_cparams(("arbitrary", "arbitrary")),
        name=name,
    )(*args, *ws)
    return outs[0] if x2 is None else (outs[0], outs[1])


def _swiglu_kernel(x_ref, wg_ref, wu_ref, o_ref):
    x = x_ref[...]
    g = jnp.dot(x, wg_ref[...].astype(BF16), preferred_element_type=F32)
    u = jnp.dot(x, wu_ref[...].astype(BF16), preferred_element_type=F32)
    o_ref[...] = _swiglu(g, u).astype(o_ref.dtype)


def swiglu_single(x, wg, wu, layer, *, bn, name):
    m, k = x.shape
    n = wg.shape[-1]
    assert n % bn == 0
    w_spec = pl.BlockSpec((None, k, bn), lambda j: (layer, 0, j))
    return pl.pallas_call(
        _swiglu_kernel,
        grid=(n // bn,),
        in_specs=[pl.BlockSpec((m, k), lambda j: (0, 0)), w_spec, w_spec],
        out_specs=pl.BlockSpec((m, bn), lambda j: (0, j)),
        out_shape=jax.ShapeDtypeStruct((m, n), BF16),
        compiler_params=_cparams(("parallel",)),
        name=name,
    )(x, wg, wu)


def dense2(x, x2, w, layer, *, out_dtype, name, bm=1024, bn=1024, n=None):
    if x.shape[0] <= bm:
        ncol = w.shape[-1] if n is None else n
        one = lambda a, nm: matmul(a, w, bm=a.shape[0], bn=256, out_dtype=out_dtype, layer=layer, name=nm,
                                   n=ncol)
        return one(x, name), (None if x2 is None else one(x2, name + "_ctx"))
    y = matmul_stream(x, (w,), layer, bm=bm, bn=bn, out_dtype=out_dtype, name=name, x2=x2, n=n)
    return (y, None) if x2 is None else y


def dense(x, w, layer, *, out_dtype, name, bm=1024, bn=1024):
    m = x.shape[0]
    if m > bm:
        return matmul_stream(x, (w,), layer, bm=bm, bn=bn, out_dtype=out_dtype, name=name)
    return matmul(x, w, bm=m, bn=256, out_dtype=out_dtype, layer=layer, name=name)


def _rms(x):
    return x * lax.rsqrt(jnp.mean(x * x, axis=-1, keepdims=True) + EPS)


def _resid_norm_kernel(*refs, has_y, want_a):
    it = iter(refs)
    h_ref = next(it)
    if has_y:
        y_ref, gate_ref, g1_ref = next(it), next(it), next(it)
    if want_a:
        g2_ref, shift_ref, scale_ref = next(it), next(it), next(it)
    if has_y:
        hout_ref = next(it)
    if want_a:
        a_ref = next(it)
    h = h_ref[0]
    if has_y:
        h = h + gate_ref[0] * (_rms(y_ref[0].astype(F32)) * g1_ref[...])
        hout_ref[0] = h
    if want_a:
        a = (_rms(h) * g2_ref[...]) * (1.0 + scale_ref[0]) + shift_ref[0]
        a_ref[0] = a.astype(a_ref.dtype)


def resid_norm(h, y=None, gate=None, g1=None, g2=None, shift=None, scale=None, *, bt=128):
    b, t, d = h.shape
    has_y, want_a = y is not None, g2 is not None
    bt = min(bt, t)
    tok = pl.BlockSpec((1, bt, d), lambda i, j: (i, j, 0))
    per_b = pl.BlockSpec((1, 1, d), lambda i, j: (i, 0, 0))
    vec = pl.BlockSpec((1, d), lambda i, j: (0, 0))
    args, in_specs, out_shape, out_specs = [h], [tok], [], []
    if has_y:
        args += [y, gate, g1.reshape(1, d)]
        in_specs += [tok, per_b, vec]
        out_shape.append(jax.ShapeDtypeStruct((b, t, d), F32))
        out_specs.append(tok)
    if want_a:
        args += [g2.reshape(1, d), shift, scale]
        in_specs += [vec, per_b, per_b]
        out_shape.append(jax.ShapeDtypeStruct((b, t, d), BF16))
        out_specs.append(tok)
    outs = pl.pallas_call(
        functools.partial(_resid_norm_kernel, has_y=has_y, want_a=want_a),
        grid=(b, t // bt),
        in_specs=in_specs,
        out_specs=out_specs,
        out_shape=out_shape,
        compiler_params=_cparams(("parallel", "parallel")),
        name="resid_norm",
    )(*args)
    outs = list(outs)
    h_new = outs.pop(0) if has_y else h
    a = outs.pop(0) if want_a else None
    return h_new, a


def _softplus(x):
    return jnp.maximum(x, 0.0) + jnp.log1p(jnp.exp(-jnp.abs(x)))


def _log_sigmoid(x):
    return jnp.minimum(x, 0.0) - jnp.log1p(jnp.exp(-jnp.abs(x)))


def _chunk_index(c, nc, reverse):
    return (nc - 1 - c) if reverse else c


def _mlstm_kernel(*refs):
    ins, outs = refs[:16], refs[16:]

    @pl.when(pl.program_id(2) == 0)
    def _():
        for d in range(2):
            for s in range(3):
                outs[4 * d + 1 + s][...] = ins[8 * d + 5 + s][...]

    for d in range(2):
        _mlstm_chunk(*ins[8 * d:8 * d + 5], *outs[4 * d:4 * d + 4], reverse=d == 1)


def _mlstm_chunk(q_ref, k_ref, v_ref, gc_ref, gr_ref, y_ref, c_ref, n_ref, m_ref, *, reverse):
    l = ML_CHUNK
    row = lax.broadcasted_iota(jnp.int32, (l, l), 0)
    col = lax.broadcasted_iota(jnp.int32, (l, l), 1)
    mask = (col >= row) if reverse else (col <= row)
    mask_t = (col <= row) if reverse else (col >= row)

    gcol = gc_ref[0, 0, 0]
    grow = gr_ref[0, 0, 0]
    ig_c = GATE_CAP * jnp.tanh(gcol[:, 0:1] / GATE_CAP)
    lf_c = _log_sigmoid(GATE_CAP * jnp.tanh(gcol[:, 1:2] / GATE_CAP))
    ig_r = GATE_CAP * jnp.tanh(grow[0:1, :] / GATE_CAP)
    lf_r = _log_sigmoid(GATE_CAP * jnp.tanh(grow[1:2, :] / GATE_CAP))

    b_c = jnp.sum(jnp.where(mask, lf_r, 0.0), axis=1, keepdims=True)
    b_r = jnp.sum(jnp.where(mask_t, lf_c, 0.0), axis=0, keepdims=True)
    total = jnp.sum(lf_c, axis=0, keepdims=True)

    m_prev = m_ref[0, 0]
    logw = jnp.where(mask, b_c - b_r + ig_r, -jnp.inf)
    inter = b_c + m_prev
    m_t = jnp.maximum(jnp.max(logw, axis=1, keepdims=True), inter)

    q = q_ref[0]
    k = k_ref[0]
    v = v_ref[0]
    scale = ML_QK_DIM ** -0.5
    qk = lax.dot_general(q, k, (((1,), (1,)), ((), ())), preferred_element_type=F32)
    s = qk * scale * jnp.exp(logw - m_t)
    inter_w = jnp.exp(inter - m_t) * scale

    c_st = c_ref[0, 0]
    n_st = n_ref[0, 0]
    num = (jnp.dot(s.astype(BF16), v, preferred_element_type=F32)
           + inter_w * jnp.dot(q, c_st.astype(BF16), preferred_element_type=F32))
    qn = jnp.sum(q.astype(F32) * n_st, axis=1, keepdims=True)
    den = jnp.sum(s, axis=1, keepdims=True) + inter_w * qn
    y_ref[0] = (num * (1.0 / jnp.maximum(jnp.abs(den), jnp.exp(-m_t)))).astype(y_ref.dtype)

    g_c = total - b_c + ig_c
    g_r = total - b_r + ig_r
    m_new = jnp.maximum(total + m_prev, jnp.max(g_r, axis=1, keepdims=True))
    w_c = jnp.exp(g_c - m_new)
    decay = jnp.exp(total + m_prev - m_new)
    kw = k.astype(F32) * w_c
    kv = lax.dot_general(kw.astype(BF16), v, (((0,), (0,)), ((), ())), preferred_element_type=F32)
    c_ref[0, 0] = decay * c_st + kv
    n_ref[0, 0] = decay * n_st + jnp.sum(kw, axis=0, keepdims=True)
    m_ref[0, 0] = m_new


def mlstm_scan(proj, gates_col, gates_row, states, *, name):
    b, t, _ = proj.shape
    l = ML_CHUNK
    nc = t // l
    qb, kb, vb = E_Q // ML_QK_DIM, E_K // ML_QK_DIM, E_V // ML_V_DIM
    st_specs = [pl.BlockSpec((1, 1, ML_QK_DIM, ML_V_DIM), lambda i, h, c: (i, h, 0, 0)),
                pl.BlockSpec((1, 1, 1, ML_QK_DIM), lambda i, h, c: (i, h, 0, 0)),
                pl.BlockSpec((1, 1, 1, 1), lambda i, h, c: (i, h, 0, 0))]
    in_specs, args, out_specs, out_shape = [], [], [], []
    for d in range(2):
        ci = functools.partial(_chunk_index, nc=nc, reverse=d == 1)
        in_specs += [
            pl.BlockSpec((1, l, ML_QK_DIM), lambda i, h, c, ci=ci: (i, ci(c), qb + h)),
            pl.BlockSpec((1, l, ML_QK_DIM), lambda i, h, c, ci=ci: (i, ci(c), kb + h)),
            pl.BlockSpec((1, l, ML_V_DIM), lambda i, h, c, ci=ci: (i, ci(c), vb + h)),
            pl.BlockSpec((1, 1, 1, l, 2), lambda i, h, c, ci=ci, d=d: (d, i, h, ci(c), 0)),
            pl.BlockSpec((1, 1, 1, 2, l), lambda i, h, c, ci=ci, d=d: (d, i, h, 0, ci(c))),
        ] + st_specs
        args += [proj, proj, proj, gates_col, gates_row, *states[d]]
        out_specs += [pl.BlockSpec((1, l, ML_V_DIM), lambda i, h, c, ci=ci: (i, ci(c), h))] + st_specs
        out_shape += [jax.ShapeDtypeStruct((b, t, ML_V), BF16)]
        out_shape += [jax.ShapeDtypeStruct(s.shape, F32) for s in states[d]]
    outs = pl.pallas_call(
        _mlstm_kernel,
        grid=(b, ML_HEADS, nc),
        in_specs=in_specs,
        out_specs=out_specs,
        out_shape=out_shape,
        compiler_params=_cparams(("parallel", "parallel", "arbitrary")),
        name=name,
    )(*args)
    return (outs[0], outs[4]), (tuple(outs[1:4]), tuple(outs[5:8]))


CONV_ROWS = 256
CONV_HALO = 16


def _conv_kernel(x_ref, w_ref, b_ref, o_ref, *, t):
    w = w_ref[...]
    bias = b_ref[...]
    r = min(CONV_ROWS, t)
    half = SSM_CONV // 2
    cw = x_ref.shape[-1]
    for r0 in range(0, t, r):
        parts = []
        if r0 == 0:
            parts.append(jnp.zeros((CONV_HALO, cw), F32))
        else:
            parts.append(x_ref[0, pl.ds(r0 - CONV_HALO, CONV_HALO), :].astype(F32))
        parts.append(x_ref[0, pl.ds(r0, r), :].astype(F32))
        if r0 + r == t:
            parts.append(jnp.zeros((CONV_HALO, cw), F32))
        else:
            parts.append(x_ref[0, pl.ds(r0 + r, CONV_HALO), :].astype(F32))
        win = jnp.concatenate(parts, axis=0)
        n = r + 2 * CONV_HALO
        acc = jnp.zeros((r, cw), F32) + bias
        for kk in range(SSM_CONV):
            sh = (half - kk) % n
            rolled = win if sh == 0 else pltpu.roll(win, sh, 0)
            acc = acc + w[kk:kk + 1, :] * rolled[CONV_HALO:CONV_HALO + r, :]
        o_ref[0, pl.ds(r0, r), :] = (acc * jax.nn.sigmoid(acc)).astype(o_ref.dtype)


def conv_silu(proj, conv_w, conv_b, *, bc=512):
    b, t, _ = proj.shape
    c = SSM_CONV_DIM
    off = E_XBC // bc
    return pl.pallas_call(
        functools.partial(_conv_kernel, t=t),
        grid=(b, c // bc),
        in_specs=[pl.BlockSpec((1, t, bc), lambda i, j: (i, 0, off + j)),
                  pl.BlockSpec((SSM_CONV, bc), lambda i, j: (0, j)),
                  pl.BlockSpec((1, bc), lambda i, j: (0, j))],
        out_specs=pl.BlockSpec((1, t, bc), lambda i, j: (i, 0, j)),
        out_shape=jax.ShapeDtypeStruct((b, t, c), BF16),
        compiler_params=_cparams(("parallel", "parallel")),
        name="conv_silu",
    )(proj, conv_w, conv_b.reshape(1, c))


def _split3(x):
    hi = x.astype(BF16)
    r1 = x - hi.astype(F32)
    mid = r1.astype(BF16)
    lo = (r1 - mid.astype(F32)).astype(BF16)
    return hi, mid, lo


def _dot3(dims, parts_lhs, rhs):
    return sum(lax.dot_general(p, rhs, dims, preferred_element_type=F32) for p in parts_lhs)


def _head_selector(heads, width):
    rows = lax.broadcasted_iota(jnp.int32, (heads, heads * width), 0)
    lanes = lax.broadcasted_iota(jnp.int32, (heads, heads * width), 1)
    return (lanes // width == rows).astype(BF16)


def _ssd_kernel(*refs):
    ins, outs = refs[:12], refs[12:]

    @pl.when(pl.program_id(2) == 0)
    def _():
        for d in range(2):
            outs[2 * d + 1][...] = ins[6 * d + 5][...]

    for d in range(2):
        _ssd_chunk(*ins[6 * d:6 * d + 5], *outs[2 * d:2 * d + 2], reverse=d == 1)


def _ssd_chunk(xs_ref, bm_ref, cm_ref, dtr_ref, pr_ref, y_ref, h_ref, *, reverse):
    l = SSM_CHUNK
    gp = SSD_GROUPS_PER_STEP
    nh = gp * SSM_HPG
    row = lax.broadcasted_iota(jnp.int32, (l, l), 0)
    col = lax.broadcasted_iota(jnp.int32, (l, l), 1)
    mask = (col >= row) if reverse else (col <= row)
    mask_t = (col <= row) if reverse else (col >= row)

    nn = (((1,), (0,)), ((), ()))
    nt = (((1,), (1,)), ((), ()))
    tn = (((0,), (0,)), ((), ()))
    pr = pr_ref[0, 0]
    dt_r = _softplus(dtr_ref[0, 0, 0] + pr[:, 0:1])
    a_r = dt_r * pr[:, 1:2]
    cum_r = _dot3(nn, _split3(a_r), mask_t.astype(BF16))
    dt_parts, cum_parts = _split3(dt_r), _split3(cum_r)
    sel_ch = _head_selector(nh, SSM_HEAD_DIM)
    dt_ch = _dot3(tn, dt_parts, sel_ch)
    cum_ch = _dot3(tn, cum_parts, sel_ch)
    cum_bc = _dot3(tn, cum_parts, _head_selector(nh, l))
    end_row = 0 if reverse else l - 1
    total_ch = cum_ch[end_row:end_row + 1, :]

    xdt = xs_ref[0].astype(F32) * dt_ch
    e_cum = jnp.exp(cum_ch)
    xw = (xdt * jnp.exp(total_ch - cum_ch)).astype(BF16)
    s_decay = jnp.exp(total_ch)
    lane_head = lax.broadcasted_iota(jnp.int32, (l, SSM_GROUP_W), 1) // SSM_HEAD_DIM
    for g in range(gp):
        xsl = slice(g * SSM_GROUP_W, (g + 1) * SSM_GROUP_W)
        nsl = slice(g * SSM_STATE, (g + 1) * SSM_STATE)
        bm = bm_ref[0, :, nsl]
        cm = cm_ref[0, :, nsl]
        hst = h_ref[0, g]
        cb = lax.dot_general(cm, bm, nt, preferred_element_type=F32)
        inter = jnp.dot(cm, hst.astype(BF16), preferred_element_type=F32)
        xdt_g = xdt[:, xsl]
        ws, xjs = [], []
        for j in range(SSM_HPG):
            hd = g * SSM_HPG + j
            decay = jnp.exp(jnp.where(mask, cum_bc[:, hd * l:(hd + 1) * l] - cum_r[hd:hd + 1, :], -jnp.inf))
            ws.append((decay * cb).astype(BF16))
            xjs.append(jnp.where(lane_head == j, xdt_g, 0.0).astype(BF16))
        y = jnp.dot(jnp.concatenate(ws, axis=1), jnp.concatenate(xjs, axis=0), preferred_element_type=F32)
        y_ref[0, :, xsl] = (y + e_cum[:, xsl] * inter).astype(y_ref.dtype)
        upd = lax.dot_general(bm, xw[:, xsl], tn, preferred_element_type=F32)
        h_ref[0, g] = s_decay[:, xsl] * hst + upd


def ssd_scan(xbc, dt_row, par_row, states, *, name):
    b, t, _ = xbc.shape
    l = SSM_CHUNK
    nc = t // l
    gp = SSD_GROUPS_PER_STEP
    xw, nw = gp * SSM_GROUP_W, gp * SSM_STATE
    bmb = SSM_INNER // nw
    cmb = bmb + SSM_GROUPS // gp
    st_spec = pl.BlockSpec((1, gp, SSM_STATE, SSM_GROUP_W), lambda i, g, c: (i, g, 0, 0))
    dt_step = dt_row.reshape(2, b, SSM_GROUPS // gp, gp * SSM_HPG, t)
    par_step = par_row.reshape(2, SSM_GROUPS // gp, gp * SSM_HPG, 2)
    in_specs, args, out_specs, out_shape = [], [], [], []
    for d in range(2):
        ci = functools.partial(_chunk_index, nc=nc, reverse=d == 1)
        in_specs += [
            pl.BlockSpec((1, l, xw), lambda i, g, c, ci=ci: (i, ci(c), g)),
            pl.BlockSpec((1, l, nw), lambda i, g, c, ci=ci: (i, ci(c), bmb + g)),
            pl.BlockSpec((1, l, nw), lambda i, g, c, ci=ci: (i, ci(c), cmb + g)),
            pl.BlockSpec((1, 1, 1, gp * SSM_HPG, l), lambda i, g, c, ci=ci, d=d: (d, i, g, 0, ci(c))),
            pl.BlockSpec((1, 1, gp * SSM_HPG, 2), lambda i, g, c, d=d: (d, g, 0, 0)),
            st_spec,
        ]
        args += [xbc, xbc, xbc, dt_step, par_step, states[d]]
        out_specs += [pl.BlockSpec((1, l, xw), lambda i, g, c, ci=ci: (i, ci(c), g)), st_spec]
        out_shape += [jax.ShapeDtypeStruct((b, t, SSM_INNER), BF16),
                      jax.ShapeDtypeStruct(states[d].shape, F32)]
    yf, hf, yb, hb = pl.pallas_call(
        _ssd_kernel,
        grid=(b, SSM_GROUPS // gp, nc),
        in_specs=in_specs,
        out_specs=out_specs,
        out_shape=out_shape,
        compiler_params=_cparams(("parallel", "parallel", "arbitrary")),
        name=name,
    )(*args)
    return (yf, yb), (hf, hb)


def _even_out_kernel(mlf_ref, mlb_ref, ssdf_ref, ssdb_ref, o_ref, z_ref, xs_ref, mlg_ref, dsk_ref, ssg_ref,
                     out_ref):
    for h in range(ML_HEADS):
        sl = slice(h * ML_V_DIM, (h + 1) * ML_V_DIM)
        y = mlf_ref[0, :, sl].astype(F32) + mlb_ref[0, :, sl].astype(F32)
        og = o_ref[0, :, sl].astype(F32)
        out_ref[0, :, sl] = (_rms(y) * mlg_ref[:, sl] * jax.nn.sigmoid(og)).astype(out_ref.dtype)
    for g in range(SSM_GROUPS):
        sl = slice(g * SSM_GROUP_W, (g + 1) * SSM_GROUP_W)
        z = z_ref[0, :, sl].astype(F32)
        y = ((ssdf_ref[0, :, sl].astype(F32) + ssdb_ref[0, :, sl].astype(F32))
             + dsk_ref[:, sl] * xs_ref[0, :, sl].astype(F32))
        y = y * (z * jax.nn.sigmoid(z))
        so = slice(ML_V + g * SSM_GROUP_W, ML_V + (g + 1) * SSM_GROUP_W)
        out_ref[0, :, so] = (_rms(y) * ssg_ref[:, sl]).astype(out_ref.dtype)


def even_out(ml_y, ssd_y, proj_a, proj_b, xbc, ml_norm_g, d_skip_full, ssm_norm_g, *, bt=128):
    b, t, _ = ml_y[0].shape
    bt = min(bt, t)
    w = ML_V
    tok = lambda blk: pl.BlockSpec((1, bt, w), lambda i, j: (i, j, blk))
    vec = pl.BlockSpec((1, w), lambda i, j: (0, 0))
    return pl.pallas_call(
        _even_out_kernel,
        grid=(b, t // bt),
        in_specs=[tok(0), tok(0), tok(0), tok(0), tok(E_O // w), tok(E_Z // w), tok(0), vec, vec, vec],
        out_specs=pl.BlockSpec((1, bt, 2 * w), lambda i, j: (i, j, 0)),
        out_shape=jax.ShapeDtypeStruct((b, t, 2 * w), BF16),
        compiler_params=_cparams(("parallel", "parallel")),
        name="even_gate",
    )(ml_y[0], ml_y[1], ssd_y[0], ssd_y[1], proj_a, proj_b, xbc, ml_norm_g.reshape(1, w), d_skip_full.reshape(1, w),
      ssm_norm_g.reshape(1, w))


def _hg_constants(reverse):
    l = HG_CHUNK
    pos = np.arange(l)
    tri = (pos[None, :] <= pos[:, None]).astype(np.float32)
    masks = []
    for lv in range(HG_LEVELS):
        m = 1 << lv
        blk = pos // (2 * m)
        later = (pos % (2 * m)) >= m
        masks.append(((blk[:, None] == blk[None, :]) & later[:, None] & (~later)[None, :]).astype(np.float32))
    masks = np.stack(masks)
    if reverse:
        tri = tri[::-1, ::-1]
        masks = masks[:, ::-1, ::-1]
    return (jnp.asarray(tri, BF16), jnp.asarray(masks.reshape(-1, l), F32))


def _hg_level_factors(q, k, f, cum, reverse):
    l, w = q.shape
    row = lax.broadcasted_iota(jnp.int32, (l, w), 0)
    sub = lax.broadcasted_iota(jnp.int32, (8, w), 0)
    zs = []
    for lv in range(HG_LEVELS):
        m = 1 << lv
        late = ((row & m) == 0) if reverse else ((row & m) != 0)
        if m >= 8:
            pieces = []
            for b0 in range(0, l, 2 * m):
                lo_rows, hi_rows = slice(b0, b0 + m), slice(b0 + m, b0 + 2 * m)
                if reverse:
                    mid = cum[b0 + m:b0 + m + 1]
                    pieces.append(q[lo_rows] * jnp.exp(cum[lo_rows] - mid))
                    pieces.append(k[hi_rows] * jnp.exp(mid - cum[hi_rows]))
                else:
                    mid = cum[b0 + m - 1:b0 + m]
                    pieces.append(k[lo_rows] * jnp.exp(mid - cum[lo_rows]))
                    pieces.append(q[hi_rows] * jnp.exp(cum[hi_rows] - mid))
            zs.append(jnp.concatenate(pieces, axis=0).astype(BF16))
            continue
        if m == 1:
            ex = jnp.where(late, f, 1.0)
        else:
            pieces = []
            if m >= 4:
                for b0 in range(0, l, 2 * m):
                    r = b0 + (m if reverse else m - 1)
                    pieces.append(cum[b0:b0 + 2 * m] - cum[r:r + 1])
            else:
                for b0 in range(0, l, 8):
                    ra, rb = (b0 + 2, b0 + 6) if reverse else (b0 + 1, b0 + 5)
                    mid = jnp.where(sub < 4, cum[ra:ra + 1], cum[rb:rb + 1])
                    pieces.append(cum[b0:b0 + 8] - mid)
            dd = pieces[0] if len(pieces) == 1 else jnp.concatenate(pieces, axis=0)
            ex = jnp.exp(jnp.where(late, dd, -dd))
        zs.append((jnp.where(late, q, k) * ex).astype(BF16))
    return zs


def _hg_chunk(q_ref, f_ref, i_ref, lb, tri, masks_ref, y_ref, s_ref, r0, reverse):
    l, e = HG_CHUNK, HG_EXPAND
    q_raw = q_ref[0, pl.ds(r0, l), :].astype(F32)
    f_raw = f_ref[0, pl.ds(r0, l), :].astype(F32)
    v = i_ref[0, pl.ds(r0, l), :]
    f = lb + (1.0 - lb) * jax.nn.sigmoid(f_raw)
    k = 1.0 - f
    lf = jnp.log(f)
    q = q_raw * jax.nn.sigmoid(q_raw) * (e ** -0.5)
    hi, mid, lo = _split3(lf)
    cum = (jnp.dot(tri, hi, preferred_element_type=F32)
           + jnp.dot(tri, mid, preferred_element_type=F32)
           + jnp.dot(tri, lo, preferred_element_type=F32))
    zs = _hg_level_factors(q, k, f, cum, reverse)
    end_row = 0 if reverse else l - 1
    cum_end = cum[end_row:end_row + 1, :]
    qc = (q * jnp.exp(cum)).astype(BF16)
    kt_end = (k * jnp.exp(cum_end - cum)).astype(BF16)
    s_decay = jnp.exp(cum_end)
    qk = q * k
    row = lax.broadcasted_iota(jnp.int32, (l, l), 0)
    col = lax.broadcasted_iota(jnp.int32, (l, l), 1)
    eye = row == col
    nt = (((1,), (1,)), ((), ()))
    tn = (((0,), (0,)), ((), ()))
    for hh in range(HG_HEADS_PER_STEP):
        sl = slice(hh * e, (hh + 1) * e)
        att = jnp.where(eye, jnp.sum(qk[:, sl], axis=1, keepdims=True), 0.0)
        for lv in range(HG_LEVELS):
            z = zs[lv][:, sl]
            a_lv = lax.dot_general(z, z, nt, preferred_element_type=F32)
            att = att + masks_ref[pl.ds(lv * l, l), :] * a_lv
        st = s_ref[0, hh]
        vh = v[:, sl]
        o = (jnp.dot(att.astype(BF16), vh, preferred_element_type=F32)
             + lax.dot_general(qc[:, sl], st.astype(BF16), nt, preferred_element_type=F32))
        y_ref[0, pl.ds(r0, l), sl] = o.astype(y_ref.dtype)
        upd = lax.dot_general(vh, kt_end[:, sl], tn, preferred_element_type=F32)
        s_ref[0, hh] = s_decay[:, sl] * st + upd


def _hg_kernel(qf_ref, ff_ref, if_ref, qb_ref, fb_ref, ib_ref, lbf_ref, lbb_ref, trif_ref, trib_ref,
               mf_ref, mb_ref, sf0_ref, sb0_ref, yf_ref, yb_ref, sf_ref, sb_ref, *, n_inner):
    l = HG_CHUNK

    @pl.when(pl.program_id(2) == 0)
    def _():
        sf_ref[...] = sf0_ref[...]
        sb_ref[...] = sb0_ref[...]

    lbf = lbf_ref[0]
    lbb = lbb_ref[0]
    trif = trif_ref[...]
    trib = trib_ref[...]

    def chunk(ci, carry):
        rf = pl.multiple_of(ci * l, l)
        rb = pl.multiple_of((n_inner - 1 - ci) * l, l)
        _hg_chunk(qf_ref, ff_ref, if_ref, lbf, trif, mf_ref, yf_ref, sf_ref, rf, False)
        _hg_chunk(qb_ref, fb_ref, ib_ref, lbb, trib, mb_ref, yb_ref, sb_ref, rb, True)
        return carry

    lax.fori_loop(0, n_inner, chunk, 0, unroll=4)


def hgrn2_scan(proj, lb, states, *, tb=512, name):
    b, t, _ = proj.shape
    tb = min(tb, t)
    nb = t // tb
    n_inner = tb // HG_CHUNK
    hw = HG_HEADS_PER_STEP * HG_EXPAND
    nh = HG_HEADS // HG_HEADS_PER_STEP
    qb_, ffb, fbb, ib_ = O_Q // hw, O_F // hw, (O_F + HG_F) // hw, O_I // hw
    trif, mf = _hg_constants(False)
    trib, mb = _hg_constants(True)
    fwd = lambda blk: pl.BlockSpec((1, tb, hw), lambda i, h, c: (i, c, blk + h))
    bwd = lambda blk: pl.BlockSpec((1, tb, hw), lambda i, h, c: (i, nb - 1 - c, blk + h))
    const = lambda a: pl.BlockSpec(a.shape, lambda i, h, c: (0,) * a.ndim)
    st_spec = pl.BlockSpec((1, HG_HEADS_PER_STEP, HG_V_DIM, HG_EXPAND), lambda i, h, c: (i, h, 0, 0))
    in_specs = [fwd(qb_), fwd(ffb), fwd(ib_), bwd(qb_), bwd(fbb), bwd(ib_),
                pl.BlockSpec((1, 1, hw), lambda i, h, c: (0, 0, h)),
                pl.BlockSpec((1, 1, hw), lambda i, h, c: (1, 0, h)),
                const(trif), const(trib), const(mf), const(mb), st_spec, st_spec]
    yf, yb, sf, sb = pl.pallas_call(
        functools.partial(_hg_kernel, n_inner=n_inner),
        grid=(b, nh, nb),
        in_specs=in_specs,
        out_specs=[fwd(0), bwd(0), st_spec, st_spec],
        out_shape=[jax.ShapeDtypeStruct((b, t, D_MODEL), BF16),
                   jax.ShapeDtypeStruct((b, t, D_MODEL), BF16),
                   jax.ShapeDtypeStruct(states[0].shape, F32),
                   jax.ShapeDtypeStruct(states[1].shape, F32)],
        compiler_params=_cparams(("parallel", "parallel", "arbitrary")),
        name=name,
    )(proj, proj, proj, proj, proj, proj, lb, lb, trif, trib, mf, mb, states[0], states[1])
    return (yf, yb), (sf, sb)


def _hg_out_kernel(of_ref, ob_ref, g_ref, ng_ref, out_ref):
    for h in range(of_ref.shape[-1] // HG_V_DIM):
        sl = slice(h * HG_V_DIM, (h + 1) * HG_V_DIM)
        g = g_ref[0, :, sl].astype(F32)
        o = of_ref[0, :, sl].astype(F32) + ob_ref[0, :, sl].astype(F32)
        out_ref[0, :, sl] = (_rms(o) * ng_ref[:, sl] * (g * jax.nn.sigmoid(g))).astype(out_ref.dtype)


def hg_out(o_f, o_b, proj, norm_g, *, bt=256, bc=1024, name):
    b, t, d = o_f.shape
    bt = min(bt, t)
    gb = O_G // bc
    tok = pl.BlockSpec((1, bt, bc), lambda i, j, c: (i, j, c))
    return pl.pallas_call(
        _hg_out_kernel,
        grid=(b, t // bt, d // bc),
        in_specs=[tok, tok,
                  pl.BlockSpec((1, bt, bc), lambda i, j, c: (i, j, gb + c)),
                  pl.BlockSpec((1, bc), lambda i, j, c: (0, c))],
        out_specs=tok,
        out_shape=jax.ShapeDtypeStruct((b, t, d), BF16),
        compiler_params=_cparams(("parallel", "parallel", "parallel")),
        name=name,
    )(o_f, o_b, proj, norm_g.reshape(1, d))


def even_mixer(a_c, a_l, w_in, w_zx, w_small, w_out, layer, ml_gate_b, ml_norm_g, conv_w, conv_b,
               dt_bias, a_log, d_skip, ssm_norm_g, need_ctx):
    bsz = a_l.shape[0]
    flat = lambda a: a.reshape(a.shape[0] * a.shape[1], a.shape[2])
    pa_l, pa_c = dense2(flat(a_l), flat(a_c), w_in, layer, out_dtype=BF16, name="ab_in_a", n=E_A)

    def features(a, proj_a, tag):
        b, t, d = a.shape
        bm = min(1024, b * t)
        a2 = a.reshape(b * t, d)
        proj_a = proj_a.reshape(b, t, E_A)
        proj = matmul(a2, w_zx, bm=bm, bn=1024, out_dtype=BF16, name="ab_in_b_" + tag).reshape(b, t, E_B)
        small = matmul(a2, w_small, bm=bm, bn=128, out_dtype=F32, name="ab_in_small_" + tag).reshape(b, t, 128)
        gates = small[..., :16].reshape(b, t, 2, 2, ML_HEADS) + ml_gate_b.astype(F32)
        g_col = gates.transpose(2, 0, 4, 1, 3)
        g_row = gates.transpose(2, 0, 4, 3, 1)
        dt = small[..., 16:16 + 2 * SSM_HEADS].reshape(b, t, 2, SSM_GROUPS, SSM_HPG)
        dt_row = dt.transpose(2, 0, 3, 4, 1)
        xbc = conv_silu(proj, conv_w, conv_b)
        return dict(proj_a=proj_a, proj=proj, xbc=xbc, g_col=g_col, g_row=g_row, dt_row=dt_row)

    fc, fl = features(a_c, pa_c, "ctx"), features(a_l, pa_l, "lat")
    neg_a = -jnp.exp(a_log.astype(F32))
    par = jnp.stack([dt_bias.astype(F32), neg_a], axis=1).reshape(2, 2, SSM_GROUPS, SSM_HPG)
    par_row = par.transpose(0, 2, 3, 1)

    ml_init = (jnp.zeros((bsz, ML_HEADS, ML_QK_DIM, ML_V_DIM), F32),
               jnp.zeros((bsz, ML_HEADS, 1, ML_QK_DIM), F32),
               jnp.zeros((bsz, ML_HEADS, 1, 1), F32))
    ssd_init = jnp.zeros((bsz, SSM_GROUPS, SSM_STATE, SSM_GROUP_W), F32)

    def ml(f, st, tag):
        return mlstm_scan(f['proj_a'], f['g_col'], f['g_row'], st, name="mlstm_scan_" + tag)

    def ssd(f, st, tag):
        return ssd_scan(f['xbc'], f['dt_row'], par_row, st, name="ssd_scan_" + tag)

    ml_c, ml_st = ml(fc, (ml_init, ml_init), "ctx")
    ml_l, _ = ml(fl, ml_st, "lat")
    ssd_c, ssd_st = ssd(fc, (ssd_init, ssd_init), "ctx")
    ssd_l, _ = ssd(fl, ssd_st, "lat")
    d_full = jnp.repeat(d_skip.astype(F32), SSM_HEAD_DIM)

    def gated(f, ml_y, ssd_y):
        b, t, _ = ml_y[0].shape
        cat = even_out(ml_y, ssd_y, f['proj_a'], f['proj'], f['xbc'], ml_norm_g, d_full, ssm_norm_g)
        return cat.reshape(b * t, D_MODEL)

    cat_c = gated(fc, ml_c, ssd_c) if need_ctx else None
    y_l, y_c = dense2(gated(fl, ml_l, ssd_l), cat_c, w_out, layer, out_dtype=BF16, name="ab_out")
    y_l = y_l.reshape(a_l.shape)
    return (y_c.reshape(a_c.shape) if need_ctx else None), y_l


def hgrn2_mixer(a_c, a_l, w_in, w_out, layer, lb, norm_g, need_ctx, lat_colmajor=True):
    bsz = a_l.shape[0]
    lb3 = lb.astype(F32).reshape(2, 1, HG_F)

    flat = lambda a: a.reshape(a.shape[0] * a.shape[1], a.shape[2])
    pl_, pc = dense2(flat(a_l), flat(a_c), w_in, layer, out_dtype=BF16, name="hg_in")
    pl_ = pl_.reshape(a_l.shape[:2] + (5 * D_MODEL,))
    pc = pc.reshape(a_c.shape[:2] + (5 * D_MODEL,))
    init = jnp.zeros((bsz, HG_HEADS, HG_V_DIM, HG_EXPAND), F32)
    o_c, st = hgrn2_scan(pc, lb3, (init, init), name="hg_scan_ctx")
    o_l, _ = hgrn2_scan(pl_, lb3, st, name="hg_scan_lat")

    def gated(p, o, colmajor, tag):
        g = hg_out(o[0], o[1], p, norm_g, name="hg_gate_" + tag)
        return flat(grid_from_colmajor(g) if colmajor else g)

    g_c = gated(pc, o_c, False, "ctx") if need_ctx else None
    y_l, y_c = dense2(gated(pl_, o_l, lat_colmajor, "lat"), g_c, w_out, layer, out_dtype=BF16, name="hg_out")
    return (y_c.reshape(a_c.shape) if need_ctx else None), y_l.reshape(a_l.shape)


def grid_to_colmajor(h):
    bsz, t, d = h.shape
    rows = t // GRID_W
    return h.reshape(bsz, rows, GRID_W, d).transpose(0, 2, 1, 3).reshape(bsz, t, d)


def grid_from_colmajor(h):
    bsz, t, d = h.shape
    rows = t // GRID_W
    return h.reshape(bsz, GRID_W, rows, d).transpose(0, 2, 1, 3).reshape(bsz, t, d)


def ffn(a_l, a_c, wg, wu, wd, layer):
    flat = lambda a: a.reshape(a.shape[0] * a.shape[1], a.shape[2])
    hid_l = matmul_stream(flat(a_l), (wg, wu), layer, bm=2048, bn=256, out_dtype=BF16, name="ffn_up_lat")
    f_l = dense(hid_l, wd, layer, out_dtype=BF16, name="ffn_down_lat", bm=512, bn=512).reshape(a_l.shape)
    if a_c is None:
        return f_l, None
    hid_c = swiglu_single(flat(a_c), wg, wu, layer, bn=256, name="ffn_up_ctx")
    f_c = dense(hid_c, wd, layer, out_dtype=BF16, name="ffn_down_ctx", bm=512, bn=512).reshape(a_c.shape)
    return f_l, f_c


def kernel(x, c, ctx, c_ctx, ada_w, ada_b, norm_g, ffn_w_gate, ffn_w_up, ffn_w_down,
           ab_w_in, ab_w_out, ml_gate_b, ml_norm_g, ssm_conv_w, ssm_conv_b, ssm_dt_bias,
           ssm_a_log, ssm_d, ssm_norm_g, hg_w_in, hg_w_out, hg_lb, hg_norm_g):
    depth = ada_w.shape[0]
    bsz = x.shape[0]
    d = D_MODEL
    lb_p = jax.nn.softmax(hg_lb.astype(F32), axis=1)
    lower_bounds = jnp.cumsum(lb_p, axis=1) - lb_p[:, :1]
    cond = jnp.concatenate([jax.nn.silu(c), jax.nn.silu(c_ctx)[None, :],
                            jnp.zeros((8 - bsz - 1, d), F32)], axis=0)
    h_lat, h_ctx = x, ctx
    y_l = y_c = f_l = f_c = None
    gate_l = gate_c = g_prev = None
    for layer in range(depth):
        need_ctx = layer < depth - 1
        j = layer // 2
        mod = matmul(cond, ada_w, bm=8, bn=512, out_dtype=F32, bias=ada_b[layer], layer=layer,
                     name="ada_mod")
        mod_l = [mod[:bsz, i * d:(i + 1) * d][:, None, :] for i in range(6)]
        mod_c = [jnp.broadcast_to(mod[bsz, i * d:(i + 1) * d][None, None, :], (bsz, 1, d)) for i in range(6)]
        g = norm_g[layer]
        h_lat, a_l = resid_norm(h_lat, f_l, gate_l, g_prev, g[0], mod_l[0], mod_l[1])
        h_ctx, a_c = resid_norm(h_ctx, f_c, gate_c, g_prev, g[0], mod_c[0], mod_c[1])
        if layer % 2 == 0:
            zx0 = E_A + E_GATES
            w_zx = ab_w_in[j, :, zx0:zx0 + E_B].astype(BF16)
            w_small = jnp.concatenate(
                [ab_w_in[j, :, E_A:zx0], ab_w_in[j, :, zx0 + E_B:],
                 jnp.zeros((d, 128 - E_GATES - 2 * SSM_HEADS), F32)], axis=1).astype(BF16)
            y_c, y_l = even_mixer(a_c, a_l, ab_w_in, w_zx, w_small, ab_w_out, j, ml_gate_b[j],
                                  ml_norm_g[j], ssm_conv_w[j], ssm_conv_b[j], ssm_dt_bias[j],
                                  ssm_a_log[j], ssm_d[j], ssm_norm_g[j], need_ctx)
        else:
            y_c, y_l = hgrn2_mixer(a_c, grid_to_colmajor(a_l), hg_w_in, hg_w_out, j,
                                   lower_bounds[:, layer], hg_norm_g[j], need_ctx)
        h_lat, a2_l = resid_norm(h_lat, y_l, mod_l[2], g[1], g[2], mod_l[3], mod_l[4])
        a2_c = None
        if need_ctx:
            h_ctx, a2_c = resid_norm(h_ctx, y_c, mod_c[2], g[1], g[2], mod_c[3], mod_c[4])
        f_l, f_c = ffn(a2_l, a2_c, ffn_w_gate, ffn_w_up, ffn_w_down, layer)
        gate_l = mod_l[5]
        gate_c = mod_c[5] if need_ctx else None
        g_prev = g[3]
    h_lat, _ = resid_norm(h_lat, f_l, gate_l, g_prev)
    return h_lat
```

```python
import functools
import math

import jax
import jax.numpy as jnp
import numpy as np
from jax import lax
from jax.experimental import pallas as pl
from jax.experimental.pallas import tpu as pltpu

F32 = jnp.float32
BF16 = jnp.bfloat16

D_MODEL = 4096
GRID_W = 64
EPS = 1e-6

ML_HEADS = 4
ML_QK_DIM = 256
ML_V_DIM = 512
ML_QK = ML_HEADS * ML_QK_DIM
ML_V = ML_HEADS * ML_V_DIM
ML_CHUNK = 128
GATE_CAP = 15.0

SSM_HEAD_DIM = 64
SSM_INNER = 2048
SSM_HEADS = 32
SSM_GROUPS = 8
SSM_HPG = 4
SSM_STATE = 128
SSM_CONV = 5
SSM_CHUNK = 128
SSM_GROUP_W = SSM_HPG * SSM_HEAD_DIM
SSM_CONV_DIM = SSM_INNER + 2 * SSM_GROUPS * SSM_STATE
SSD_GROUPS_PER_STEP = 4

HG_EXPAND = 128
HG_HEADS = 32
HG_F = HG_HEADS * HG_EXPAND
HG_V_DIM = 128
HG_CHUNK = 64
HG_LEVELS = 6
HG_HEADS_PER_STEP = 8

FFN_HIDDEN = 11008

E_Q, E_K, E_V, E_O, E_Z, E_XBC = 0, 1024, 2048, 4096, 6144, 8192
E_A = 6144
E_GATES = 16
E_B = SSM_INNER + SSM_CONV_DIM
E_MAIN = E_A + E_B
O_Q, O_F, O_I, O_G = 0, 4096, 12288, 16384

VMEM_LIMIT = 56 * 1024 * 1024


def _cparams(sem):
    return pltpu.CompilerParams(dimension_semantics=sem, vmem_limit_bytes=VMEM_LIMIT)


def _mm_kernel(*refs, has_bias, cast_w):
    x_ref, w_ref = refs[0], refs[1]
    b_ref = refs[2] if has_bias else None
    o_ref = refs[2 + has_bias]
    if cast_w:
        wb_ref = refs[3 + has_bias]

        @pl.when(pl.program_id(1) == 0)
        def _():
            wb_ref[...] = w_ref[...].astype(BF16)

        w = wb_ref[...]
    else:
        w = w_ref[...]
    acc = jnp.dot(x_ref[...].astype(BF16), w, preferred_element_type=F32)
    if has_bias:
        acc = acc + b_ref[...]
    o_ref[...] = acc.astype(o_ref.dtype)


def matmul(x, w, *, bm, bn, out_dtype, name, bias=None, layer=None, n=None):
    m, k = x.shape
    n = w.shape[-1] if n is None else n
    assert m % bm == 0 and n % bn == 0, (m, bm, n, bn)
    if layer is None:
        w_spec = pl.BlockSpec((k, bn), lambda j, i: (0, j))
    else:
        w_spec = pl.BlockSpec((None, k, bn), lambda j, i: (layer, 0, j))
    in_specs = [pl.BlockSpec((bm, k), lambda j, i: (i, 0)), w_spec]
    args = [x, w]
    if bias is not None:
        in_specs.append(pl.BlockSpec((1, bn), lambda j, i: (0, j)))
        args.append(bias.reshape(1, n))
    cast_w = w.dtype != BF16
    return pl.pallas_call(
        functools.partial(_mm_kernel, has_bias=bias is not None, cast_w=cast_w),
        grid=(n // bn, m // bm),
        in_specs=in_specs,
        out_specs=pl.BlockSpec((bm, bn), lambda j, i: (i, j)),
        out_shape=jax.ShapeDtypeStruct((m, n), out_dtype),
        scratch_shapes=[pltpu.VMEM((k, bn), BF16)] if cast_w else [],
        compiler_params=_cparams(("parallel", "arbitrary" if cast_w else "parallel")),
        name=name,
    )(*args)


def _swiglu(g, u):
    return g * jax.nn.sigmoid(g) * u


def _mm_stream_kernel(*refs, n_w, n_x, nj, mi, kc):
    x_refs, w_refs = refs[:n_x], refs[n_x:n_x + n_w]
    o_refs, wb_refs = refs[n_x + n_w:2 * n_x + n_w], refs[2 * n_x + n_w:]
    j, i = pl.program_id(0), pl.program_id(1)
    slot = j % 2

    @pl.when((j < nj) & (i < mi))
    def _():
        r0 = pl.multiple_of(i * kc, 16)
        for w_ref, wb_ref in zip(w_refs, wb_refs):
            wb_ref[slot, pl.ds(r0, kc), :] = w_ref[...].astype(BF16)

    def product(x_ref, o_ref):
        x = x_ref[...]
        accs = [jnp.dot(x, wb_ref[1 - slot], preferred_element_type=F32) for wb_ref in wb_refs]
        out = _swiglu(*accs) if n_w == 2 else accs[0]
        o_ref[...] = out.astype(o_ref.dtype)

    @pl.when((j > 0) & (i < mi))
    def _():
        product(x_refs[0], o_refs[0])

    if n_x == 2:
        @pl.when((j > 0) & (i == mi))
        def _():
            product(x_refs[1], o_refs[1])


def matmul_stream(x, ws, layer, *, bm, bn, out_dtype, name, x2=None, n=None):
    m, k = x.shape
    n = ws[0].shape[-1] if n is None else n
    mi, nj = m // bm, n // bn
    kc = k // mi
    assert m % bm == 0 and n % bn == 0 and k % mi == 0 and kc % 16 == 0 and mi > 1, (m, bm, n, bn, k)
    n_x = 1 if x2 is None else 2
    row = lambda j, i: jnp.where(j == 0, 0, jnp.minimum(i, mi - 1))
    col = lambda j: jnp.maximum(j - 1, 0)
    w_spec = pl.BlockSpec(
        (None, kc, bn),
        lambda j, i: (layer, jnp.where(j < nj, jnp.minimum(i, mi - 1), mi - 1), jnp.minimum(j, nj - 1)))
    in_specs = [pl.BlockSpec((bm, k), lambda j, i: (row(j, i), 0))]
    out_specs = [pl.BlockSpec((bm, bn), lambda j, i: (row(j, i), col(j)))]
    out_shape = [jax.ShapeDtypeStruct((m, n), out_dtype)]
    args = [x]
    if x2 is not None:
        m2 = x2.shape[0]
        assert m2 <= bm and x2.shape[1] == k
        in_specs.append(pl.BlockSpec((m2, k), lambda j, i: (0, 0)))
        out_specs.append(pl.BlockSpec((m2, bn), lambda j, i: (0, col(j))))
        out_shape.append(jax.ShapeDtypeStruct((m2, n), out_dtype))
        args.append(x2)
    outs = pl.pallas_call(
        functools.partial(_mm_stream_kernel, n_w=len(ws), n_x=n_x, nj=nj, mi=mi, kc=kc),
        grid=(nj + 1, mi + n_x - 1),
        in_specs=in_specs + [w_spec] * len(ws),
        out_specs=out_specs,
        out_shape=out_shape,
        scratch_shapes=[pltpu.VMEM((2, k, bn), BF16) for _ in ws],
        compiler_params=_cparams(("arbitrary", "arbitrary")),
        name=name,
    )(*args, *ws)
    return outs[0] if x2 is None else (outs[0], outs[1])


def _swiglu_kernel(x_ref, wg_ref, wu_ref, o_ref):
    x = x_ref[...]
    g = jnp.dot(x, wg_ref[...].astype(BF16), preferred_element_type=F32)
    u = jnp.dot(x, wu_ref[...].astype(BF16), preferred_element_type=F32)
    o_ref[...] = _swiglu(g, u).astype(o_ref.dtype)


def swiglu_single(x, wg, wu, layer, *, bn, name):
    m, k = x.shape
    n = wg.shape[-1]
    assert n % bn == 0
    w_spec = pl.BlockSpec((None, k, bn), lambda j: (layer, 0, j))
    return pl.pallas_call(
        _swiglu_kernel,
        grid=(n // bn,),
        in_specs=[pl.BlockSpec((m, k), lambda j: (0, 0)), w_spec, w_spec],
        out_specs=pl.BlockSpec((m, bn), lambda j: (0, j)),
        out_shape=jax.ShapeDtypeStruct((m, n), BF16),
        compiler_params=_cparams(("parallel",)),
        name=name,
    )(x, wg, wu)


def dense2(x, x2, w, layer, *, out_dtype, name, bm=1024, bn=1024, n=None):
    if x.shape[0] <= bm:
        ncol = w.shape[-1] if n is None else n
        one = lambda a, nm: matmul(a, w, bm=a.shape[0], bn=256, out_dtype=out_dtype, layer=layer, name=nm,
                                   n=ncol)
        return one(x, name), (None if x2 is None else one(x2, name + "_ctx"))
    y = matmul_stream(x, (w,), layer, bm=bm, bn=bn, out_dtype=out_dtype, name=name, x2=x2, n=n)
    return (y, None) if x2 is None else y


def dense(x, w, layer, *, out_dtype, name, bm=1024, bn=1024):
    m = x.shape[0]
    if m > bm:
        return matmul_stream(x, (w,), layer, bm=bm, bn=bn, out_dtype=out_dtype, name=name)
    return matmul(x, w, bm=m, bn=256, out_dtype=out_dtype, layer=layer, name=name)


def _rms(x):
    return x * lax.rsqrt(jnp.mean(x * x, axis=-1, keepdims=True) + EPS)


def _resid_norm_kernel(*refs, has_y, want_a):
    it = iter(refs)
    h_ref = next(it)
    if has_y:
        y_ref, gate_ref, g1_ref = next(it), next(it), next(it)
    if want_a:
        g2_ref, shift_ref, scale_ref = next(it), next(it), next(it)
    if has_y:
        hout_ref = next(it)
    if want_a:
        a_ref = next(it)
    h = h_ref[0]
    if has_y:
        h = h + gate_ref[0] * (_rms(y_ref[0].astype(F32)) * g1_ref[...])
        hout_ref[0] = h
    if want_a:
        a = (_rms(h) * g2_ref[...]) * (1.0 + scale_ref[0]) + shift_ref[0]
        a_ref[0] = a.astype(a_ref.dtype)


def resid_norm(h, y=None, gate=None, g1=None, g2=None, shift=None, scale=None, *, bt=128):
    b, t, d = h.shape
    has_y, want_a = y is not None, g2 is not None
    bt = min(bt, t)
    tok = pl.BlockSpec((1, bt, d), lambda i, j: (i, j, 0))
    per_b = pl.BlockSpec((1, 1, d), lambda i, j: (i, 0, 0))
    vec = pl.BlockSpec((1, d), lambda i, j: (0, 0))
    args, in_specs, out_shape, out_specs = [h], [tok], [], []
    if has_y:
        args += [y, gate, g1.reshape(1, d)]
        in_specs += [tok, per_b, vec]
        out_shape.append(jax.ShapeDtypeStruct((b, t, d), F32))
        out_specs.append(tok)
    if want_a:
        args += [g2.reshape(1, d), shift, scale]
        in_specs += [vec, per_b, per_b]
        out_shape.append(jax.ShapeDtypeStruct((b, t, d), BF16))
        out_specs.append(tok)
    outs = pl.pallas_call(
        functools.partial(_resid_norm_kernel, has_y=has_y, want_a=want_a),
        grid=(b, t // bt),
        in_specs=in_specs,
        out_specs=out_specs,
        out_shape=out_shape,
        compiler_params=_cparams(("parallel", "parallel")),
        name="resid_norm",
    )(*args)
    outs = list(outs)
    h_new = outs.pop(0) if has_y else h
    a = outs.pop(0) if want_a else None
    return h_new, a


def _softplus(x):
    return jnp.maximum(x, 0.0) + jnp.log1p(jnp.exp(-jnp.abs(x)))


def _log_sigmoid(x):
    return jnp.minimum(x, 0.0) - jnp.log1p(jnp.exp(-jnp.abs(x)))


def _chunk_index(c, nc, reverse):
    return (nc - 1 - c) if reverse else c


def _mlstm_kernel(*refs):
    ins, outs = refs[:16], refs[16:]

    @pl.when(pl.program_id(2) == 0)
    def _():
        for d in range(2):
            for s in range(3):
                outs[4 * d + 1 + s][...] = ins[8 * d + 5 + s][...]

    for d in range(2):
        _mlstm_chunk(*ins[8 * d:8 * d + 5], *outs[4 * d:4 * d + 4], reverse=d == 1)


def _mlstm_chunk(q_ref, k_ref, v_ref, gc_ref, gr_ref, y_ref, c_ref, n_ref, m_ref, *, reverse):
    l = ML_CHUNK
    row = lax.broadcasted_iota(jnp.int32, (l, l), 0)
    col = lax.broadcasted_iota(jnp.int32, (l, l), 1)
    mask = (col >= row) if reverse else (col <= row)
    mask_t = (col <= row) if reverse else (col >= row)

    gcol = gc_ref[0, 0, 0]
    grow = gr_ref[0, 0, 0]
    ig_c = GATE_CAP * jnp.tanh(gcol[:, 0:1] / GATE_CAP)
    lf_c = _log_sigmoid(GATE_CAP * jnp.tanh(gcol[:, 1:2] / GATE_CAP))
    ig_r = GATE_CAP * jnp.tanh(grow[0:1, :] / GATE_CAP)
    lf_r = _log_sigmoid(GATE_CAP * jnp.tanh(grow[1:2, :] / GATE_CAP))

    b_c = jnp.sum(jnp.where(mask, lf_r, 0.0), axis=1, keepdims=True)
    b_r = jnp.sum(jnp.where(mask_t, lf_c, 0.0), axis=0, keepdims=True)
    total = jnp.sum(lf_c, axis=0, keepdims=True)

    m_prev = m_ref[0, 0]
    logw = jnp.where(mask, b_c - b_r + ig_r, -jnp.inf)
    inter = b_c + m_prev
    m_t = jnp.maximum(jnp.max(logw, axis=1, keepdims=True), inter)

    q = q_ref[0]
    k = k_ref[0]
    v = v_ref[0]
    scale = ML_QK_DIM ** -0.5
    qk = lax.dot_general(q, k, (((1,), (1,)), ((), ())), preferred_element_type=F32)
    s = qk * scale * jnp.exp(logw - m_t)
    inter_w = jnp.exp(inter - m_t) * scale

    c_st = c_ref[0, 0]
    n_st = n_ref[0, 0]
    num = (jnp.dot(s.astype(BF16), v, preferred_element_type=F32)
           + inter_w * jnp.dot(q, c_st.astype(BF16), preferred_element_type=F32))
    qn = jnp.sum(q.astype(F32) * n_st, axis=1, keepdims=True)
    den = jnp.sum(s, axis=1, keepdims=True) + inter_w * qn
    y_ref[0] = (num * (1.0 / jnp.maximum(jnp.abs(den), jnp.exp(-m_t)))).astype(y_ref.dtype)

    g_c = total - b_c + ig_c
    g_r = total - b_r + ig_r
    m_new = jnp.maximum(total + m_prev, jnp.max(g_r, axis=1, keepdims=True))
    w_c = jnp.exp(g_c - m_new)
    decay = jnp.exp(total + m_prev - m_new)
    kw = k.astype(F32) * w_c
    kv = lax.dot_general(kw.astype(BF16), v, (((0,), (0,)), ((), ())), preferred_element_type=F32)
    c_ref[0, 0] = decay * c_st + kv
    n_ref[0, 0] = decay * n_st + jnp.sum(kw, axis=0, keepdims=True)
    m_ref[0, 0] = m_new


def mlstm_scan(proj, gates_col, gates_row, states, *, name):
    b, t, _ = proj.shape
    l = ML_CHUNK
    nc = t // l
    qb, kb, vb = E_Q // ML_QK_DIM, E_K // ML_QK_DIM, E_V // ML_V_DIM
    st_specs = [pl.BlockSpec((1, 1, ML_QK_DIM, ML_V_DIM), lambda i, h, c: (i, h, 0, 0)),
                pl.BlockSpec((1, 1, 1, ML_QK_DIM), lambda i, h, c: (i, h, 0, 0)),
                pl.BlockSpec((1, 1, 1, 1), lambda i, h, c: (i, h, 0, 0))]
    in_specs, args, out_specs, out_shape = [], [], [], []
    for d in range(2):
        ci = functools.partial(_chunk_index, nc=nc, reverse=d == 1)
        in_specs += [
            pl.BlockSpec((1, l, ML_QK_DIM), lambda i, h, c, ci=ci: (i, ci(c), qb + h)),
            pl.BlockSpec((1, l, ML_QK_DIM), lambda i, h, c, ci=ci: (i, ci(c), kb + h)),
            pl.BlockSpec((1, l, ML_V_DIM), lambda i, h, c, ci=ci: (i, ci(c), vb + h)),
            pl.BlockSpec((1, 1, 1, l, 2), lambda i, h, c, ci=ci, d=d: (d, i, h, ci(c), 0)),
            pl.BlockSpec((1, 1, 1, 2, l), lambda i, h, c, ci=ci, d=d: (d, i, h, 0, ci(c))),
        ] + st_specs
        args += [proj, proj, proj, gates_col, gates_row, *states[d]]
        out_specs += [pl.BlockSpec((1, l, ML_V_DIM), lambda i, h, c, ci=ci: (i, ci(c), h))] + st_specs
        out_shape += [jax.ShapeDtypeStruct((b, t, ML_V), BF16)]
        out_shape += [jax.ShapeDtypeStruct(s.shape, F32) for s in states[d]]
    outs = pl.pallas_call(
        _mlstm_kernel,
        grid=(b, ML_HEADS, nc),
        in_specs=in_specs,
        out_specs=out_specs,
        out_shape=out_shape,
        compiler_params=_cparams(("parallel", "parallel", "arbitrary")),
        name=name,
    )(*args)
    return (outs[0], outs[4]), (tuple(outs[1:4]), tuple(outs[5:8]))


CONV_ROWS = 256
CONV_HALO = 16


def _conv_kernel(x_ref, w_ref, b_ref, o_ref, *, t):
    w = w_ref[...]
    bias = b_ref[...]
    r = min(CONV_ROWS, t)
    half = SSM_CONV // 2
    cw = x_ref.shape[-1]
    for r0 in range(0, t, r):
        parts = []
        if r0 == 0:
            parts.append(jnp.zeros((CONV_HALO, cw), F32))
        else:
            parts.append(x_ref[0, pl.ds(r0 - CONV_HALO, CONV_HALO), :].astype(F32))
        parts.append(x_ref[0, pl.ds(r0, r), :].astype(F32))
        if r0 + r == t:
            parts.append(jnp.zeros((CONV_HALO, cw), F32))
        else:
            parts.append(x_ref[0, pl.ds(r0 + r, CONV_HALO), :].astype(F32))
        win = jnp.concatenate(parts, axis=0)
        n = r + 2 * CONV_HALO
        acc = jnp.zeros((r, cw), F32) + bias
        for kk in range(SSM_CONV):
            sh = (half - kk) % n
            rolled = win if sh == 0 else pltpu.roll(win, sh, 0)
            acc = acc + w[kk:kk + 1, :] * rolled[CONV_HALO:CONV_HALO + r, :]
        o_ref[0, pl.ds(r0, r), :] = (acc * jax.nn.sigmoid(acc)).astype(o_ref.dtype)


def conv_silu(proj, conv_w, conv_b, *, bc=512):
    b, t, _ = proj.shape
    c = SSM_CONV_DIM
    off = E_XBC // bc
    return pl.pallas_call(
        functools.partial(_conv_kernel, t=t),
        grid=(b, c // bc),
        in_specs=[pl.BlockSpec((1, t, bc), lambda i, j: (i, 0, off + j)),
                  pl.BlockSpec((SSM_CONV, bc), lambda i, j: (0, j)),
                  pl.BlockSpec((1, bc), lambda i, j: (0, j))],
        out_specs=pl.BlockSpec((1, t, bc), lambda i, j: (i, 0, j)),
        out_shape=jax.ShapeDtypeStruct((b, t, c), BF16),
        compiler_params=_cparams(("parallel", "parallel")),
        name="conv_silu",
    )(proj, conv_w, conv_b.reshape(1, c))


def _split3(x):
    hi = x.astype(BF16)
    r1 = x - hi.astype(F32)
    mid = r1.astype(BF16)
    lo = (r1 - mid.astype(F32)).astype(BF16)
    return hi, mid, lo


def _dot3(dims, parts_lhs, rhs):
    return sum(lax.dot_general(p, rhs, dims, preferred_element_type=F32) for p in parts_lhs)


def _head_selector(heads, width):
    rows = lax.broadcasted_iota(jnp.int32, (heads, heads * width), 0)
    lanes = lax.broadcasted_iota(jnp.int32, (heads, heads * width), 1)
    return (lanes // width == rows).astype(BF16)


def _ssd_kernel(*refs):
    ins, outs = refs[:12], refs[12:]

    @pl.when(pl.program_id(2) == 0)
    def _():
        for d in range(2):
            outs[2 * d + 1][...] = ins[6 * d + 5][...]

    for d in range(2):
        _ssd_chunk(*ins[6 * d:6 * d + 5], *outs[2 * d:2 * d + 2], reverse=d == 1)


def _ssd_chunk(xs_ref, bm_ref, cm_ref, dtr_ref, pr_ref, y_ref, h_ref, *, reverse):
    l = SSM_CHUNK
    gp = SSD_GROUPS_PER_STEP
    nh = gp * SSM_HPG
    row = lax.broadcasted_iota(jnp.int32, (l, l), 0)
    col = lax.broadcasted_iota(jnp.int32, (l, l), 1)
    mask = (col >= row) if reverse else (col <= row)
    mask_t = (col <= row) if reverse else (col >= row)

    nn = (((1,), (0,)), ((), ()))
    nt = (((1,), (1,)), ((), ()))
    tn = (((0,), (0,)), ((), ()))
    pr = pr_ref[0, 0]
    dt_r = _softplus(dtr_ref[0, 0, 0] + pr[:, 0:1])
    a_r = dt_r * pr[:, 1:2]
    cum_r = _dot3(nn, _split3(a_r), mask_t.astype(BF16))
    dt_parts, cum_parts = _split3(dt_r), _split3(cum_r)
    sel_ch = _head_selector(nh, SSM_HEAD_DIM)
    dt_ch = _dot3(tn, dt_parts, sel_ch)
    cum_ch = _dot3(tn, cum_parts, sel_ch)
    cum_bc = _dot3(tn, cum_parts, _head_selector(nh, l))
    end_row = 0 if reverse else l - 1
    total_ch = cum_ch[end_row:end_row + 1, :]

    xdt = xs_ref[0].astype(F32) * dt_ch
    e_cum = jnp.exp(cum_ch)
    xw = (xdt * jnp.exp(total_ch - cum_ch)).astype(BF16)
    s_decay = jnp.exp(total_ch)
    lane_head = lax.broadcasted_iota(jnp.int32, (l, SSM_GROUP_W), 1) // SSM_HEAD_DIM
    for g in range(gp):
        xsl = slice(g * SSM_GROUP_W, (g + 1) * SSM_GROUP_W)
        nsl = slice(g * SSM_STATE, (g + 1) * SSM_STATE)
        bm = bm_ref[0, :, nsl]
        cm = cm_ref[0, :, nsl]
        hst = h_ref[0, g]
        cb = lax.dot_general(cm, bm, nt, preferred_element_type=F32)
        inter = jnp.dot(cm, hst.astype(BF16), preferred_element_type=F32)
        xdt_g = xdt[:, xsl]
        ws, xjs = [], []
        for j in range(SSM_HPG):
            hd = g * SSM_HPG + j
            decay = jnp.exp(jnp.where(mask, cum_bc[:, hd * l:(hd + 1) * l] - cum_r[hd:hd + 1, :], -jnp.inf))
            ws.append((decay * cb).astype(BF16))
            xjs.append(jnp.where(lane_head == j, xdt_g, 0.0).astype(BF16))
        y = jnp.dot(jnp.concatenate(ws, axis=1), jnp.concatenate(xjs, axis=0), preferred_element_type=F32)
        y_ref[0, :, xsl] = (y + e_cum[:, xsl] * inter).astype(y_ref.dtype)
        upd = lax.dot_general(bm, xw[:, xsl], tn, preferred_element_type=F32)
        h_ref[0, g] = s_decay[:, xsl] * hst + upd


def ssd_scan(xbc, dt_row, par_row, states, *, name):
    b, t, _ = xbc.shape
    l = SSM_CHUNK
    nc = t // l
    gp = SSD_GROUPS_PER_STEP
    xw, nw = gp * SSM_GROUP_W, gp * SSM_STATE
    bmb = SSM_INNER // nw
    cmb = bmb + SSM_GROUPS // gp
    st_spec = pl.BlockSpec((1, gp, SSM_STATE, SSM_GROUP_W), lambda i, g, c: (i, g, 0, 0))
    dt_step = dt_row.reshape(2, b, SSM_GROUPS // gp, gp * SSM_HPG, t)
    par_step = par_row.reshape(2, SSM_GROUPS // gp, gp * SSM_HPG, 2)
    in_specs, args, out_specs, out_shape = [], [], [], []
    for d in range(2):
        ci = functools.partial(_chunk_index, nc=nc, reverse=d == 1)
        in_specs += [
            pl.BlockSpec((1, l, xw), lambda i, g, c, ci=ci: (i, ci(c), g)),
            pl.BlockSpec((1, l, nw), lambda i, g, c, ci=ci: (i, ci(c), bmb + g)),
            pl.BlockSpec((1, l, nw), lambda i, g, c, ci=ci: (i, ci(c), cmb + g)),
            pl.BlockSpec((1, 1, 1, gp * SSM_HPG, l), lambda i, g, c, ci=ci, d=d: (d, i, g, 0, ci(c))),
            pl.BlockSpec((1, 1, gp * SSM_HPG, 2), lambda i, g, c, d=d: (d, g, 0, 0)),
            st_spec,
        ]
        args += [xbc, xbc, xbc, dt_step, par_step, states[d]]
        out_specs += [pl.BlockSpec((1, l, xw), lambda i, g, c, ci=ci: (i, ci(c), g)), st_spec]
        out_shape += [jax.ShapeDtypeStruct((b, t, SSM_INNER), BF16),
                      jax.ShapeDtypeStruct(states[d].shape, F32)]
    yf, hf, yb, hb = pl.pallas_call(
        _ssd_kernel,
        grid=(b, SSM_GROUPS // gp, nc),
        in_specs=in_specs,
        out_specs=out_specs,
        out_shape=out_shape,
        compiler_params=_cparams(("parallel", "parallel", "arbitrary")),
        name=name,
    )(*args)
    return (yf, yb), (hf, hb)


def _even_out_kernel(mlf_ref, mlb_ref, ssdf_ref, ssdb_ref, o_ref, z_ref, xs_ref, mlg_ref, dsk_ref, ssg_ref,
                     out_ref):
    for h in range(ML_HEADS):
        sl = slice(h * ML_V_DIM, (h + 1) * ML_V_DIM)
        y = mlf_ref[0, :, sl].astype(F32) + mlb_ref[0, :, sl].astype(F32)
        og = o_ref[0, :, sl].astype(F32)
        out_ref[0, :, sl] = (_rms(y) * mlg_ref[:, sl] * jax.nn.sigmoid(og)).astype(out_ref.dtype)
    for g in range(SSM_GROUPS):
        sl = slice(g * SSM_GROUP_W, (g + 1) * SSM_GROUP_W)
        z = z_ref[0, :, sl].astype(F32)
        y = ((ssdf_ref[0, :, sl].astype(F32) + ssdb_ref[0, :, sl].astype(F32))
             + dsk_ref[:, sl] * xs_ref[0, :, sl].astype(F32))
        y = y * (z * jax.nn.sigmoid(z))
        so = slice(ML_V + g * SSM_GROUP_W, ML_V + (g + 1) * SSM_GROUP_W)
        out_ref[0, :, so] = (_rms(y) * ssg_ref[:, sl]).astype(out_ref.dtype)


def even_out(ml_y, ssd_y, proj_a, proj_b, xbc, ml_norm_g, d_skip_full, ssm_norm_g, *, bt=128):
    b, t, _ = ml_y[0].shape
    bt = min(bt, t)
    w = ML_V
    tok = lambda blk: pl.BlockSpec((1, bt, w), lambda i, j: (i, j, blk))
    vec = pl.BlockSpec((1, w), lambda i, j: (0, 0))
    return pl.pallas_call(
        _even_out_kernel,
        grid=(b, t // bt),
        in_specs=[tok(0), tok(0), tok(0), tok(0), tok(E_O // w), tok(E_Z // w), tok(0), vec, vec, vec],
        out_specs=pl.BlockSpec((1, bt, 2 * w), lambda i, j: (i, j, 0)),
        out_shape=jax.ShapeDtypeStruct((b, t, 2 * w), BF16),
        compiler_params=_cparams(("parallel", "parallel")),
        name="even_gate",
    )(ml_y[0], ml_y[1], ssd_y[0], ssd_y[1], proj_a, proj_b, xbc, ml_norm_g.reshape(1, w), d_skip_full.reshape(1, w),
      ssm_norm_g.reshape(1, w))


def _hg_constants(reverse):
    l = HG_CHUNK
    pos = np.arange(l)
    tri = (pos[None, :] <= pos[:, None]).astype(np.float32)
    masks = []
    for lv in range(HG_LEVELS):
        m = 1 << lv
        blk = pos // (2 * m)
        later = (pos % (2 * m)) >= m
        masks.append(((blk[:, None] == blk[None, :]) & later[:, None] & (~later)[None, :]).astype(np.float32))
    masks = np.stack(masks)
    if reverse:
        tri = tri[::-1, ::-1]
        masks = masks[:, ::-1, ::-1]
    return (jnp.asarray(tri, BF16), jnp.asarray(masks.reshape(-1, l), F32))


def _hg_level_factors(q, k, f, cum, reverse):
    l, w = q.shape
    row = lax.broadcasted_iota(jnp.int32, (l, w), 0)
    sub = lax.broadcasted_iota(jnp.int32, (8, w), 0)
    zs = []
    for lv in range(HG_LEVELS):
        m = 1 << lv
        late = ((row & m) == 0) if reverse else ((row & m) != 0)
        if m >= 8:
            pieces = []
            for b0 in range(0, l, 2 * m):
                lo_rows, hi_rows = slice(b0, b0 + m), slice(b0 + m, b0 + 2 * m)
                if reverse:
                    mid = cum[b0 + m:b0 + m + 1]
                    pieces.append(q[lo_rows] * jnp.exp(cum[lo_rows] - mid))
                    pieces.append(k[hi_rows] * jnp.exp(mid - cum[hi_rows]))
                else:
                    mid = cum[b0 + m - 1:b0 + m]
                    pieces.append(k[lo_rows] * jnp.exp(mid - cum[lo_rows]))
                    pieces.append(q[hi_rows] * jnp.exp(cum[hi_rows] - mid))
            zs.append(jnp.concatenate(pieces, axis=0).astype(BF16))
            continue
        if m == 1:
            ex = jnp.where(late, f, 1.0)
        else:
            pieces = []
            if m >= 4:
                for b0 in range(0, l, 2 * m):
                    r = b0 + (m if reverse else m - 1)
                    pieces.append(cum[b0:b0 + 2 * m] - cum[r:r + 1])
            else:
                for b0 in range(0, l, 8):
                    ra, rb = (b0 + 2, b0 + 6) if reverse else (b0 + 1, b0 + 5)
                    mid = jnp.where(sub < 4, cum[ra:ra + 1], cum[rb:rb + 1])
                    pieces.append(cum[b0:b0 + 8] - mid)
            dd = pieces[0] if len(pieces) == 1 else jnp.concatenate(pieces, axis=0)
            ex = jnp.exp(jnp.where(late, dd, -dd))
        zs.append((jnp.where(late, q, k) * ex).astype(BF16))
    return zs


def _hg_chunk(q_ref, f_ref, i_ref, lb, tri, masks_ref, y_ref, s_ref, r0, reverse):
    l, e = HG_CHUNK, HG_EXPAND
    q_raw = q_ref[0, pl.ds(r0, l), :].astype(F32)
    f_raw = f_ref[0, pl.ds(r0, l), :].astype(F32)
    v = i_ref[0, pl.ds(r0, l), :]
    f = lb + (1.0 - lb) * jax.nn.sigmoid(f_raw)
    k = 1.0 - f
    lf = jnp.log(f)
    q = q_raw * jax.nn.sigmoid(q_raw) * (e ** -0.5)
    hi, mid, lo = _split3(lf)
    cum = (jnp.dot(tri, hi, preferred_element_type=F32)
           + jnp.dot(tri, mid, preferred_element_type=F32)
           + jnp.dot(tri, lo, preferred_element_type=F32))
    zs = _hg_level_factors(q, k, f, cum, reverse)
    end_row = 0 if reverse else l - 1
    cum_end = cum[end_row:end_row + 1, :]
    qc = (q * jnp.exp(cum)).astype(BF16)
    kt_end = (k * jnp.exp(cum_end - cum)).astype(BF16)
    s_decay = jnp.exp(cum_end)
    qk = q * k
    row = lax.broadcasted_iota(jnp.int32, (l, l), 0)
    col = lax.broadcasted_iota(jnp.int32, (l, l), 1)
    eye = row == col
    nt = (((1,), (1,)), ((), ()))
    tn = (((0,), (0,)), ((), ()))
    for hh in range(HG_HEADS_PER_STEP):
        sl = slice(hh * e, (hh + 1) * e)
        att = jnp.where(eye, jnp.sum(qk[:, sl], axis=1, keepdims=True), 0.0)
        for lv in range(HG_LEVELS):
            z = zs[lv][:, sl]
            a_lv = lax.dot_general(z, z, nt, preferred_element_type=F32)
            att = att + masks_ref[pl.ds(lv * l, l), :] * a_lv
        st = s_ref[0, hh]
        vh = v[:, sl]
        o = (jnp.dot(att.astype(BF16), vh, preferred_element_type=F32)
             + lax.dot_general(qc[:, sl], st.astype(BF16), nt, preferred_element_type=F32))
        y_ref[0, pl.ds(r0, l), sl] = o.astype(y_ref.dtype)
        upd = lax.dot_general(vh, kt_end[:, sl], tn, preferred_element_type=F32)
        s_ref[0, hh] = s_decay[:, sl] * st + upd


def _hg_kernel(qf_ref, ff_ref, if_ref, qb_ref, fb_ref, ib_ref, lbf_ref, lbb_ref, trif_ref, trib_ref,
               mf_ref, mb_ref, sf0_ref, sb0_ref, yf_ref, yb_ref, sf_ref, sb_ref, *, n_inner):
    l = HG_CHUNK

    @pl.when(pl.program_id(2) == 0)
    def _():
        sf_ref[...] = sf0_ref[...]
        sb_ref[...] = sb0_ref[...]

    lbf = lbf_ref[0]
    lbb = lbb_ref[0]
    trif = trif_ref[...]
    trib = trib_ref[...]

    def chunk(ci, carry):
        rf = pl.multiple_of(ci * l, l)
        rb = pl.multiple_of((n_inner - 1 - ci) * l, l)
        _hg_chunk(qf_ref, ff_ref, if_ref, lbf, trif, mf_ref, yf_ref, sf_ref, rf, False)
        _hg_chunk(qb_ref, fb_ref, ib_ref, lbb, trib, mb_ref, yb_ref, sb_ref, rb, True)
        return carry

    lax.fori_loop(0, n_inner, chunk, 0, unroll=4)


def hgrn2_scan(proj, lb, states, *, tb=512, name):
    b, t, _ = proj.shape
    tb = min(tb, t)
    nb = t // tb
    n_inner = tb // HG_CHUNK
    hw = HG_HEADS_PER_STEP * HG_EXPAND
    nh = HG_HEADS // HG_HEADS_PER_STEP
    qb_, ffb, fbb, ib_ = O_Q // hw, O_F // hw, (O_F + HG_F) // hw, O_I // hw
    trif, mf = _hg_constants(False)
    trib, mb = _hg_constants(True)
    fwd = lambda blk: pl.BlockSpec((1, tb, hw), lambda i, h, c: (i, c, blk + h))
    bwd = lambda blk: pl.BlockSpec((1, tb, hw), lambda i, h, c: (i, nb - 1 - c, blk + h))
    const = lambda a: pl.BlockSpec(a.shape, lambda i, h, c: (0,) * a.ndim)
    st_spec = pl.BlockSpec((1, HG_HEADS_PER_STEP, HG_V_DIM, HG_EXPAND), lambda i, h, c: (i, h, 0, 0))
    in_specs = [fwd(qb_), fwd(ffb), fwd(ib_), bwd(qb_), bwd(fbb), bwd(ib_),
                pl.BlockSpec((1, 1, hw), lambda i, h, c: (0, 0, h)),
                pl.BlockSpec((1, 1, hw), lambda i, h, c: (1, 0, h)),
                const(trif), const(trib), const(mf), const(mb), st_spec, st_spec]
    yf, yb, sf, sb = pl.pallas_call(
        functools.partial(_hg_kernel, n_inner=n_inner),
        grid=(b, nh, nb),
        in_specs=in_specs,
        out_specs=[fwd(0), bwd(0), st_spec, st_spec],
        out_shape=[jax.ShapeDtypeStruct((b, t, D_MODEL), BF16),
                   jax.ShapeDtypeStruct((b, t, D_MODEL), BF16),
                   jax.ShapeDtypeStruct(states[0].shape, F32),
                   jax.ShapeDtypeStruct(states[1].shape, F32)],
        compiler_params=_cparams(("parallel", "parallel", "arbitrary")),
        name=name,
    )(proj, proj, proj, proj, proj, proj, lb, lb, trif, trib, mf, mb, states[0], states[1])
    return (yf, yb), (sf, sb)


def _hg_out_kernel(of_ref, ob_ref, g_ref, ng_ref, out_ref):
    for h in range(of_ref.shape[-1] // HG_V_DIM):
        sl = slice(h * HG_V_DIM, (h + 1) * HG_V_DIM)
        g = g_ref[0, :, sl].astype(F32)
        o = of_ref[0, :, sl].astype(F32) + ob_ref[0, :, sl].astype(F32)
        out_ref[0, :, sl] = (_rms(o) * ng_ref[:, sl] * (g * jax.nn.sigmoid(g))).astype(out_ref.dtype)


def hg_out(o_f, o_b, proj, norm_g, *, bt=256, bc=1024, name):
    b, t, d = o_f.shape
    bt = min(bt, t)
    gb = O_G // bc
    tok = pl.BlockSpec((1, bt, bc), lambda i, j, c: (i, j, c))
    return pl.pallas_call(
        _hg_out_kernel,
        grid=(b, t // bt, d // bc),
        in_specs=[tok, tok,
                  pl.BlockSpec((1, bt, bc), lambda i, j, c: (i, j, gb + c)),
                  pl.BlockSpec((1, bc), lambda i, j, c: (0, c))],
        out_specs=tok,
        out_shape=jax.ShapeDtypeStruct((b, t, d), BF16),
        compiler_params=_cparams(("parallel", "parallel", "parallel")),
        name=name,
    )(o_f, o_b, proj, norm_g.reshape(1, d))


def even_mixer(a_c, a_l, w_main, w_small, w_out, layer, ml_gate_b, ml_norm_g, conv_w, conv_b,
               dt_bias, a_log, d_skip, ssm_norm_g, need_ctx):
    bsz = a_l.shape[0]

    def features(a, tag):
        b, t, d = a.shape
        bm = min(1024, b * t)
        a2 = a.reshape(b * t, d)
        proj = matmul(a2, w_main, bm=bm, bn=1024, out_dtype=BF16, name="ab_in_" + tag).reshape(b, t, E_MAIN)
        proj_a = proj
        small = matmul(a2, w_small, bm=bm, bn=128, out_dtype=F32, name="ab_in_small_" + tag).reshape(b, t, 128)
        gates = small[..., :16].reshape(b, t, 2, 2, ML_HEADS) + ml_gate_b.astype(F32)
        g_col = gates.transpose(2, 0, 4, 1, 3)
        g_row = gates.transpose(2, 0, 4, 3, 1)
        dt = small[..., 16:16 + 2 * SSM_HEADS].reshape(b, t, 2, SSM_GROUPS, SSM_HPG)
        dt_row = dt.transpose(2, 0, 3, 4, 1)
        xbc = conv_silu(proj, conv_w, conv_b)
        return dict(proj_a=proj_a, proj=proj, xbc=xbc, g_col=g_col, g_row=g_row, dt_row=dt_row)

    fc, fl = features(a_c, "ctx"), features(a_l, "lat")
    neg_a = -jnp.exp(a_log.astype(F32))
    par = jnp.stack([dt_bias.astype(F32), neg_a], axis=1).reshape(2, 2, SSM_GROUPS, SSM_HPG)
    par_row = par.transpose(0, 2, 3, 1)

    ml_init = (jnp.zeros((bsz, ML_HEADS, ML_QK_DIM, ML_V_DIM), F32),
               jnp.zeros((bsz, ML_HEADS, 1, ML_QK_DIM), F32),
               jnp.zeros((bsz, ML_HEADS, 1, 1), F32))
    ssd_init = jnp.zeros((bsz, SSM_GROUPS, SSM_STATE, SSM_GROUP_W), F32)

    def ml(f, st, tag):
        return mlstm_scan(f['proj_a'], f['g_col'], f['g_row'], st, name="mlstm_scan_" + tag)

    def ssd(f, st, tag):
        return ssd_scan(f['xbc'], f['dt_row'], par_row, st, name="ssd_scan_" + tag)

    ml_c, ml_st = ml(fc, (ml_init, ml_init), "ctx")
    ml_l, _ = ml(fl, ml_st, "lat")
    ssd_c, ssd_st = ssd(fc, (ssd_init, ssd_init), "ctx")
    ssd_l, _ = ssd(fl, ssd_st, "lat")
    d_full = jnp.repeat(d_skip.astype(F32), SSM_HEAD_DIM)

    def gated(f, ml_y, ssd_y):
        b, t, _ = ml_y[0].shape
        cat = even_out(ml_y, ssd_y, f['proj_a'], f['proj'], f['xbc'], ml_norm_g, d_full, ssm_norm_g)
        return cat.reshape(b * t, D_MODEL)

    cat_c = gated(fc, ml_c, ssd_c) if need_ctx else None
    y_l, y_c = dense2(gated(fl, ml_l, ssd_l), cat_c, w_out, layer, out_dtype=BF16, name="ab_out")
    y_l = y_l.reshape(a_l.shape)
    return (y_c.reshape(a_c.shape) if need_ctx else None), y_l


def hgrn2_mixer(a_c, a_l, w_in, w_out, layer, lb, norm_g, need_ctx, lat_colmajor=True):
    bsz = a_l.shape[0]
    lb3 = lb.astype(F32).reshape(2, 1, HG_F)

    flat = lambda a: a.reshape(a.shape[0] * a.shape[1], a.shape[2])
    pl_, pc = dense2(flat(a_l), flat(a_c), w_in, layer, out_dtype=BF16, name="hg_in")
    pl_ = pl_.reshape(a_l.shape[:2] + (5 * D_MODEL,))
    pc = pc.reshape(a_c.shape[:2] + (5 * D_MODEL,))
    init = jnp.zeros((bsz, HG_HEADS, HG_V_DIM, HG_EXPAND), F32)
    o_c, st = hgrn2_scan(pc, lb3, (init, init), name="hg_scan_ctx")
    o_l, _ = hgrn2_scan(pl_, lb3, st, name="hg_scan_lat")

    def gated(p, o, colmajor, tag):
        g = hg_out(o[0], o[1], p, norm_g, name="hg_gate_" + tag)
        return flat(grid_from_colmajor(g) if colmajor else g)

    g_c = gated(pc, o_c, False, "ctx") if need_ctx else None
    y_l, y_c = dense2(gated(pl_, o_l, lat_colmajor, "lat"), g_c, w_out, layer, out_dtype=BF16, name="hg_out")
    return (y_c.reshape(a_c.shape) if need_ctx else None), y_l.reshape(a_l.shape)


def grid_to_colmajor(h):
    bsz, t, d = h.shape
    rows = t // GRID_W
    return h.reshape(bsz, rows, GRID_W, d).transpose(0, 2, 1, 3).reshape(bsz, t, d)


def grid_from_colmajor(h):
    bsz, t, d = h.shape
    rows = t // GRID_W
    return h.reshape(bsz, GRID_W, rows, d).transpose(0, 2, 1, 3).reshape(bsz, t, d)


def ffn(a_l, a_c, wg, wu, wd, layer):
    flat = lambda a: a.reshape(a.shape[0] * a.shape[1], a.shape[2])
    hid_l = matmul_stream(flat(a_l), (wg, wu), layer, bm=2048, bn=256, out_dtype=BF16, name="ffn_up_lat")
    f_l = dense(hid_l, wd, layer, out_dtype=BF16, name="ffn_down_lat", bm=512, bn=512).reshape(a_l.shape)
    if a_c is None:
        return f_l, None
    hid_c = swiglu_single(flat(a_c), wg, wu, layer, bn=256, name="ffn_up_ctx")
    f_c = dense(hid_c, wd, layer, out_dtype=BF16, name="ffn_down_ctx", bm=512, bn=512).reshape(a_c.shape)
    return f_l, f_c


def kernel(x, c, ctx, c_ctx, ada_w, ada_b, norm_g, ffn_w_gate, ffn_w_up, ffn_w_down,
           ab_w_in, ab_w_out, ml_gate_b, ml_norm_g, ssm_conv_w, ssm_conv_b, ssm_dt_bias,
           ssm_a_log, ssm_d, ssm_norm_g, hg_w_in, hg_w_out, hg_lb, hg_norm_g):
    depth = ada_w.shape[0]
    bsz = x.shape[0]
    d = D_MODEL
    lb_p = jax.nn.softmax(hg_lb.astype(F32), axis=1)
    lower_bounds = jnp.cumsum(lb_p, axis=1) - lb_p[:, :1]
    cond = jnp.concatenate([jax.nn.silu(c), jax.nn.silu(c_ctx)[None, :],
                            jnp.zeros((8 - bsz - 1, d), F32)], axis=0)
    h_lat, h_ctx = x, ctx
    y_l = y_c = f_l = f_c = None
    gate_l = gate_c = g_prev = None
    for layer in range(depth):
        need_ctx = layer < depth - 1
        j = layer // 2
        mod = matmul(cond, ada_w, bm=8, bn=512, out_dtype=F32, bias=ada_b[layer], layer=layer,
                     name="ada_mod")
        mod_l = [mod[:bsz, i * d:(i + 1) * d][:, None, :] for i in range(6)]
        mod_c = [jnp.broadcast_to(mod[bsz, i * d:(i + 1) * d][None, None, :], (bsz, 1, d)) for i in range(6)]
        g = norm_g[layer]
        h_lat, a_l = resid_norm(h_lat, f_l, gate_l, g_prev, g[0], mod_l[0], mod_l[1])
        h_ctx, a_c = resid_norm(h_ctx, f_c, gate_c, g_prev, g[0], mod_c[0], mod_c[1])
        if layer % 2 == 0:
            w_in = ab_w_in[j]
            zx0 = E_A + E_GATES
            w_main = jnp.concatenate([w_in[:, :E_A], w_in[:, zx0:zx0 + E_B]], axis=1).astype(BF16)
            w_small = jnp.concatenate(
                [w_in[:, E_A:zx0], w_in[:, zx0 + E_B:],
                 jnp.zeros((d, 128 - E_GATES - 2 * SSM_HEADS), F32)], axis=1).astype(BF16)
            y_c, y_l = even_mixer(a_c, a_l, w_main, w_small, ab_w_out, j, ml_gate_b[j],
                                  ml_norm_g[j], ssm_conv_w[j], ssm_conv_b[j], ssm_dt_bias[j],
                                  ssm_a_log[j], ssm_d[j], ssm_norm_g[j], need_ctx)
        else:
            y_c, y_l = hgrn2_mixer(a_c, grid_to_colmajor(a_l), hg_w_in, hg_w_out, j,
                                   lower_bounds[:, layer], hg_norm_g[j], need_ctx)
        h_lat, a2_l = resid_norm(h_lat, y_l, mod_l[2], g[1], g[2], mod_l[3], mod_l[4])
        a2_c = None
        if need_ctx:
            h_ctx, a2_c = resid_norm(h_ctx, y_c, mod_c[2], g[1], g[2], mod_c[3], mod_c[4])
        f_l, f_c = ffn(a2_l, a2_c, ffn_w_gate, ffn_w_up, ffn_w_down, layer)
        gate_l = mod_l[5]
        gate_c = mod_c[5] if need_ctx else None
        g_prev = g[3]
    h_lat, _ = resid_norm(h_lat, f_l, gate_l, g_prev)
    return h_lat
```

```python
import functools
import math

import jax
import jax.numpy as jnp
import numpy as np
from jax import lax
from jax.experimental import pallas as pl
from jax.experimental.pallas import tpu as pltpu

F32 = jnp.float32
BF16 = jnp.bfloat16

D_MODEL = 4096
GRID_W = 64
EPS = 1e-6

ML_HEADS = 4
ML_QK_DIM = 256
ML_V_DIM = 512
ML_QK = ML_HEADS * ML_QK_DIM
ML_V = ML_HEADS * ML_V_DIM
ML_CHUNK = 128
ML_HEADS_PER_STEP = 4
GATE_CAP = 15.0

SSM_HEAD_DIM = 64
SSM_INNER = 2048
SSM_HEADS = 32
SSM_GROUPS = 8
SSM_HPG = 4
SSM_STATE = 128
SSM_CONV = 5
SSM_CHUNK = 128
SSM_GROUP_W = SSM_HPG * SSM_HEAD_DIM
SSM_CONV_DIM = SSM_INNER + 2 * SSM_GROUPS * SSM_STATE
SSD_GROUPS_PER_STEP = 8

HG_EXPAND = 128
HG_HEADS = 32
HG_F = HG_HEADS * HG_EXPAND
HG_V_DIM = 128
HG_CHUNK = 64
HG_LEVELS = 6
HG_HEADS_PER_STEP = 8

FFN_HIDDEN = 11008

E_Q, E_K, E_V, E_O, E_Z, E_XBC = 0, 1024, 2048, 4096, 6144, 8192
E_A = 6144
E_GATES = 16
E_B = SSM_INNER + SSM_CONV_DIM
E_MAIN = E_A + E_B
O_Q, O_F, O_I, O_G = 0, 4096, 12288, 16384

VMEM_LIMIT = 56 * 1024 * 1024


def _cparams(sem):
    return pltpu.CompilerParams(dimension_semantics=sem, vmem_limit_bytes=VMEM_LIMIT)


def _mm_kernel(*refs, has_bias, cast_w):
    x_ref, w_ref = refs[0], refs[1]
    b_ref = refs[2] if has_bias else None
    o_ref = refs[2 + has_bias]
    if cast_w:
        wb_ref = refs[3 + has_bias]

        @pl.when(pl.program_id(1) == 0)
        def _():
            wb_ref[...] = w_ref[...].astype(BF16)

        w = wb_ref[...]
    else:
        w = w_ref[...]
    acc = jnp.dot(x_ref[...].astype(BF16), w, preferred_element_type=F32)
    if has_bias:
        acc = acc + b_ref[...]
    o_ref[...] = acc.astype(o_ref.dtype)


def matmul(x, w, *, bm, bn, out_dtype, name, bias=None, layer=None, n=None):
    m, k = x.shape
    n = w.shape[-1] if n is None else n
    assert m % bm == 0 and n % bn == 0, (m, bm, n, bn)
    if layer is None:
        w_spec = pl.BlockSpec((k, bn), lambda j, i: (0, j))
    else:
        w_spec = pl.BlockSpec((None, k, bn), lambda j, i: (layer, 0, j))
    in_specs = [pl.BlockSpec((bm, k), lambda j, i: (i, 0)), w_spec]
    args = [x, w]
    if bias is not None:
        in_specs.append(pl.BlockSpec((1, bn), lambda j, i: (0, j)))
        args.append(bias.reshape(1, n))
    cast_w = w.dtype != BF16
    return pl.pallas_call(
        functools.partial(_mm_kernel, has_bias=bias is not None, cast_w=cast_w),
        grid=(n // bn, m // bm),
        in_specs=in_specs,
        out_specs=pl.BlockSpec((bm, bn), lambda j, i: (i, j)),
        out_shape=jax.ShapeDtypeStruct((m, n), out_dtype),
        scratch_shapes=[pltpu.VMEM((k, bn), BF16)] if cast_w else [],
        compiler_params=_cparams(("parallel", "arbitrary" if cast_w else "parallel")),
        name=name,
    )(*args)


def _swiglu(g, u):
    return g * jax.nn.sigmoid(g) * u


def _mm_stream_kernel(*refs, n_w, n_x, nj, mi, kc):
    x_refs, w_refs = refs[:n_x], refs[n_x:n_x + n_w]
    o_refs, wb_refs = refs[n_x + n_w:2 * n_x + n_w], refs[2 * n_x + n_w:]
    j, i = pl.program_id(0), pl.program_id(1)
    slot = j % 2

    @pl.when((j < nj) & (i < mi))
    def _():
        r0 = pl.multiple_of(i * kc, 16)
        for w_ref, wb_ref in zip(w_refs, wb_refs):
            wb_ref[slot, pl.ds(r0, kc), :] = w_ref[...].astype(BF16)

    def product(x_ref, o_ref):
        x = x_ref[...]
        accs = [jnp.dot(x, wb_ref[1 - slot], preferred_element_type=F32) for wb_ref in wb_refs]
        out = _swiglu(*accs) if n_w == 2 else accs[0]
        o_ref[...] = out.astype(o_ref.dtype)

    @pl.when((j > 0) & (i < mi))
    def _():
        product(x_refs[0], o_refs[0])

    if n_x == 2:
        @pl.when((j > 0) & (i == mi))
        def _():
            product(x_refs[1], o_refs[1])


def matmul_stream(x, ws, layer, *, bm, bn, out_dtype, name, x2=None, n=None):
    m, k = x.shape
    n = ws[0].shape[-1] if n is None else n
    mi, nj = m // bm, n // bn
    kc = k // mi
    assert m % bm == 0 and n % bn == 0 and k % mi == 0 and kc % 16 == 0 and mi > 1, (m, bm, n, bn, k)
    n_x = 1 if x2 is None else 2
    row = lambda j, i: jnp.where(j == 0, 0, jnp.minimum(i, mi - 1))
    col = lambda j: jnp.maximum(j - 1, 0)
    w_spec = pl.BlockSpec(
        (None, kc, bn),
        lambda j, i: (layer, jnp.where(j < nj, jnp.minimum(i, mi - 1), mi - 1), jnp.minimum(j, nj - 1)))
    in_specs = [pl.BlockSpec((bm, k), lambda j, i: (row(j, i), 0))]
    out_specs = [pl.BlockSpec((bm, bn), lambda j, i: (row(j, i), col(j)))]
    out_shape = [jax.ShapeDtypeStruct((m, n), out_dtype)]
    args = [x]
    if x2 is not None:
        m2 = x2.shape[0]
        assert m2 <= bm and x2.shape[1] == k
        in_specs.append(pl.BlockSpec((m2, k), lambda j, i: (0, 0)))
        out_specs.append(pl.BlockSpec((m2, bn), lambda j, i: (0, col(j))))
        out_shape.append(jax.ShapeDtypeStruct((m2, n), out_dtype))
        args.append(x2)
    outs = pl.pallas_call(
        functools.partial(_mm_stream_kernel, n_w=len(ws), n_x=n_x, nj=nj, mi=mi, kc=kc),
        grid=(nj + 1, mi + n_x - 1),
        in_specs=in_specs + [w_spec] * len(ws),
        out_specs=out_specs,
        out_shape=out_shape,
        scratch_shapes=[pltpu.VMEM((2, k, bn), BF16) for _ in ws],
        compiler_params=_cparams(("arbitrary", "arbitrary")),
        name=name,
    )(*args, *ws)
    return outs[0] if x2 is None else (outs[0], outs[1])


def _swiglu_kernel(x_ref, wg_ref, wu_ref, o_ref):
    x = x_ref[...]
    g = jnp.dot(x, wg_ref[...].astype(BF16), preferred_element_type=F32)
    u = jnp.dot(x, wu_ref[...].astype(BF16), preferred_element_type=F32)
    o_ref[...] = _swiglu(g, u).astype(o_ref.dtype)


def swiglu_single(x, wg, wu, layer, *, bn, name):
    m, k = x.shape
    n = wg.shape[-1]
    assert n % bn == 0
    w_spec = pl.BlockSpec((None, k, bn), lambda j: (layer, 0, j))
    return pl.pallas_call(
        _swiglu_kernel,
        grid=(n // bn,),
        in_specs=[pl.BlockSpec((m, k), lambda j: (0, 0)), w_spec, w_spec],
        out_specs=pl.BlockSpec((m, bn), lambda j: (0, j)),
        out_shape=jax.ShapeDtypeStruct((m, n), BF16),
        compiler_params=_cparams(("parallel",)),
        name=name,
    )(x, wg, wu)


def dense2(x, x2, w, layer, *, out_dtype, name, bm=1024, bn=1024, n=None):
    if x.shape[0] <= bm:
        ncol = w.shape[-1] if n is None else n
        one = lambda a, nm: matmul(a, w, bm=a.shape[0], bn=256, out_dtype=out_dtype, layer=layer, name=nm,
                                   n=ncol)
        return one(x, name), (None if x2 is None else one(x2, name + "_ctx"))
    y = matmul_stream(x, (w,), layer, bm=bm, bn=bn, out_dtype=out_dtype, name=name, x2=x2, n=n)
    return (y, None) if x2 is None else y


def dense(x, w, layer, *, out_dtype, name, bm=1024, bn=1024):
    m = x.shape[0]
    if m > bm:
        return matmul_stream(x, (w,), layer, bm=bm, bn=bn, out_dtype=out_dtype, name=name)
    return matmul(x, w, bm=m, bn=256, out_dtype=out_dtype, layer=layer, name=name)


def _rms(x):
    return x * lax.rsqrt(jnp.mean(x * x, axis=-1, keepdims=True) + EPS)


def _resid_norm_kernel(*refs, has_y, want_a):
    it = iter(refs)
    h_ref = next(it)
    if has_y:
        y_ref, gate_ref, g1_ref = next(it), next(it), next(it)
    if want_a:
        g2_ref, shift_ref, scale_ref = next(it), next(it), next(it)
    if has_y:
        hout_ref = next(it)
    if want_a:
        a_ref = next(it)
    h = h_ref[0]
    if has_y:
        h = h + gate_ref[0] * (_rms(y_ref[0].astype(F32)) * g1_ref[...])
        hout_ref[0] = h
    if want_a:
        a = (_rms(h) * g2_ref[...]) * (1.0 + scale_ref[0]) + shift_ref[0]
        a_ref[0] = a.astype(a_ref.dtype)


def resid_norm(h, y=None, gate=None, g1=None, g2=None, shift=None, scale=None, *, bt=128):
    b, t, d = h.shape
    has_y, want_a = y is not None, g2 is not None
    bt = min(bt, t)
    tok = pl.BlockSpec((1, bt, d), lambda i, j: (i, j, 0))
    per_b = pl.BlockSpec((1, 1, d), lambda i, j: (i, 0, 0))
    vec = pl.BlockSpec((1, d), lambda i, j: (0, 0))
    args, in_specs, out_shape, out_specs = [h], [tok], [], []
    if has_y:
        args += [y, gate, g1.reshape(1, d)]
        in_specs += [tok, per_b, vec]
        out_shape.append(jax.ShapeDtypeStruct((b, t, d), F32))
        out_specs.append(tok)
    if want_a:
        args += [g2.reshape(1, d), shift, scale]
        in_specs += [vec, per_b, per_b]
        out_shape.append(jax.ShapeDtypeStruct((b, t, d), BF16))
        out_specs.append(tok)
    outs = pl.pallas_call(
        functools.partial(_resid_norm_kernel, has_y=has_y, want_a=want_a),
        grid=(b, t // bt),
        in_specs=in_specs,
        out_specs=out_specs,
        out_shape=out_shape,
        compiler_params=_cparams(("parallel", "parallel")),
        name="resid_norm",
    )(*args)
    outs = list(outs)
    h_new = outs.pop(0) if has_y else h
    a = outs.pop(0) if want_a else None
    return h_new, a


def _softplus(x):
    return jnp.maximum(x, 0.0) + jnp.log1p(jnp.exp(-jnp.abs(x)))


def _log_sigmoid(x):
    return jnp.minimum(x, 0.0) - jnp.log1p(jnp.exp(-jnp.abs(x)))


def _chunk_index(c, nc, reverse):
    return (nc - 1 - c) if reverse else c


def _mlstm_kernel(*refs):
    ins, outs = refs[:16], refs[16:]

    @pl.when(pl.program_id(2) == 0)
    def _():
        for d in range(2):
            for s in range(3):
                outs[4 * d + 1 + s][...] = ins[8 * d + 5 + s][...]

    for hh in range(ML_HEADS_PER_STEP):
        for d in range(2):
            _mlstm_chunk(*ins[8 * d:8 * d + 5], *outs[4 * d:4 * d + 4], hh=hh, reverse=d == 1)


def _mlstm_chunk(q_ref, k_ref, v_ref, gc_ref, gr_ref, y_ref, c_ref, n_ref, m_ref, *, hh, reverse):
    qsl = slice(hh * ML_QK_DIM, (hh + 1) * ML_QK_DIM)
    vsl = slice(hh * ML_V_DIM, (hh + 1) * ML_V_DIM)
    l = ML_CHUNK
    row = lax.broadcasted_iota(jnp.int32, (l, l), 0)
    col = lax.broadcasted_iota(jnp.int32, (l, l), 1)
    mask = (col >= row) if reverse else (col <= row)
    mask_t = (col <= row) if reverse else (col >= row)

    gcol = gc_ref[0, 0, hh]
    grow = gr_ref[0, 0, hh]
    ig_c = GATE_CAP * jnp.tanh(gcol[:, 0:1] / GATE_CAP)
    lf_c = _log_sigmoid(GATE_CAP * jnp.tanh(gcol[:, 1:2] / GATE_CAP))
    ig_r = GATE_CAP * jnp.tanh(grow[0:1, :] / GATE_CAP)
    lf_r = _log_sigmoid(GATE_CAP * jnp.tanh(grow[1:2, :] / GATE_CAP))

    b_c = jnp.sum(jnp.where(mask, lf_r, 0.0), axis=1, keepdims=True)
    b_r = jnp.sum(jnp.where(mask_t, lf_c, 0.0), axis=0, keepdims=True)
    total = jnp.sum(lf_c, axis=0, keepdims=True)

    m_prev = m_ref[0, hh]
    logw = jnp.where(mask, b_c - b_r + ig_r, -jnp.inf)
    inter = b_c + m_prev
    m_t = jnp.maximum(jnp.max(logw, axis=1, keepdims=True), inter)

    q = q_ref[0, :, qsl]
    k = k_ref[0, :, qsl]
    v = v_ref[0, :, vsl]
    scale = ML_QK_DIM ** -0.5
    qk = lax.dot_general(q, k, (((1,), (1,)), ((), ())), preferred_element_type=F32)
    s = qk * scale * jnp.exp(logw - m_t)
    inter_w = jnp.exp(inter - m_t) * scale

    c_st = c_ref[0, hh]
    n_st = n_ref[0, hh]
    num = (jnp.dot(s.astype(BF16), v, preferred_element_type=F32)
           + inter_w * jnp.dot(q, c_st.astype(BF16), preferred_element_type=F32))
    qn = jnp.sum(q.astype(F32) * n_st, axis=1, keepdims=True)
    den = jnp.sum(s, axis=1, keepdims=True) + inter_w * qn
    y_ref[0, :, vsl] = (num * (1.0 / jnp.maximum(jnp.abs(den), jnp.exp(-m_t)))).astype(y_ref.dtype)

    g_c = total - b_c + ig_c
    g_r = total - b_r + ig_r
    m_new = jnp.maximum(total + m_prev, jnp.max(g_r, axis=1, keepdims=True))
    w_c = jnp.exp(g_c - m_new)
    decay = jnp.exp(total + m_prev - m_new)
    kw = k.astype(F32) * w_c
    kv = lax.dot_general(kw.astype(BF16), v, (((0,), (0,)), ((), ())), preferred_element_type=F32)
    c_ref[0, hh] = decay * c_st + kv
    n_ref[0, hh] = decay * n_st + jnp.sum(kw, axis=0, keepdims=True)
    m_ref[0, hh] = m_new


def mlstm_scan(proj, gates_col, gates_row, states, *, name):
    b, t, _ = proj.shape
    l = ML_CHUNK
    nc = t // l
    hp = ML_HEADS_PER_STEP
    qw, vw = hp * ML_QK_DIM, hp * ML_V_DIM
    qb, kb, vb = E_Q // qw, E_K // qw, E_V // vw
    st_specs = [pl.BlockSpec((1, hp, ML_QK_DIM, ML_V_DIM), lambda i, h, c: (i, h, 0, 0)),
                pl.BlockSpec((1, hp, 1, ML_QK_DIM), lambda i, h, c: (i, h, 0, 0)),
                pl.BlockSpec((1, hp, 1, 1), lambda i, h, c: (i, h, 0, 0))]
    in_specs, args, out_specs, out_shape = [], [], [], []
    for d in range(2):
        ci = functools.partial(_chunk_index, nc=nc, reverse=d == 1)
        in_specs += [
            pl.BlockSpec((1, l, qw), lambda i, h, c, ci=ci: (i, ci(c), qb + h)),
            pl.BlockSpec((1, l, qw), lambda i, h, c, ci=ci: (i, ci(c), kb + h)),
            pl.BlockSpec((1, l, vw), lambda i, h, c, ci=ci: (i, ci(c), vb + h)),
            pl.BlockSpec((1, 1, hp, l, 2), lambda i, h, c, ci=ci, d=d: (d, i, h, ci(c), 0)),
            pl.BlockSpec((1, 1, hp, 2, l), lambda i, h, c, ci=ci, d=d: (d, i, h, 0, ci(c))),
        ] + st_specs
        args += [proj, proj, proj, gates_col, gates_row, *states[d]]
        out_specs += [pl.BlockSpec((1, l, vw), lambda i, h, c, ci=ci: (i, ci(c), h))] + st_specs
        out_shape += [jax.ShapeDtypeStruct((b, t, ML_V), BF16)]
        out_shape += [jax.ShapeDtypeStruct(s.shape, F32) for s in states[d]]
    outs = pl.pallas_call(
        _mlstm_kernel,
        grid=(b, ML_HEADS // hp, nc),
        in_specs=in_specs,
        out_specs=out_specs,
        out_shape=out_shape,
        compiler_params=_cparams(("parallel", "parallel", "arbitrary")),
        name=name,
    )(*args)
    return (outs[0], outs[4]), (tuple(outs[1:4]), tuple(outs[5:8]))


CONV_ROWS = 256
CONV_HALO = 16


def _conv_kernel(x_ref, w_ref, b_ref, o_ref, *, t):
    w = w_ref[...]
    bias = b_ref[...]
    r = min(CONV_ROWS, t)
    half = SSM_CONV // 2
    cw = x_ref.shape[-1]
    for r0 in range(0, t, r):
        parts = []
        if r0 == 0:
            parts.append(jnp.zeros((CONV_HALO, cw), F32))
        else:
            parts.append(x_ref[0, pl.ds(r0 - CONV_HALO, CONV_HALO), :].astype(F32))
        parts.append(x_ref[0, pl.ds(r0, r), :].astype(F32))
        if r0 + r == t:
            parts.append(jnp.zeros((CONV_HALO, cw), F32))
        else:
            parts.append(x_ref[0, pl.ds(r0 + r, CONV_HALO), :].astype(F32))
        win = jnp.concatenate(parts, axis=0)
        n = r + 2 * CONV_HALO
        acc = jnp.zeros((r, cw), F32) + bias
        for kk in range(SSM_CONV):
            sh = (half - kk) % n
            rolled = win if sh == 0 else pltpu.roll(win, sh, 0)
            acc = acc + w[kk:kk + 1, :] * rolled[CONV_HALO:CONV_HALO + r, :]
        o_ref[0, pl.ds(r0, r), :] = (acc * jax.nn.sigmoid(acc)).astype(o_ref.dtype)


def conv_silu(proj, conv_w, conv_b, *, bc=512):
    b, t, _ = proj.shape
    c = SSM_CONV_DIM
    off = E_XBC // bc
    return pl.pallas_call(
        functools.partial(_conv_kernel, t=t),
        grid=(b, c // bc),
        in_specs=[pl.BlockSpec((1, t, bc), lambda i, j: (i, 0, off + j)),
                  pl.BlockSpec((SSM_CONV, bc), lambda i, j: (0, j)),
                  pl.BlockSpec((1, bc), lambda i, j: (0, j))],
        out_specs=pl.BlockSpec((1, t, bc), lambda i, j: (i, 0, j)),
        out_shape=jax.ShapeDtypeStruct((b, t, c), BF16),
        compiler_params=_cparams(("parallel", "parallel")),
        name="conv_silu",
    )(proj, conv_w, conv_b.reshape(1, c))


def _split3(x):
    hi = x.astype(BF16)
    r1 = x - hi.astype(F32)
    mid = r1.astype(BF16)
    lo = (r1 - mid.astype(F32)).astype(BF16)
    return hi, mid, lo


def _dot3(dims, parts_lhs, rhs):
    return sum(lax.dot_general(p, rhs, dims, preferred_element_type=F32) for p in parts_lhs)


def _head_selector(heads, width):
    rows = lax.broadcasted_iota(jnp.int32, (heads, heads * width), 0)
    lanes = lax.broadcasted_iota(jnp.int32, (heads, heads * width), 1)
    return (lanes // width == rows).astype(BF16)


def _ssd_kernel(*refs):
    ins, outs = refs[:12], refs[12:]

    @pl.when(pl.program_id(2) == 0)
    def _():
        for d in range(2):
            outs[2 * d + 1][...] = ins[6 * d + 5][...]

    for d in range(2):
        _ssd_chunk(*ins[6 * d:6 * d + 5], *outs[2 * d:2 * d + 2], reverse=d == 1)


def _ssd_chunk(xs_ref, bm_ref, cm_ref, dtr_ref, pr_ref, y_ref, h_ref, *, reverse):
    l = SSM_CHUNK
    gp = SSD_GROUPS_PER_STEP
    nh = gp * SSM_HPG
    row = lax.broadcasted_iota(jnp.int32, (l, l), 0)
    col = lax.broadcasted_iota(jnp.int32, (l, l), 1)
    mask = (col >= row) if reverse else (col <= row)
    mask_t = (col <= row) if reverse else (col >= row)

    nn = (((1,), (0,)), ((), ()))
    nt = (((1,), (1,)), ((), ()))
    tn = (((0,), (0,)), ((), ()))
    pr = pr_ref[0, 0]
    dt_r = _softplus(dtr_ref[0, 0, 0] + pr[:, 0:1])
    a_r = dt_r * pr[:, 1:2]
    cum_r = _dot3(nn, _split3(a_r), mask_t.astype(BF16))
    dt_parts, cum_parts = _split3(dt_r), _split3(cum_r)
    sel_ch = _head_selector(nh, SSM_HEAD_DIM)
    dt_ch = _dot3(tn, dt_parts, sel_ch)
    cum_ch = _dot3(tn, cum_parts, sel_ch)
    cum_bc = _dot3(tn, cum_parts, _head_selector(nh, l))
    end_row = 0 if reverse else l - 1
    total_ch = cum_ch[end_row:end_row + 1, :]

    xdt = xs_ref[0].astype(F32) * dt_ch
    e_cum = jnp.exp(cum_ch)
    xw = (xdt * jnp.exp(total_ch - cum_ch)).astype(BF16)
    s_decay = jnp.exp(total_ch)
    lane_head = lax.broadcasted_iota(jnp.int32, (l, SSM_GROUP_W), 1) // SSM_HEAD_DIM
    for g in range(gp):
        xsl = slice(g * SSM_GROUP_W, (g + 1) * SSM_GROUP_W)
        nsl = slice(g * SSM_STATE, (g + 1) * SSM_STATE)
        bm = bm_ref[0, :, nsl]
        cm = cm_ref[0, :, nsl]
        hst = h_ref[0, g]
        cb = lax.dot_general(cm, bm, nt, preferred_element_type=F32)
        inter = jnp.dot(cm, hst.astype(BF16), preferred_element_type=F32)
        xdt_g = xdt[:, xsl]
        ws, xjs = [], []
        for j in range(SSM_HPG):
            hd = g * SSM_HPG + j
            decay = jnp.exp(jnp.where(mask, cum_bc[:, hd * l:(hd + 1) * l] - cum_r[hd:hd + 1, :], -jnp.inf))
            ws.append((decay * cb).astype(BF16))
            xjs.append(jnp.where(lane_head == j, xdt_g, 0.0).astype(BF16))
        y = jnp.dot(jnp.concatenate(ws, axis=1), jnp.concatenate(xjs, axis=0), preferred_element_type=F32)
        y_ref[0, :, xsl] = (y + e_cum[:, xsl] * inter).astype(y_ref.dtype)
        upd = lax.dot_general(bm, xw[:, xsl], tn, preferred_element_type=F32)
        h_ref[0, g] = s_decay[:, xsl] * hst + upd


def ssd_scan(xbc, dt_row, par_row, states, *, name):
    b, t, _ = xbc.shape
    l = SSM_CHUNK
    nc = t // l
    gp = SSD_GROUPS_PER_STEP
    xw, nw = gp * SSM_GROUP_W, gp * SSM_STATE
    bmb = SSM_INNER // nw
    cmb = bmb + SSM_GROUPS // gp
    st_spec = pl.BlockSpec((1, gp, SSM_STATE, SSM_GROUP_W), lambda i, g, c: (i, g, 0, 0))
    dt_step = dt_row.reshape(2, b, SSM_GROUPS // gp, gp * SSM_HPG, t)
    par_step = par_row.reshape(2, SSM_GROUPS // gp, gp * SSM_HPG, 2)
    in_specs, args, out_specs, out_shape = [], [], [], []
    for d in range(2):
        ci = functools.partial(_chunk_index, nc=nc, reverse=d == 1)
        in_specs += [
            pl.BlockSpec((1, l, xw), lambda i, g, c, ci=ci: (i, ci(c), g)),
            pl.BlockSpec((1, l, nw), lambda i, g, c, ci=ci: (i, ci(c), bmb + g)),
            pl.BlockSpec((1, l, nw), lambda i, g, c, ci=ci: (i, ci(c), cmb + g)),
            pl.BlockSpec((1, 1, 1, gp * SSM_HPG, l), lambda i, g, c, ci=ci, d=d: (d, i, g, 0, ci(c))),
            pl.BlockSpec((1, 1, gp * SSM_HPG, 2), lambda i, g, c, d=d: (d, g, 0, 0)),
            st_spec,
        ]
        args += [xbc, xbc, xbc, dt_step, par_step, states[d]]
        out_specs += [pl.BlockSpec((1, l, xw), lambda i, g, c, ci=ci: (i, ci(c), g)), st_spec]
        out_shape += [jax.ShapeDtypeStruct((b, t, SSM_INNER), BF16),
                      jax.ShapeDtypeStruct(states[d].shape, F32)]
    yf, hf, yb, hb = pl.pallas_call(
        _ssd_kernel,
        grid=(b, SSM_GROUPS // gp, nc),
        in_specs=in_specs,
        out_specs=out_specs,
        out_shape=out_shape,
        compiler_params=_cparams(("parallel", "parallel", "arbitrary")),
        name=name,
    )(*args)
    return (yf, yb), (hf, hb)


def _even_out_kernel(mlf_ref, mlb_ref, ssdf_ref, ssdb_ref, o_ref, z_ref, xs_ref, mlg_ref, dsk_ref, ssg_ref,
                     out_ref):
    for h in range(ML_HEADS):
        sl = slice(h * ML_V_DIM, (h + 1) * ML_V_DIM)
        y = mlf_ref[0, :, sl].astype(F32) + mlb_ref[0, :, sl].astype(F32)
        og = o_ref[0, :, sl].astype(F32)
        out_ref[0, :, sl] = (_rms(y) * mlg_ref[:, sl] * jax.nn.sigmoid(og)).astype(out_ref.dtype)
    for g in range(SSM_GROUPS):
        sl = slice(g * SSM_GROUP_W, (g + 1) * SSM_GROUP_W)
        z = z_ref[0, :, sl].astype(F32)
        y = ((ssdf_ref[0, :, sl].astype(F32) + ssdb_ref[0, :, sl].astype(F32))
             + dsk_ref[:, sl] * xs_ref[0, :, sl].astype(F32))
        y = y * (z * jax.nn.sigmoid(z))
        so = slice(ML_V + g * SSM_GROUP_W, ML_V + (g + 1) * SSM_GROUP_W)
        out_ref[0, :, so] = (_rms(y) * ssg_ref[:, sl]).astype(out_ref.dtype)


def even_out(ml_y, ssd_y, proj_a, proj_b, xbc, ml_norm_g, d_skip_full, ssm_norm_g, *, bt=128):
    b, t, _ = ml_y[0].shape
    bt = min(bt, t)
    w = ML_V
    tok = lambda blk: pl.BlockSpec((1, bt, w), lambda i, j: (i, j, blk))
    vec = pl.BlockSpec((1, w), lambda i, j: (0, 0))
    return pl.pallas_call(
        _even_out_kernel,
        grid=(b, t // bt),
        in_specs=[tok(0), tok(0), tok(0), tok(0), tok(E_O // w), tok(E_Z // w), tok(0), vec, vec, vec],
        out_specs=pl.BlockSpec((1, bt, 2 * w), lambda i, j: (i, j, 0)),
        out_shape=jax.ShapeDtypeStruct((b, t, 2 * w), BF16),
        compiler_params=_cparams(("parallel", "parallel")),
        name="even_gate",
    )(ml_y[0], ml_y[1], ssd_y[0], ssd_y[1], proj_a, proj_b, xbc, ml_norm_g.reshape(1, w), d_skip_full.reshape(1, w),
      ssm_norm_g.reshape(1, w))


def _hg_constants(reverse):
    l = HG_CHUNK
    pos = np.arange(l)
    tri = (pos[None, :] <= pos[:, None]).astype(np.float32)
    masks = []
    for lv in range(HG_LEVELS):
        m = 1 << lv
        blk = pos // (2 * m)
        later = (pos % (2 * m)) >= m
        masks.append(((blk[:, None] == blk[None, :]) & later[:, None] & (~later)[None, :]).astype(np.float32))
    masks = np.stack(masks)
    if reverse:
        tri = tri[::-1, ::-1]
        masks = masks[:, ::-1, ::-1]
    return (jnp.asarray(tri, BF16), jnp.asarray(masks.reshape(-1, l), F32))


def _hg_level_factors(q, k, f, cum, reverse):
    l, w = q.shape
    row = lax.broadcasted_iota(jnp.int32, (l, w), 0)
    sub = lax.broadcasted_iota(jnp.int32, (8, w), 0)
    zs = []
    for lv in range(HG_LEVELS):
        m = 1 << lv
        late = ((row & m) == 0) if reverse else ((row & m) != 0)
        if m >= 8:
            pieces = []
            for b0 in range(0, l, 2 * m):
                lo_rows, hi_rows = slice(b0, b0 + m), slice(b0 + m, b0 + 2 * m)
                if reverse:
                    mid = cum[b0 + m:b0 + m + 1]
                    pieces.append(q[lo_rows] * jnp.exp(cum[lo_rows] - mid))
                    pieces.append(k[hi_rows] * jnp.exp(mid - cum[hi_rows]))
                else:
                    mid = cum[b0 + m - 1:b0 + m]
                    pieces.append(k[lo_rows] * jnp.exp(mid - cum[lo_rows]))
                    pieces.append(q[hi_rows] * jnp.exp(cum[hi_rows] - mid))
            zs.append(jnp.concatenate(pieces, axis=0).astype(BF16))
            continue
        if m == 1:
            ex = jnp.where(late, f, 1.0)
        else:
            pieces = []
            if m >= 4:
                for b0 in range(0, l, 2 * m):
                    r = b0 + (m if reverse else m - 1)
                    pieces.append(cum[b0:b0 + 2 * m] - cum[r:r + 1])
            else:
                for b0 in range(0, l, 8):
                    ra, rb = (b0 + 2, b0 + 6) if reverse else (b0 + 1, b0 + 5)
                    mid = jnp.where(sub < 4, cum[ra:ra + 1], cum[rb:rb + 1])
                    pieces.append(cum[b0:b0 + 8] - mid)
            dd = pieces[0] if len(pieces) == 1 else jnp.concatenate(pieces, axis=0)
            ex = jnp.exp(jnp.where(late, dd, -dd))
        zs.append((jnp.where(late, q, k) * ex).astype(BF16))
    return zs


def _hg_chunk(q_ref, f_ref, i_ref, lb, tri, masks_ref, y_ref, s_ref, r0, reverse):
    l, e = HG_CHUNK, HG_EXPAND
    q_raw = q_ref[0, pl.ds(r0, l), :].astype(F32)
    f_raw = f_ref[0, pl.ds(r0, l), :].astype(F32)
    v = i_ref[0, pl.ds(r0, l), :]
    f = lb + (1.0 - lb) * jax.nn.sigmoid(f_raw)
    k = 1.0 - f
    lf = jnp.log(f)
    q = q_raw * jax.nn.sigmoid(q_raw) * (e ** -0.5)
    hi, mid, lo = _split3(lf)
    cum = (jnp.dot(tri, hi, preferred_element_type=F32)
           + jnp.dot(tri, mid, preferred_element_type=F32)
           + jnp.dot(tri, lo, preferred_element_type=F32))
    zs = _hg_level_factors(q, k, f, cum, reverse)
    end_row = 0 if reverse else l - 1
    cum_end = cum[end_row:end_row + 1, :]
    qc = (q * jnp.exp(cum)).astype(BF16)
    kt_end = (k * jnp.exp(cum_end - cum)).astype(BF16)
    s_decay = jnp.exp(cum_end)
    qk = q * k
    row = lax.broadcasted_iota(jnp.int32, (l, l), 0)
    col = lax.broadcasted_iota(jnp.int32, (l, l), 1)
    eye = row == col
    nt = (((1,), (1,)), ((), ()))
    tn = (((0,), (0,)), ((), ()))
    for hh in range(HG_HEADS_PER_STEP):
        sl = slice(hh * e, (hh + 1) * e)
        att = jnp.where(eye, jnp.sum(qk[:, sl], axis=1, keepdims=True), 0.0)
        for lv in range(HG_LEVELS):
            z = zs[lv][:, sl]
            a_lv = lax.dot_general(z, z, nt, preferred_element_type=F32)
            att = att + masks_ref[pl.ds(lv * l, l), :] * a_lv
        st = s_ref[0, hh]
        vh = v[:, sl]
        o = (jnp.dot(att.astype(BF16), vh, preferred_element_type=F32)
             + lax.dot_general(qc[:, sl], st.astype(BF16), nt, preferred_element_type=F32))
        y_ref[0, pl.ds(r0, l), sl] = o.astype(y_ref.dtype)
        upd = lax.dot_general(vh, kt_end[:, sl], tn, preferred_element_type=F32)
        s_ref[0, hh] = s_decay[:, sl] * st + upd


def _hg_kernel(qf_ref, ff_ref, if_ref, qb_ref, fb_ref, ib_ref, lbf_ref, lbb_ref, trif_ref, trib_ref,
               mf_ref, mb_ref, sf0_ref, sb0_ref, yf_ref, yb_ref, sf_ref, sb_ref, *, n_inner):
    l = HG_CHUNK

    @pl.when(pl.program_id(2) == 0)
    def _():
        sf_ref[...] = sf0_ref[...]
        sb_ref[...] = sb0_ref[...]

    lbf = lbf_ref[0]
    lbb = lbb_ref[0]
    trif = trif_ref[...]
    trib = trib_ref[...]

    def chunk(ci, carry):
        rf = pl.multiple_of(ci * l, l)
        rb = pl.multiple_of((n_inner - 1 - ci) * l, l)
        _hg_chunk(qf_ref, ff_ref, if_ref, lbf, trif, mf_ref, yf_ref, sf_ref, rf, False)
        _hg_chunk(qb_ref, fb_ref, ib_ref, lbb, trib, mb_ref, yb_ref, sb_ref, rb, True)
        return carry

    lax.fori_loop(0, n_inner, chunk, 0, unroll=4)


def hgrn2_scan(proj, lb, states, *, tb=512, name):
    b, t, _ = proj.shape
    tb = min(tb, t)
    nb = t // tb
    n_inner = tb // HG_CHUNK
    hw = HG_HEADS_PER_STEP * HG_EXPAND
    nh = HG_HEADS // HG_HEADS_PER_STEP
    qb_, ffb, fbb, ib_ = O_Q // hw, O_F // hw, (O_F + HG_F) // hw, O_I // hw
    trif, mf = _hg_constants(False)
    trib, mb = _hg_constants(True)
    fwd = lambda blk: pl.BlockSpec((1, tb, hw), lambda i, h, c: (i, c, blk + h))
    bwd = lambda blk: pl.BlockSpec((1, tb, hw), lambda i, h, c: (i, nb - 1 - c, blk + h))
    const = lambda a: pl.BlockSpec(a.shape, lambda i, h, c: (0,) * a.ndim)
    st_spec = pl.BlockSpec((1, HG_HEADS_PER_STEP, HG_V_DIM, HG_EXPAND), lambda i, h, c: (i, h, 0, 0))
    in_specs = [fwd(qb_), fwd(ffb), fwd(ib_), bwd(qb_), bwd(fbb), bwd(ib_),
                pl.BlockSpec((1, 1, hw), lambda i, h, c: (0, 0, h)),
                pl.BlockSpec((1, 1, hw), lambda i, h, c: (1, 0, h)),
                const(trif), const(trib), const(mf), const(mb), st_spec, st_spec]
    yf, yb, sf, sb = pl.pallas_call(
        functools.partial(_hg_kernel, n_inner=n_inner),
        grid=(b, nh, nb),
        in_specs=in_specs,
        out_specs=[fwd(0), bwd(0), st_spec, st_spec],
        out_shape=[jax.ShapeDtypeStruct((b, t, D_MODEL), BF16),
                   jax.ShapeDtypeStruct((b, t, D_MODEL), BF16),
                   jax.ShapeDtypeStruct(states[0].shape, F32),
                   jax.ShapeDtypeStruct(states[1].shape, F32)],
        compiler_params=_cparams(("parallel", "parallel", "arbitrary")),
        name=name,
    )(proj, proj, proj, proj, proj, proj, lb, lb, trif, trib, mf, mb, states[0], states[1])
    return (yf, yb), (sf, sb)


def _hg_out_kernel(of_ref, ob_ref, g_ref, ng_ref, out_ref):
    for h in range(of_ref.shape[-1] // HG_V_DIM):
        sl = slice(h * HG_V_DIM, (h + 1) * HG_V_DIM)
        g = g_ref[0, :, sl].astype(F32)
        o = of_ref[0, :, sl].astype(F32) + ob_ref[0, :, sl].astype(F32)
        out_ref[0, :, sl] = (_rms(o) * ng_ref[:, sl] * (g * jax.nn.sigmoid(g))).astype(out_ref.dtype)


def hg_out(o_f, o_b, proj, norm_g, *, bt=256, bc=1024, name):
    b, t, d = o_f.shape
    bt = min(bt, t)
    gb = O_G // bc
    tok = pl.BlockSpec((1, bt, bc), lambda i, j, c: (i, j, c))
    return pl.pallas_call(
        _hg_out_kernel,
        grid=(b, t // bt, d // bc),
        in_specs=[tok, tok,
                  pl.BlockSpec((1, bt, bc), lambda i, j, c: (i, j, gb + c)),
                  pl.BlockSpec((1, bc), lambda i, j, c: (0, c))],
        out_specs=tok,
        out_shape=jax.ShapeDtypeStruct((b, t, d), BF16),
        compiler_params=_cparams(("parallel", "parallel", "parallel")),
        name=name,
    )(o_f, o_b, proj, norm_g.reshape(1, d))


def even_mixer(a_c, a_l, w_main, w_small, w_out, layer, ml_gate_b, ml_norm_g, conv_w, conv_b,
               dt_bias, a_log, d_skip, ssm_norm_g, need_ctx):
    bsz = a_l.shape[0]

    def features(a, tag):
        b, t, d = a.shape
        bm = min(1024, b * t)
        a2 = a.reshape(b * t, d)
        proj = matmul(a2, w_main, bm=bm, bn=1024, out_dtype=BF16, name="ab_in_" + tag).reshape(b, t, E_MAIN)
        proj_a = proj
        small = matmul(a2, w_small, bm=bm, bn=128, out_dtype=F32, name="ab_in_small_" + tag).reshape(b, t, 128)
        gates = small[..., :16].reshape(b, t, 2, 2, ML_HEADS) + ml_gate_b.astype(F32)
        g_col = gates.transpose(2, 0, 4, 1, 3)
        g_row = gates.transpose(2, 0, 4, 3, 1)
        dt = small[..., 16:16 + 2 * SSM_HEADS].reshape(b, t, 2, SSM_GROUPS, SSM_HPG)
        dt_row = dt.transpose(2, 0, 3, 4, 1)
        xbc = conv_silu(proj, conv_w, conv_b)
        return dict(proj_a=proj_a, proj=proj, xbc=xbc, g_col=g_col, g_row=g_row, dt_row=dt_row)

    fc, fl = features(a_c, "ctx"), features(a_l, "lat")
    neg_a = -jnp.exp(a_log.astype(F32))
    par = jnp.stack([dt_bias.astype(F32), neg_a], axis=1).reshape(2, 2, SSM_GROUPS, SSM_HPG)
    par_row = par.transpose(0, 2, 3, 1)

    ml_init = (jnp.zeros((bsz, ML_HEADS, ML_QK_DIM, ML_V_DIM), F32),
               jnp.zeros((bsz, ML_HEADS, 1, ML_QK_DIM), F32),
               jnp.zeros((bsz, ML_HEADS, 1, 1), F32))
    ssd_init = jnp.zeros((bsz, SSM_GROUPS, SSM_STATE, SSM_GROUP_W), F32)

    def ml(f, st, tag):
        return mlstm_scan(f['proj_a'], f['g_col'], f['g_row'], st, name="mlstm_scan_" + tag)

    def ssd(f, st, tag):
        return ssd_scan(f['xbc'], f['dt_row'], par_row, st, name="ssd_scan_" + tag)

    ml_c, ml_st = ml(fc, (ml_init, ml_init), "ctx")
    ml_l, _ = ml(fl, ml_st, "lat")
    ssd_c, ssd_st = ssd(fc, (ssd_init, ssd_init), "ctx")
    ssd_l, _ = ssd(fl, ssd_st, "lat")
    d_full = jnp.repeat(d_skip.astype(F32), SSM_HEAD_DIM)

    def gated(f, ml_y, ssd_y):
        b, t, _ = ml_y[0].shape
        cat = even_out(ml_y, ssd_y, f['proj_a'], f['proj'], f['xbc'], ml_norm_g, d_full, ssm_norm_g)
        return cat.reshape(b * t, D_MODEL)

    cat_c = gated(fc, ml_c, ssd_c) if need_ctx else None
    y_l, y_c = dense2(gated(fl, ml_l, ssd_l), cat_c, w_out, layer, out_dtype=BF16, name="ab_out")
    y_l = y_l.reshape(a_l.shape)
    return (y_c.reshape(a_c.shape) if need_ctx else None), y_l


def hgrn2_mixer(a_c, a_l, w_in, w_out, layer, lb, norm_g, need_ctx, lat_colmajor=True):
    bsz = a_l.shape[0]
    lb3 = lb.astype(F32).reshape(2, 1, HG_F)

    flat = lambda a: a.reshape(a.shape[0] * a.shape[1], a.shape[2])
    pl_, pc = dense2(flat(a_l), flat(a_c), w_in, layer, out_dtype=BF16, name="hg_in")
    pl_ = pl_.reshape(a_l.shape[:2] + (5 * D_MODEL,))
    pc = pc.reshape(a_c.shape[:2] + (5 * D_MODEL,))
    init = jnp.zeros((bsz, HG_HEADS, HG_V_DIM, HG_EXPAND), F32)
    o_c, st = hgrn2_scan(pc, lb3, (init, init), name="hg_scan_ctx")
    o_l, _ = hgrn2_scan(pl_, lb3, st, name="hg_scan_lat")

    def gated(p, o, colmajor, tag):
        g = hg_out(o[0], o[1], p, norm_g, name="hg_gate_" + tag)
        return flat(grid_from_colmajor(g) if colmajor else g)

    g_c = gated(pc, o_c, False, "ctx") if need_ctx else None
    y_l, y_c = dense2(gated(pl_, o_l, lat_colmajor, "lat"), g_c, w_out, layer, out_dtype=BF16, name="hg_out")
    return (y_c.reshape(a_c.shape) if need_ctx else None), y_l.reshape(a_l.shape)


def grid_to_colmajor(h):
    bsz, t, d = h.shape
    rows = t // GRID_W
    return h.reshape(bsz, rows, GRID_W, d).transpose(0, 2, 1, 3).reshape(bsz, t, d)


def grid_from_colmajor(h):
    bsz, t, d = h.shape
    rows = t // GRID_W
    return h.reshape(bsz, GRID_W, rows, d).transpose(0, 2, 1, 3).reshape(bsz, t, d)


def ffn(a_l, a_c, wg, wu, wd, layer):
    flat = lambda a: a.reshape(a.shape[0] * a.shape[1], a.shape[2])
    hid_l = matmul_stream(flat(a_l), (wg, wu), layer, bm=2048, bn=256, out_dtype=BF16, name="ffn_up_lat")
    f_l = dense(hid_l, wd, layer, out_dtype=BF16, name="ffn_down_lat", bm=512, bn=512).reshape(a_l.shape)
    if a_c is None:
        return f_l, None
    hid_c = swiglu_single(flat(a_c), wg, wu, layer, bn=256, name="ffn_up_ctx")
    f_c = dense(hid_c, wd, layer, out_dtype=BF16, name="ffn_down_ctx", bm=512, bn=512).reshape(a_c.shape)
    return f_l, f_c


def kernel(x, c, ctx, c_ctx, ada_w, ada_b, norm_g, ffn_w_gate, ffn_w_up, ffn_w_down,
           ab_w_in, ab_w_out, ml_gate_b, ml_norm_g, ssm_conv_w, ssm_conv_b, ssm_dt_bias,
           ssm_a_log, ssm_d, ssm_norm_g, hg_w_in, hg_w_out, hg_lb, hg_norm_g):
    depth = ada_w.shape[0]
    bsz = x.shape[0]
    d = D_MODEL
    lb_p = jax.nn.softmax(hg_lb.astype(F32), axis=1)
    lower_bounds = jnp.cumsum(lb_p, axis=1) - lb_p[:, :1]
    cond = jnp.concatenate([jax.nn.silu(c), jax.nn.silu(c_ctx)[None, :],
                            jnp.zeros((8 - bsz - 1, d), F32)], axis=0)
    h_lat, h_ctx = x, ctx
    y_l = y_c = f_l = f_c = None
    gate_l = gate_c = g_prev = None
    for layer in range(depth):
        need_ctx = layer < depth - 1
        j = layer // 2
        mod = matmul(cond, ada_w, bm=8, bn=512, out_dtype=F32, bias=ada_b[layer], layer=layer,
                     name="ada_mod")
        mod_l = [mod[:bsz, i * d:(i + 1) * d][:, None, :] for i in range(6)]
        mod_c = [jnp.broadcast_to(mod[bsz, i * d:(i + 1) * d][None, None, :], (bsz, 1, d)) for i in range(6)]
        g = norm_g[layer]
        h_lat, a_l = resid_norm(h_lat, f_l, gate_l, g_prev, g[0], mod_l[0], mod_l[1])
        h_ctx, a_c = resid_norm(h_ctx, f_c, gate_c, g_prev, g[0], mod_c[0], mod_c[1])
        if layer % 2 == 0:
            w_in = ab_w_in[j]
            zx0 = E_A + E_GATES
            w_main = jnp.concatenate([w_in[:, :E_A].astype(BF16), w_in[:, zx0:zx0 + E_B].astype(BF16)], axis=1)
            w_small = jnp.concatenate(
                [w_in[:, E_A:zx0], w_in[:, zx0 + E_B:],
                 jnp.zeros((d, 128 - E_GATES - 2 * SSM_HEADS), F32)], axis=1).astype(BF16)
            y_c, y_l = even_mixer(a_c, a_l, w_main, w_small, ab_w_out, j, ml_gate_b[j],
                                  ml_norm_g[j], ssm_conv_w[j], ssm_conv_b[j], ssm_dt_bias[j],
                                  ssm_a_log[j], ssm_d[j], ssm_norm_g[j], need_ctx)
        else:
            y_c, y_l = hgrn2_mixer(a_c, grid_to_colmajor(a_l), hg_w_in, hg_w_out, j,
                                   lower_bounds[:, layer], hg_norm_g[j], need_ctx)
        h_lat, a2_l = resid_norm(h_lat, y_l, mod_l[2], g[1], g[2], mod_l[3], mod_l[4])
        a2_c = None
        if need_ctx:
            h_ctx, a2_c = resid_norm(h_ctx, y_c, mod_c[2], g[1], g[2], mod_c[3], mod_c[4])
        f_l, f_c = ffn(a2_l, a2_c, ffn_w_gate, ffn_w_up, ffn_w_down, layer)
        gate_l = mod_l[5]
        gate_c = mod_c[5] if need_ctx else None
        g_prev = g[3]
    h_lat, _ = resid_norm(h_lat, f_l, gate_l, g_prev)
    return h_lat
```

```python
import functools
import math

import jax
import jax.numpy as jnp
import numpy as np
from jax import lax
from jax.experimental import pallas as pl
from jax.experimental.pallas import tpu as pltpu

F32 = jnp.float32
BF16 = jnp.bfloat16

D_MODEL = 4096
GRID_W = 64
EPS = 1e-6

ML_HEADS = 4
ML_QK_DIM = 256
ML_V_DIM = 512
ML_QK = ML_HEADS * ML_QK_DIM
ML_V = ML_HEADS * ML_V_DIM
ML_CHUNK = 128
ML_HEADS_PER_STEP = 4
GATE_CAP = 15.0

SSM_HEAD_DIM = 64
SSM_INNER = 2048
SSM_HEADS = 32
SSM_GROUPS = 8
SSM_HPG = 4
SSM_STATE = 128
SSM_CONV = 5
SSM_CHUNK = 128
SSM_GROUP_W = SSM_HPG * SSM_HEAD_DIM
SSM_CONV_DIM = SSM_INNER + 2 * SSM_GROUPS * SSM_STATE
SSD_GROUPS_PER_STEP = 8

HG_EXPAND = 128
HG_HEADS = 32
HG_F = HG_HEADS * HG_EXPAND
HG_V_DIM = 128
HG_CHUNK = 64
HG_LEVELS = 6
HG_HEADS_PER_STEP = 8

FFN_HIDDEN = 11008

E_Q, E_K, E_V, E_O, E_Z, E_XBC = 0, 1024, 2048, 4096, 6144, 8192
E_A = 6144
E_GATES = 16
E_B = SSM_INNER + SSM_CONV_DIM
E_MAIN = E_A + E_B
O_Q, O_F, O_I, O_G = 0, 4096, 12288, 16384

VMEM_LIMIT = 56 * 1024 * 1024


def _cparams(sem):
    return pltpu.CompilerParams(dimension_semantics=sem, vmem_limit_bytes=VMEM_LIMIT)


def _mm_kernel(*refs, has_bias, cast_w):
    x_ref, w_ref = refs[0], refs[1]
    b_ref = refs[2] if has_bias else None
    o_ref = refs[2 + has_bias]
    if cast_w:
        wb_ref = refs[3 + has_bias]

        @pl.when(pl.program_id(1) == 0)
        def _():
            wb_ref[...] = w_ref[...].astype(BF16)

        w = wb_ref[...]
    else:
        w = w_ref[...]
    acc = jnp.dot(x_ref[...].astype(BF16), w, preferred_element_type=F32)
    if has_bias:
        acc = acc + b_ref[...]
    o_ref[...] = acc.astype(o_ref.dtype)


def matmul(x, w, *, bm, bn, out_dtype, name, bias=None, layer=None, n=None):
    m, k = x.shape
    n = w.shape[-1] if n is None else n
    assert m % bm == 0 and n % bn == 0, (m, bm, n, bn)
    if layer is None:
        w_spec = pl.BlockSpec((k, bn), lambda j, i: (0, j))
    else:
        w_spec = pl.BlockSpec((None, k, bn), lambda j, i: (layer, 0, j))
    in_specs = [pl.BlockSpec((bm, k), lambda j, i: (i, 0)), w_spec]
    args = [x, w]
    if bias is not None:
        in_specs.append(pl.BlockSpec((1, bn), lambda j, i: (0, j)))
        args.append(bias.reshape(1, n))
    cast_w = w.dtype != BF16
    return pl.pallas_call(
        functools.partial(_mm_kernel, has_bias=bias is not None, cast_w=cast_w),
        grid=(n // bn, m // bm),
        in_specs=in_specs,
        out_specs=pl.BlockSpec((bm, bn), lambda j, i: (i, j)),
        out_shape=jax.ShapeDtypeStruct((m, n), out_dtype),
        scratch_shapes=[pltpu.VMEM((k, bn), BF16)] if cast_w else [],
        compiler_params=_cparams(("parallel", "arbitrary" if cast_w else "parallel")),
        name=name,
    )(*args)


def _swiglu(g, u):
    return g * jax.nn.sigmoid(g) * u


def _mm_stream_kernel(*refs, n_w, n_x, nj, mi, kc):
    x_refs, w_refs = refs[:n_x], refs[n_x:n_x + n_w]
    o_refs, wb_refs = refs[n_x + n_w:2 * n_x + n_w], refs[2 * n_x + n_w:]
    j, i = pl.program_id(0), pl.program_id(1)
    slot = j % 2

    @pl.when((j < nj) & (i < mi))
    def _():
        r0 = pl.multiple_of(i * kc, 16)
        for w_ref, wb_ref in zip(w_refs, wb_refs):
            wb_ref[slot, pl.ds(r0, kc), :] = w_ref[...].astype(BF16)

    def product(x_ref, o_ref):
        x = x_ref[...]
        accs = [jnp.dot(x, wb_ref[1 - slot], preferred_element_type=F32) for wb_ref in wb_refs]
        out = _swiglu(*accs) if n_w == 2 else accs[0]
        o_ref[...] = out.astype(o_ref.dtype)

    @pl.when((j > 0) & (i < mi))
    def _():
        product(x_refs[0], o_refs[0])

    if n_x == 2:
        @pl.when((j > 0) & (i == mi))
        def _():
            product(x_refs[1], o_refs[1])


def matmul_stream(x, ws, layer, *, bm, bn, out_dtype, name, x2=None, n=None):
    m, k = x.shape
    n = ws[0].shape[-1] if n is None else n
    mi, nj = m // bm, n // bn
    kc = k // mi
    assert m % bm == 0 and n % bn == 0 and k % mi == 0 and kc % 16 == 0 and mi > 1, (m, bm, n, bn, k)
    n_x = 1 if x2 is None else 2
    row = lambda j, i: jnp.where(j == 0, 0, jnp.minimum(i, mi - 1))
    col = lambda j: jnp.maximum(j - 1, 0)
    w_spec = pl.BlockSpec(
        (None, kc, bn),
        lambda j, i: (layer, jnp.where(j < nj, jnp.minimum(i, mi - 1), mi - 1), jnp.minimum(j, nj - 1)))
    in_specs = [pl.BlockSpec((bm, k), lambda j, i: (row(j, i), 0))]
    out_specs = [pl.BlockSpec((bm, bn), lambda j, i: (row(j, i), col(j)))]
    out_shape = [jax.ShapeDtypeStruct((m, n), out_dtype)]
    args = [x]
    if x2 is not None:
        m2 = x2.shape[0]
        assert m2 <= bm and x2.shape[1] == k
        in_specs.append(pl.BlockSpec((m2, k), lambda j, i: (0, 0)))
        out_specs.append(pl.BlockSpec((m2, bn), lambda j, i: (0, col(j))))
        out_shape.append(jax.ShapeDtypeStruct((m2, n), out_dtype))
        args.append(x2)
    outs = pl.pallas_call(
        functools.partial(_mm_stream_kernel, n_w=len(ws), n_x=n_x, nj=nj, mi=mi, kc=kc),
        grid=(nj + 1, mi + n_x - 1),
        in_specs=in_specs + [w_spec] * len(ws),
        out_specs=out_specs,
        out_shape=out_shape,
        scratch_shapes=[pltpu.VMEM((2, k, bn), BF16) for _ in ws],
        compiler_params=_cparams(("arbitrary", "arbitrary")),
        name=name,
    )(*args, *ws)
    return outs[0] if x2 is None else (outs[0], outs[1])


def _swiglu_kernel(x_ref, wg_ref, wu_ref, o_ref):
    x = x_ref[...]
    g = jnp.dot(x, wg_ref[...].astype(BF16), preferred_element_type=F32)
    u = jnp.dot(x, wu_ref[...].astype(BF16), preferred_element_type=F32)
    o_ref[...] = _swiglu(g, u).astype(o_ref.dtype)


def swiglu_single(x, wg, wu, layer, *, bn, name):
    m, k = x.shape
    n = wg.shape[-1]
    assert n % bn == 0
    w_spec = pl.BlockSpec((None, k, bn), lambda j: (layer, 0, j))
    return pl.pallas_call(
        _swiglu_kernel,
        grid=(n // bn,),
        in_specs=[pl.BlockSpec((m, k), lambda j: (0, 0)), w_spec, w_spec],
        out_specs=pl.BlockSpec((m, bn), lambda j: (0, j)),
        out_shape=jax.ShapeDtypeStruct((m, n), BF16),
        compiler_params=_cparams(("parallel",)),
        name=name,
    )(x, wg, wu)


def dense2(x, x2, w, layer, *, out_dtype, name, bm=1024, bn=1024, n=None):
    if x.shape[0] <= bm:
        ncol = w.shape[-1] if n is None else n
        one = lambda a, nm: matmul(a, w, bm=a.shape[0], bn=256, out_dtype=out_dtype, layer=layer, name=nm,
                                   n=ncol)
        return one(x, name), (None if x2 is None else one(x2, name + "_ctx"))
    y = matmul_stream(x, (w,), layer, bm=bm, bn=bn, out_dtype=out_dtype, name=name, x2=x2, n=n)
    return (y, None) if x2 is None else y


def dense(x, w, layer, *, out_dtype, name, bm=1024, bn=1024):
    m = x.shape[0]
    if m > bm:
        return matmul_stream(x, (w,), layer, bm=bm, bn=bn, out_dtype=out_dtype, name=name)
    return matmul(x, w, bm=m, bn=256, out_dtype=out_dtype, layer=layer, name=name)


def _rms(x):
    return x * lax.rsqrt(jnp.mean(x * x, axis=-1, keepdims=True) + EPS)


def _resid_norm_kernel(*refs, has_y, want_a):
    it = iter(refs)
    h_ref = next(it)
    if has_y:
        y_ref, gate_ref, g1_ref = next(it), next(it), next(it)
    if want_a:
        g2_ref, shift_ref, scale_ref = next(it), next(it), next(it)
    if has_y:
        hout_ref = next(it)
    if want_a:
        a_ref = next(it)
    h = h_ref[0]
    if has_y:
        h = h + gate_ref[0] * (_rms(y_ref[0].astype(F32)) * g1_ref[...])
        hout_ref[0] = h
    if want_a:
        a = (_rms(h) * g2_ref[...]) * (1.0 + scale_ref[0]) + shift_ref[0]
        a_ref[0] = a.astype(a_ref.dtype)


def resid_norm(h, y=None, gate=None, g1=None, g2=None, shift=None, scale=None, *, bt=128):
    b, t, d = h.shape
    has_y, want_a = y is not None, g2 is not None
    bt = min(bt, t)
    tok = pl.BlockSpec((1, bt, d), lambda i, j: (i, j, 0))
    per_b = pl.BlockSpec((1, 1, d), lambda i, j: (i, 0, 0))
    vec = pl.BlockSpec((1, d), lambda i, j: (0, 0))
    args, in_specs, out_shape, out_specs = [h], [tok], [], []
    if has_y:
        args += [y, gate, g1.reshape(1, d)]
        in_specs += [tok, per_b, vec]
        out_shape.append(jax.ShapeDtypeStruct((b, t, d), F32))
        out_specs.append(tok)
    if want_a:
        args += [g2.reshape(1, d), shift, scale]
        in_specs += [vec, per_b, per_b]
        out_shape.append(jax.ShapeDtypeStruct((b, t, d), BF16))
        out_specs.append(tok)
    outs = pl.pallas_call(
        functools.partial(_resid_norm_kernel, has_y=has_y, want_a=want_a),
        grid=(b, t // bt),
        in_specs=in_specs,
        out_specs=out_specs,
        out_shape=out_shape,
        compiler_params=_cparams(("parallel", "parallel")),
        name="resid_norm",
    )(*args)
    outs = list(outs)
    h_new = outs.pop(0) if has_y else h
    a = outs.pop(0) if want_a else None
    return h_new, a


def _softplus(x):
    return jnp.maximum(x, 0.0) + jnp.log1p(jnp.exp(-jnp.abs(x)))


def _log_sigmoid(x):
    return jnp.minimum(x, 0.0) - jnp.log1p(jnp.exp(-jnp.abs(x)))


def _chunk_index(c, nc, reverse):
    return (nc - 1 - c) if reverse else c


def _mlstm_kernel(*refs):
    ins, outs = refs[:14], refs[14:]

    @pl.when(pl.program_id(2) == 0)
    def _():
        for d in range(2):
            for s in range(3):
                outs[4 * d + 1 + s][...] = ins[7 * d + 4 + s][...]

    gates = [_mlstm_gate_forms(ins[7 * d + 3], reverse=d == 1) for d in range(2)]
    for hh in range(ML_HEADS_PER_STEP):
        for d in range(2):
            _mlstm_chunk(*ins[7 * d:7 * d + 3], gates[d], *outs[4 * d:4 * d + 4], hh=hh, reverse=d == 1)


def _mlstm_gate_forms(gr_ref, *, reverse):
    l = ML_CHUNK
    nr = 2 * ML_HEADS_PER_STEP
    row = lax.broadcasted_iota(jnp.int32, (l, l), 0)
    col = lax.broadcasted_iota(jnp.int32, (l, l), 1)
    mask_t = (col <= row) if reverse else (col >= row)
    is_forget = (lax.broadcasted_iota(jnp.int32, (nr, l), 0) & 1) == 1
    gcap = GATE_CAP * jnp.tanh(gr_ref[0, 0, 0] / GATE_CAP)
    rows = jnp.where(is_forget, _log_sigmoid(gcap), gcap)
    cums = _dot3((((1,), (0,)), ((), ())), _split3(rows), mask_t.astype(BF16))
    mixed = jnp.where(is_forget, cums, gcap)
    bc = _dot3((((0,), (0,)), ((), ())), _split3(mixed), _head_selector(nr, l))
    return gcap, cums, bc


def _mlstm_chunk(q_ref, k_ref, v_ref, gates, y_ref, c_ref, n_ref, m_ref, *, hh, reverse):
    qsl = slice(hh * ML_QK_DIM, (hh + 1) * ML_QK_DIM)
    vsl = slice(hh * ML_V_DIM, (hh + 1) * ML_V_DIM)
    l = ML_CHUNK
    row = lax.broadcasted_iota(jnp.int32, (l, l), 0)
    col = lax.broadcasted_iota(jnp.int32, (l, l), 1)
    mask = (col >= row) if reverse else (col <= row)

    gcap, cums, bc = gates
    ig_r = gcap[2 * hh:2 * hh + 1, :]
    b_r = cums[2 * hh + 1:2 * hh + 2, :]
    end_col = 0 if reverse else l - 1
    total = b_r[:, end_col:end_col + 1]
    ig_bc = bc[:, 2 * hh * l:(2 * hh + 1) * l]
    b_bc = bc[:, (2 * hh + 1) * l:(2 * hh + 2) * l]
    ig_c, b_c = ig_bc[:, 0:1], b_bc[:, 0:1]

    m_prev = m_ref[0, hh]
    logw = jnp.where(mask, b_bc - b_r + ig_r, -jnp.inf)
    inter = b_c + m_prev
    m_t = jnp.maximum(jnp.max(logw, axis=1, keepdims=True), inter)

    q = q_ref[0, :, qsl]
    k = k_ref[0, :, qsl]
    v = v_ref[0, :, vsl]
    scale = ML_QK_DIM ** -0.5
    qk = lax.dot_general(q, k, (((1,), (1,)), ((), ())), preferred_element_type=F32)
    s = qk * scale * jnp.exp(logw - m_t)
    inter_w = jnp.exp(inter - m_t) * scale

    c_st = c_ref[0, hh]
    n_st = n_ref[0, hh]
    num = (jnp.dot(s.astype(BF16), v, preferred_element_type=F32)
           + inter_w * jnp.dot(q, c_st.astype(BF16), preferred_element_type=F32))
    qn = jnp.sum(q.astype(F32) * n_st, axis=1, keepdims=True)
    den = jnp.sum(s, axis=1, keepdims=True) + inter_w * qn
    y_ref[0, :, vsl] = (num * (1.0 / jnp.maximum(jnp.abs(den), jnp.exp(-m_t)))).astype(y_ref.dtype)

    g_c = total - b_c + ig_c
    g_r = total - b_r + ig_r
    m_new = jnp.maximum(total + m_prev, jnp.max(g_r, axis=1, keepdims=True))
    w_c = jnp.exp(g_c - m_new)
    decay = jnp.exp(total + m_prev - m_new)
    kw = k.astype(F32) * w_c
    kv = lax.dot_general(kw.astype(BF16), v, (((0,), (0,)), ((), ())), preferred_element_type=F32)
    c_ref[0, hh] = decay * c_st + kv
    n_ref[0, hh] = decay * n_st + jnp.sum(kw, axis=0, keepdims=True)
    m_ref[0, hh] = m_new


def mlstm_scan(proj, gates_row, states, *, name):
    b, t, _ = proj.shape
    l = ML_CHUNK
    nc = t // l
    hp = ML_HEADS_PER_STEP
    gates_step = gates_row.reshape(2, b, ML_HEADS // hp, 2 * hp, t)
    qw, vw = hp * ML_QK_DIM, hp * ML_V_DIM
    qb, kb, vb = E_Q // qw, E_K // qw, E_V // vw
    st_specs = [pl.BlockSpec((1, hp, ML_QK_DIM, ML_V_DIM), lambda i, h, c: (i, h, 0, 0)),
                pl.BlockSpec((1, hp, 1, ML_QK_DIM), lambda i, h, c: (i, h, 0, 0)),
                pl.BlockSpec((1, hp, 1, 1), lambda i, h, c: (i, h, 0, 0))]
    in_specs, args, out_specs, out_shape = [], [], [], []
    for d in range(2):
        ci = functools.partial(_chunk_index, nc=nc, reverse=d == 1)
        in_specs += [
            pl.BlockSpec((1, l, qw), lambda i, h, c, ci=ci: (i, ci(c), qb + h)),
            pl.BlockSpec((1, l, qw), lambda i, h, c, ci=ci: (i, ci(c), kb + h)),
            pl.BlockSpec((1, l, vw), lambda i, h, c, ci=ci: (i, ci(c), vb + h)),
            pl.BlockSpec((1, 1, 1, 2 * hp, l), lambda i, h, c, ci=ci, d=d: (d, i, h, 0, ci(c))),
        ] + st_specs
        args += [proj, proj, proj, gates_step, *states[d]]
        out_specs += [pl.BlockSpec((1, l, vw), lambda i, h, c, ci=ci: (i, ci(c), h))] + st_specs
        out_shape += [jax.ShapeDtypeStruct((b, t, ML_V), BF16)]
        out_shape += [jax.ShapeDtypeStruct(s.shape, F32) for s in states[d]]
    outs = pl.pallas_call(
        _mlstm_kernel,
        grid=(b, ML_HEADS // hp, nc),
        in_specs=in_specs,
        out_specs=out_specs,
        out_shape=out_shape,
        compiler_params=_cparams(("parallel", "parallel", "arbitrary")),
        name=name,
    )(*args)
    return (outs[0], outs[4]), (tuple(outs[1:4]), tuple(outs[5:8]))


CONV_ROWS = 256
CONV_HALO = 16


def _conv_kernel(x_ref, w_ref, b_ref, o_ref, *, t):
    w = w_ref[...]
    bias = b_ref[...]
    r = min(CONV_ROWS, t)
    half = SSM_CONV // 2
    cw = x_ref.shape[-1]
    for r0 in range(0, t, r):
        parts = []
        if r0 == 0:
            parts.append(jnp.zeros((CONV_HALO, cw), F32))
        else:
            parts.append(x_ref[0, pl.ds(r0 - CONV_HALO, CONV_HALO), :].astype(F32))
        parts.append(x_ref[0, pl.ds(r0, r), :].astype(F32))
        if r0 + r == t:
            parts.append(jnp.zeros((CONV_HALO, cw), F32))
        else:
            parts.append(x_ref[0, pl.ds(r0 + r, CONV_HALO), :].astype(F32))
        win = jnp.concatenate(parts, axis=0)
        n = r + 2 * CONV_HALO
        acc = jnp.zeros((r, cw), F32) + bias
        for kk in range(SSM_CONV):
            sh = (half - kk) % n
            rolled = win if sh == 0 else pltpu.roll(win, sh, 0)
            acc = acc + w[kk:kk + 1, :] * rolled[CONV_HALO:CONV_HALO + r, :]
        o_ref[0, pl.ds(r0, r), :] = (acc * jax.nn.sigmoid(acc)).astype(o_ref.dtype)


def conv_silu(proj, conv_w, conv_b, *, bc=512):
    b, t, _ = proj.shape
    c = SSM_CONV_DIM
    off = E_XBC // bc
    return pl.pallas_call(
        functools.partial(_conv_kernel, t=t),
        grid=(b, c // bc),
        in_specs=[pl.BlockSpec((1, t, bc), lambda i, j: (i, 0, off + j)),
                  pl.BlockSpec((SSM_CONV, bc), lambda i, j: (0, j)),
                  pl.BlockSpec((1, bc), lambda i, j: (0, j))],
        out_specs=pl.BlockSpec((1, t, bc), lambda i, j: (i, 0, j)),
        out_shape=jax.ShapeDtypeStruct((b, t, c), BF16),
        compiler_params=_cparams(("parallel", "parallel")),
        name="conv_silu",
    )(proj, conv_w, conv_b.reshape(1, c))


def _split3(x):
    hi = x.astype(BF16)
    r1 = x - hi.astype(F32)
    mid = r1.astype(BF16)
    lo = (r1 - mid.astype(F32)).astype(BF16)
    return hi, mid, lo


def _dot3(dims, parts_lhs, rhs):
    return sum(lax.dot_general(p, rhs, dims, preferred_element_type=F32) for p in parts_lhs)


def _head_selector(heads, width):
    rows = lax.broadcasted_iota(jnp.int32, (heads, heads * width), 0)
    lanes = lax.broadcasted_iota(jnp.int32, (heads, heads * width), 1)
    return (lanes // width == rows).astype(BF16)


def _ssd_kernel(*refs):
    ins, outs = refs[:12], refs[12:]

    @pl.when(pl.program_id(2) == 0)
    def _():
        for d in range(2):
            outs[2 * d + 1][...] = ins[6 * d + 5][...]

    for d in range(2):
        _ssd_chunk(*ins[6 * d:6 * d + 5], *outs[2 * d:2 * d + 2], reverse=d == 1)


def _ssd_chunk(xs_ref, bm_ref, cm_ref, dtr_ref, pr_ref, y_ref, h_ref, *, reverse):
    l = SSM_CHUNK
    gp = SSD_GROUPS_PER_STEP
    nh = gp * SSM_HPG
    row = lax.broadcasted_iota(jnp.int32, (l, l), 0)
    col = lax.broadcasted_iota(jnp.int32, (l, l), 1)
    mask = (col >= row) if reverse else (col <= row)
    mask_t = (col <= row) if reverse else (col >= row)

    nn = (((1,), (0,)), ((), ()))
    nt = (((1,), (1,)), ((), ()))
    tn = (((0,), (0,)), ((), ()))
    pr = pr_ref[0, 0]
    dt_r = _softplus(dtr_ref[0, 0, 0] + pr[:, 0:1])
    a_r = dt_r * pr[:, 1:2]
    cum_r = _dot3(nn, _split3(a_r), mask_t.astype(BF16))
    dt_parts, cum_parts = _split3(dt_r), _split3(cum_r)
    sel_ch = _head_selector(nh, SSM_HEAD_DIM)
    dt_ch = _dot3(tn, dt_parts, sel_ch)
    cum_ch = _dot3(tn, cum_parts, sel_ch)
    cum_bc = _dot3(tn, cum_parts, _head_selector(nh, l))
    end_row = 0 if reverse else l - 1
    total_ch = cum_ch[end_row:end_row + 1, :]

    xdt = xs_ref[0].astype(F32) * dt_ch
    e_cum = jnp.exp(cum_ch)
    xw = (xdt * jnp.exp(total_ch - cum_ch)).astype(BF16)
    s_decay = jnp.exp(total_ch)
    lane_head = lax.broadcasted_iota(jnp.int32, (l, SSM_GROUP_W), 1) // SSM_HEAD_DIM
    for g in range(gp):
        xsl = slice(g * SSM_GROUP_W, (g + 1) * SSM_GROUP_W)
        nsl = slice(g * SSM_STATE, (g + 1) * SSM_STATE)
        bm = bm_ref[0, :, nsl]
        cm = cm_ref[0, :, nsl]
        hst = h_ref[0, g]
        cb = lax.dot_general(cm, bm, nt, preferred_element_type=F32)
        inter = jnp.dot(cm, hst.astype(BF16), preferred_element_type=F32)
        xdt_g = xdt[:, xsl]
        ws, xjs = [], []
        for j in range(SSM_HPG):
            hd = g * SSM_HPG + j
            decay = jnp.exp(jnp.where(mask, cum_bc[:, hd * l:(hd + 1) * l] - cum_r[hd:hd + 1, :], -jnp.inf))
            ws.append((decay * cb).astype(BF16))
            xjs.append(jnp.where(lane_head == j, xdt_g, 0.0).astype(BF16))
        y = jnp.dot(jnp.concatenate(ws, axis=1), jnp.concatenate(xjs, axis=0), preferred_element_type=F32)
        y_ref[0, :, xsl] = (y + e_cum[:, xsl] * inter).astype(y_ref.dtype)
        upd = lax.dot_general(bm, xw[:, xsl], tn, preferred_element_type=F32)
        h_ref[0, g] = s_decay[:, xsl] * hst + upd


def ssd_scan(xbc, dt_row, par_row, states, *, name):
    b, t, _ = xbc.shape
    l = SSM_CHUNK
    nc = t // l
    gp = SSD_GROUPS_PER_STEP
    xw, nw = gp * SSM_GROUP_W, gp * SSM_STATE
    bmb = SSM_INNER // nw
    cmb = bmb + SSM_GROUPS // gp
    st_spec = pl.BlockSpec((1, gp, SSM_STATE, SSM_GROUP_W), lambda i, g, c: (i, g, 0, 0))
    dt_step = dt_row.reshape(2, b, SSM_GROUPS // gp, gp * SSM_HPG, t)
    par_step = par_row.reshape(2, SSM_GROUPS // gp, gp * SSM_HPG, 2)
    in_specs, args, out_specs, out_shape = [], [], [], []
    for d in range(2):
        ci = functools.partial(_chunk_index, nc=nc, reverse=d == 1)
        in_specs += [
            pl.BlockSpec((1, l, xw), lambda i, g, c, ci=ci: (i, ci(c), g)),
            pl.BlockSpec((1, l, nw), lambda i, g, c, ci=ci: (i, ci(c), bmb + g)),
            pl.BlockSpec((1, l, nw), lambda i, g, c, ci=ci: (i, ci(c), cmb + g)),
            pl.BlockSpec((1, 1, 1, gp * SSM_HPG, l), lambda i, g, c, ci=ci, d=d: (d, i, g, 0, ci(c))),
            pl.BlockSpec((1, 1, gp * SSM_HPG, 2), lambda i, g, c, d=d: (d, g, 0, 0)),
            st_spec,
        ]
        args += [xbc, xbc, xbc, dt_step, par_step, states[d]]
        out_specs += [pl.BlockSpec((1, l, xw), lambda i, g, c, ci=ci: (i, ci(c), g)), st_spec]
        out_shape += [jax.ShapeDtypeStruct((b, t, SSM_INNER), BF16),
                      jax.ShapeDtypeStruct(states[d].shape, F32)]
    yf, hf, yb, hb = pl.pallas_call(
        _ssd_kernel,
        grid=(b, SSM_GROUPS // gp, nc),
        in_specs=in_specs,
        out_specs=out_specs,
        out_shape=out_shape,
        compiler_params=_cparams(("parallel", "parallel", "arbitrary")),
        name=name,
    )(*args)
    return (yf, yb), (hf, hb)


def _even_out_kernel(mlf_ref, mlb_ref, ssdf_ref, ssdb_ref, o_ref, z_ref, xs_ref, mlg_ref, dsk_ref, ssg_ref,
                     out_ref):
    for h in range(ML_HEADS):
        sl = slice(h * ML_V_DIM, (h + 1) * ML_V_DIM)
        y = mlf_ref[0, :, sl].astype(F32) + mlb_ref[0, :, sl].astype(F32)
        og = o_ref[0, :, sl].astype(F32)
        out_ref[0, :, sl] = (_rms(y) * mlg_ref[:, sl] * jax.nn.sigmoid(og)).astype(out_ref.dtype)
    for g in range(SSM_GROUPS):
        sl = slice(g * SSM_GROUP_W, (g + 1) * SSM_GROUP_W)
        z = z_ref[0, :, sl].astype(F32)
        y = ((ssdf_ref[0, :, sl].astype(F32) + ssdb_ref[0, :, sl].astype(F32))
             + dsk_ref[:, sl] * xs_ref[0, :, sl].astype(F32))
        y = y * (z * jax.nn.sigmoid(z))
        so = slice(ML_V + g * SSM_GROUP_W, ML_V + (g + 1) * SSM_GROUP_W)
        out_ref[0, :, so] = (_rms(y) * ssg_ref[:, sl]).astype(out_ref.dtype)


def even_out(ml_y, ssd_y, proj_a, proj_b, xbc, ml_norm_g, d_skip_full, ssm_norm_g, *, bt=128):
    b, t, _ = ml_y[0].shape
    bt = min(bt, t)
    w = ML_V
    tok = lambda blk: pl.BlockSpec((1, bt, w), lambda i, j: (i, j, blk))
    vec = pl.BlockSpec((1, w), lambda i, j: (0, 0))
    return pl.pallas_call(
        _even_out_kernel,
        grid=(b, t // bt),
        in_specs=[tok(0), tok(0), tok(0), tok(0), tok(E_O // w), tok(E_Z // w), tok(0), vec, vec, vec],
        out_specs=pl.BlockSpec((1, bt, 2 * w), lambda i, j: (i, j, 0)),
        out_shape=jax.ShapeDtypeStruct((b, t, 2 * w), BF16),
        compiler_params=_cparams(("parallel", "parallel")),
        name="even_gate",
    )(ml_y[0], ml_y[1], ssd_y[0], ssd_y[1], proj_a, proj_b, xbc, ml_norm_g.reshape(1, w), d_skip_full.reshape(1, w),
      ssm_norm_g.reshape(1, w))


def _hg_constants(reverse):
    l = HG_CHUNK
    pos = np.arange(l)
    tri = (pos[None, :] <= pos[:, None]).astype(np.float32)
    masks = []
    for lv in range(HG_LEVELS):
        m = 1 << lv
        blk = pos // (2 * m)
        later = (pos % (2 * m)) >= m
        masks.append(((blk[:, None] == blk[None, :]) & later[:, None] & (~later)[None, :]).astype(np.float32))
    masks = np.stack(masks)
    if reverse:
        tri = tri[::-1, ::-1]
        masks = masks[:, ::-1, ::-1]
    return (jnp.asarray(tri, BF16), jnp.asarray(masks.reshape(-1, l), F32))


def _hg_level_factors(q, k, f, cum, reverse):
    l, w = q.shape
    row = lax.broadcasted_iota(jnp.int32, (l, w), 0)
    sub = lax.broadcasted_iota(jnp.int32, (8, w), 0)
    zs = []
    for lv in range(HG_LEVELS):
        m = 1 << lv
        late = ((row & m) == 0) if reverse else ((row & m) != 0)
        if m >= 8:
            pieces = []
            for b0 in range(0, l, 2 * m):
                lo_rows, hi_rows = slice(b0, b0 + m), slice(b0 + m, b0 + 2 * m)
                if reverse:
                    mid = cum[b0 + m:b0 + m + 1]
                    pieces.append(q[lo_rows] * jnp.exp(cum[lo_rows] - mid))
                    pieces.append(k[hi_rows] * jnp.exp(mid - cum[hi_rows]))
                else:
                    mid = cum[b0 + m - 1:b0 + m]
                    pieces.append(k[lo_rows] * jnp.exp(mid - cum[lo_rows]))
                    pieces.append(q[hi_rows] * jnp.exp(cum[hi_rows] - mid))
            zs.append(jnp.concatenate(pieces, axis=0).astype(BF16))
            continue
        if m == 1:
            ex = jnp.where(late, f, 1.0)
        else:
            pieces = []
            if m >= 4:
                for b0 in range(0, l, 2 * m):
                    r = b0 + (m if reverse else m - 1)
                    pieces.append(cum[b0:b0 + 2 * m] - cum[r:r + 1])
            else:
                for b0 in range(0, l, 8):
                    ra, rb = (b0 + 2, b0 + 6) if reverse else (b0 + 1, b0 + 5)
                    mid = jnp.where(sub < 4, cum[ra:ra + 1], cum[rb:rb + 1])
                    pieces.append(cum[b0:b0 + 8] - mid)
            dd = pieces[0] if len(pieces) == 1 else jnp.concatenate(pieces, axis=0)
            ex = jnp.exp(jnp.where(late, dd, -dd))
        zs.append((jnp.where(late, q, k) * ex).astype(BF16))
    return zs


def _hg_chunk(q_ref, f_ref, i_ref, lb, tri, masks_ref, y_ref, s_ref, r0, reverse):
    l, e = HG_CHUNK, HG_EXPAND
    q_raw = q_ref[0, pl.ds(r0, l), :].astype(F32)
    f_raw = f_ref[0, pl.ds(r0, l), :].astype(F32)
    v = i_ref[0, pl.ds(r0, l), :]
    f = lb + (1.0 - lb) * jax.nn.sigmoid(f_raw)
    k = 1.0 - f
    lf = jnp.log(f)
    q = q_raw * jax.nn.sigmoid(q_raw) * (e ** -0.5)
    hi, mid, lo = _split3(lf)
    cum = (jnp.dot(tri, hi, preferred_element_type=F32)
           + jnp.dot(tri, mid, preferred_element_type=F32)
           + jnp.dot(tri, lo, preferred_element_type=F32))
    zs = _hg_level_factors(q, k, f, cum, reverse)
    end_row = 0 if reverse else l - 1
    cum_end = cum[end_row:end_row + 1, :]
    qc = (q * jnp.exp(cum)).astype(BF16)
    kt_end = (k * jnp.exp(cum_end - cum)).astype(BF16)
    s_decay = jnp.exp(cum_end)
    qk = q * k
    row = lax.broadcasted_iota(jnp.int32, (l, l), 0)
    col = lax.broadcasted_iota(jnp.int32, (l, l), 1)
    eye = row == col
    nt = (((1,), (1,)), ((), ()))
    tn = (((0,), (0,)), ((), ()))
    for hh in range(HG_HEADS_PER_STEP):
        sl = slice(hh * e, (hh + 1) * e)
        att = jnp.where(eye, jnp.sum(qk[:, sl], axis=1, keepdims=True), 0.0)
        for lv in range(HG_LEVELS):
            z = zs[lv][:, sl]
            a_lv = lax.dot_general(z, z, nt, preferred_element_type=F32)
            att = att + masks_ref[pl.ds(lv * l, l), :] * a_lv
        st = s_ref[0, hh]
        vh = v[:, sl]
        o = (jnp.dot(att.astype(BF16), vh, preferred_element_type=F32)
             + lax.dot_general(qc[:, sl], st.astype(BF16), nt, preferred_element_type=F32))
        y_ref[0, pl.ds(r0, l), sl] = o.astype(y_ref.dtype)
        upd = lax.dot_general(vh, kt_end[:, sl], tn, preferred_element_type=F32)
        s_ref[0, hh] = s_decay[:, sl] * st + upd


def _hg_kernel(qf_ref, ff_ref, if_ref, qb_ref, fb_ref, ib_ref, lbf_ref, lbb_ref, trif_ref, trib_ref,
               mf_ref, mb_ref, sf0_ref, sb0_ref, yf_ref, yb_ref, sf_ref, sb_ref, *, n_inner):
    l = HG_CHUNK

    @pl.when(pl.program_id(2) == 0)
    def _():
        sf_ref[...] = sf0_ref[...]
        sb_ref[...] = sb0_ref[...]

    lbf = lbf_ref[0]
    lbb = lbb_ref[0]
    trif = trif_ref[...]
    trib = trib_ref[...]

    def chunk(ci, carry):
        rf = pl.multiple_of(ci * l, l)
        rb = pl.multiple_of((n_inner - 1 - ci) * l, l)
        _hg_chunk(qf_ref, ff_ref, if_ref, lbf, trif, mf_ref, yf_ref, sf_ref, rf, False)
        _hg_chunk(qb_ref, fb_ref, ib_ref, lbb, trib, mb_ref, yb_ref, sb_ref, rb, True)
        return carry

    lax.fori_loop(0, n_inner, chunk, 0, unroll=4)


def hgrn2_scan(proj, lb, states, *, tb=512, name):
    b, t, _ = proj.shape
    tb = min(tb, t)
    nb = t // tb
    n_inner = tb // HG_CHUNK
    hw = HG_HEADS_PER_STEP * HG_EXPAND
    nh = HG_HEADS // HG_HEADS_PER_STEP
    qb_, ffb, fbb, ib_ = O_Q // hw, O_F // hw, (O_F + HG_F) // hw, O_I // hw
    trif, mf = _hg_constants(False)
    trib, mb = _hg_constants(True)
    fwd = lambda blk: pl.BlockSpec((1, tb, hw), lambda i, h, c: (i, c, blk + h))
    bwd = lambda blk: pl.BlockSpec((1, tb, hw), lambda i, h, c: (i, nb - 1 - c, blk + h))
    const = lambda a: pl.BlockSpec(a.shape, lambda i, h, c: (0,) * a.ndim)
    st_spec = pl.BlockSpec((1, HG_HEADS_PER_STEP, HG_V_DIM, HG_EXPAND), lambda i, h, c: (i, h, 0, 0))
    in_specs = [fwd(qb_), fwd(ffb), fwd(ib_), bwd(qb_), bwd(fbb), bwd(ib_),
                pl.BlockSpec((1, 1, hw), lambda i, h, c: (0, 0, h)),
                pl.BlockSpec((1, 1, hw), lambda i, h, c: (1, 0, h)),
                const(trif), const(trib), const(mf), const(mb), st_spec, st_spec]
    yf, yb, sf, sb = pl.pallas_call(
        functools.partial(_hg_kernel, n_inner=n_inner),
        grid=(b, nh, nb),
        in_specs=in_specs,
        out_specs=[fwd(0), bwd(0), st_spec, st_spec],
        out_shape=[jax.ShapeDtypeStruct((b, t, D_MODEL), BF16),
                   jax.ShapeDtypeStruct((b, t, D_MODEL), BF16),
                   jax.ShapeDtypeStruct(states[0].shape, F32),
                   jax.ShapeDtypeStruct(states[1].shape, F32)],
        compiler_params=_cparams(("parallel", "parallel", "arbitrary")),
        name=name,
    )(proj, proj, proj, proj, proj, proj, lb, lb, trif, trib, mf, mb, states[0], states[1])
    return (yf, yb), (sf, sb)


def _hg_out_kernel(of_ref, ob_ref, g_ref, ng_ref, out_ref):
    for h in range(of_ref.shape[-1] // HG_V_DIM):
        sl = slice(h * HG_V_DIM, (h + 1) * HG_V_DIM)
        g = g_ref[0, :, sl].astype(F32)
        o = of_ref[0, :, sl].astype(F32) + ob_ref[0, :, sl].astype(F32)
        out_ref[0, :, sl] = (_rms(o) * ng_ref[:, sl] * (g * jax.nn.sigmoid(g))).astype(out_ref.dtype)


def hg_out(o_f, o_b, proj, norm_g, *, bt=256, bc=1024, name):
    b, t, d = o_f.shape
    bt = min(bt, t)
    gb = O_G // bc
    tok = pl.BlockSpec((1, bt, bc), lambda i, j, c: (i, j, c))
    return pl.pallas_call(
        _hg_out_kernel,
        grid=(b, t // bt, d // bc),
        in_specs=[tok, tok,
                  pl.BlockSpec((1, bt, bc), lambda i, j, c: (i, j, gb + c)),
                  pl.BlockSpec((1, bc), lambda i, j, c: (0, c))],
        out_specs=tok,
        out_shape=jax.ShapeDtypeStruct((b, t, d), BF16),
        compiler_params=_cparams(("parallel", "parallel", "parallel")),
        name=name,
    )(o_f, o_b, proj, norm_g.reshape(1, d))


def even_mixer(a_c, a_l, w_main, w_small, w_out, layer, ml_gate_b, ml_norm_g, conv_w, conv_b,
               dt_bias, a_log, d_skip, ssm_norm_g, need_ctx):
    bsz = a_l.shape[0]

    def features(a, tag):
        b, t, d = a.shape
        bm = min(1024, b * t)
        a2 = a.reshape(b * t, d)
        proj = matmul(a2, w_main, bm=bm, bn=1024, out_dtype=BF16, name="ab_in_" + tag).reshape(b, t, E_MAIN)
        proj_a = proj
        small = matmul(a2, w_small, bm=bm, bn=128, out_dtype=F32, name="ab_in_small_" + tag).reshape(b, t, 128)
        gates = small[..., :16].reshape(b, t, 2, 2, ML_HEADS) + ml_gate_b.astype(F32)
        g_row = gates.transpose(2, 0, 4, 3, 1)
        dt = small[..., 16:16 + 2 * SSM_HEADS].reshape(b, t, 2, SSM_GROUPS, SSM_HPG)
        dt_row = dt.transpose(2, 0, 3, 4, 1)
        xbc = conv_silu(proj, conv_w, conv_b)
        return dict(proj_a=proj_a, proj=proj, xbc=xbc, g_row=g_row, dt_row=dt_row)

    fc, fl = features(a_c, "ctx"), features(a_l, "lat")
    neg_a = -jnp.exp(a_log.astype(F32))
    par = jnp.stack([dt_bias.astype(F32), neg_a], axis=1).reshape(2, 2, SSM_GROUPS, SSM_HPG)
    par_row = par.transpose(0, 2, 3, 1)

    ml_init = (jnp.zeros((bsz, ML_HEADS, ML_QK_DIM, ML_V_DIM), F32),
               jnp.zeros((bsz, ML_HEADS, 1, ML_QK_DIM), F32),
               jnp.zeros((bsz, ML_HEADS, 1, 1), F32))
    ssd_init = jnp.zeros((bsz, SSM_GROUPS, SSM_STATE, SSM_GROUP_W), F32)

    def ml(f, st, tag):
        return mlstm_scan(f['proj_a'], f['g_row'], st, name="mlstm_scan_" + tag)

    def ssd(f, st, tag):
        return ssd_scan(f['xbc'], f['dt_row'], par_row, st, name="ssd_scan_" + tag)

    ml_c, ml_st = ml(fc, (ml_init, ml_init), "ctx")
    ml_l, _ = ml(fl, ml_st, "lat")
    ssd_c, ssd_st = ssd(fc, (ssd_init, ssd_init), "ctx")
    ssd_l, _ = ssd(fl, ssd_st, "lat")
    d_full = jnp.repeat(d_skip.astype(F32), SSM_HEAD_DIM)

    def gated(f, ml_y, ssd_y):
        b, t, _ = ml_y[0].shape
        cat = even_out(ml_y, ssd_y, f['proj_a'], f['proj'], f['xbc'], ml_norm_g, d_full, ssm_norm_g)
        return cat.reshape(b * t, D_MODEL)

    cat_c = gated(fc, ml_c, ssd_c) if need_ctx else None
    y_l, y_c = dense2(gated(fl, ml_l, ssd_l), cat_c, w_out, layer, out_dtype=BF16, name="ab_out")
    y_l = y_l.reshape(a_l.shape)
    return (y_c.reshape(a_c.shape) if need_ctx else None), y_l


def hgrn2_mixer(a_c, a_l, w_in, w_out, layer, lb, norm_g, need_ctx, lat_colmajor=True):
    bsz = a_l.shape[0]
    lb3 = lb.astype(F32).reshape(2, 1, HG_F)

    flat = lambda a: a.reshape(a.shape[0] * a.shape[1], a.shape[2])
    pl_, pc = dense2(flat(a_l), flat(a_c), w_in, layer, out_dtype=BF16, name="hg_in")
    pl_ = pl_.reshape(a_l.shape[:2] + (5 * D_MODEL,))
    pc = pc.reshape(a_c.shape[:2] + (5 * D_MODEL,))
    init = jnp.zeros((bsz, HG_HEADS, HG_V_DIM, HG_EXPAND), F32)
    o_c, st = hgrn2_scan(pc, lb3, (init, init), name="hg_scan_ctx")
    o_l, _ = hgrn2_scan(pl_, lb3, st, name="hg_scan_lat")

    def gated(p, o, colmajor, tag):
        g = hg_out(o[0], o[1], p, norm_g, name="hg_gate_" + tag)
        return flat(grid_from_colmajor(g) if colmajor else g)

    g_c = gated(pc, o_c, False, "ctx") if need_ctx else None
    y_l, y_c = dense2(gated(pl_, o_l, lat_colmajor, "lat"), g_c, w_out, layer, out_dtype=BF16, name="hg_out")
    return (y_c.reshape(a_c.shape) if need_ctx else None), y_l.reshape(a_l.shape)


def grid_to_colmajor(h):
    bsz, t, d = h.shape
    rows = t // GRID_W
    return h.reshape(bsz, rows, GRID_W, d).transpose(0, 2, 1, 3).reshape(bsz, t, d)


def grid_from_colmajor(h):
    bsz, t, d = h.shape
    rows = t // GRID_W
    return h.reshape(bsz, GRID_W, rows, d).transpose(0, 2, 1, 3).reshape(bsz, t, d)


def ffn(a_l, a_c, wg, wu, wd, layer):
    flat = lambda a: a.reshape(a.shape[0] * a.shape[1], a.shape[2])
    hid_l = matmul_stream(flat(a_l), (wg, wu), layer, bm=2048, bn=256, out_dtype=BF16, name="ffn_up_lat")
    f_l = dense(hid_l, wd, layer, out_dtype=BF16, name="ffn_down_lat", bm=512, bn=512).reshape(a_l.shape)
    if a_c is None:
        return f_l, None
    hid_c = swiglu_single(flat(a_c), wg, wu, layer, bn=256, name="ffn_up_ctx")
    f_c = dense(hid_c, wd, layer, out_dtype=BF16, name="ffn_down_ctx", bm=512, bn=512).reshape(a_c.shape)
    return f_l, f_c


def kernel(x, c, ctx, c_ctx, ada_w, ada_b, norm_g, ffn_w_gate, ffn_w_up, ffn_w_down,
           ab_w_in, ab_w_out, ml_gate_b, ml_norm_g, ssm_conv_w, ssm_conv_b, ssm_dt_bias,
           ssm_a_log, ssm_d, ssm_norm_g, hg_w_in, hg_w_out, hg_lb, hg_norm_g):
    depth = ada_w.shape[0]
    bsz = x.shape[0]
    d = D_MODEL
    lb_p = jax.nn.softmax(hg_lb.astype(F32), axis=1)
    lower_bounds = jnp.cumsum(lb_p, axis=1) - lb_p[:, :1]
    cond = jnp.concatenate([jax.nn.silu(c), jax.nn.silu(c_ctx)[None, :],
                            jnp.zeros((8 - bsz - 1, d), F32)], axis=0)
    h_lat, h_ctx = x, ctx
    y_l = y_c = f_l = f_c = None
    gate_l = gate_c = g_prev = None
    for layer in range(depth):
        need_ctx = layer < depth - 1
        j = layer // 2
        mod = matmul(cond, ada_w, bm=8, bn=512, out_dtype=F32, bias=ada_b[layer], layer=layer,
                     name="ada_mod")
        mod_l = [mod[:bsz, i * d:(i + 1) * d][:, None, :] for i in range(6)]
        mod_c = [jnp.broadcast_to(mod[bsz, i * d:(i + 1) * d][None, None, :], (bsz, 1, d)) for i in range(6)]
        g = norm_g[layer]
        h_lat, a_l = resid_norm(h_lat, f_l, gate_l, g_prev, g[0], mod_l[0], mod_l[1])
        h_ctx, a_c = resid_norm(h_ctx, f_c, gate_c, g_prev, g[0], mod_c[0], mod_c[1])
        if layer % 2 == 0:
            w_in = ab_w_in[j]
            zx0 = E_A + E_GATES
            w_main = jnp.concatenate([w_in[:, :E_A].astype(BF16), w_in[:, zx0:zx0 + E_B].astype(BF16)], axis=1)
            w_small = jnp.concatenate(
                [w_in[:, E_A:zx0], w_in[:, zx0 + E_B:],
                 jnp.zeros((d, 128 - E_GATES - 2 * SSM_HEADS), F32)], axis=1).astype(BF16)
            y_c, y_l = even_mixer(a_c, a_l, w_main, w_small, ab_w_out, j, ml_gate_b[j],
                                  ml_norm_g[j], ssm_conv_w[j], ssm_conv_b[j], ssm_dt_bias[j],
                                  ssm_a_log[j], ssm_d[j], ssm_norm_g[j], need_ctx)
        else:
            y_c, y_l = hgrn2_mixer(a_c, grid_to_colmajor(a_l), hg_w_in, hg_w_out, j,
                                   lower_bounds[:, layer], hg_norm_g[j], need_ctx)
        h_lat, a2_l = resid_norm(h_lat, y_l, mod_l[2], g[1], g[2], mod_l[3], mod_l[4])
        a2_c = None
        if need_ctx:
            h_ctx, a2_c = resid_norm(h_ctx, y_c, mod_c[2], g[1], g[2], mod_c[3], mod_c[4])
        f_l, f_c = ffn(a2_l, a2_c, ffn_w_gate, ffn_w_up, ffn_w_down, layer)
        gate_l = mod_l[5]
        gate_c = mod_c[5] if need_ctx else None
        g_prev = g[3]
    h_lat, _ = resid_norm(h_lat, f_l, gate_l, g_prev)
    return h_lat
```

```python
import functools
import math

import jax
import jax.numpy as jnp
import numpy as np
from jax import lax
from jax.experimental import pallas as pl
from jax.experimental.pallas import tpu as pltpu

F32 = jnp.float32
BF16 = jnp.bfloat16

D_MODEL = 4096
GRID_W = 64
EPS = 1e-6

ML_HEADS = 4
ML_QK_DIM = 256
ML_V_DIM = 512
ML_QK = ML_HEADS * ML_QK_DIM
ML_V = ML_HEADS * ML_V_DIM
ML_CHUNK = 128
ML_HEADS_PER_STEP = 4
GATE_CAP = 15.0

SSM_HEAD_DIM = 64
SSM_INNER = 2048
SSM_HEADS = 32
SSM_GROUPS = 8
SSM_HPG = 4
SSM_STATE = 128
SSM_CONV = 5
SSM_CHUNK = 128
SSM_GROUP_W = SSM_HPG * SSM_HEAD_DIM
SSM_CONV_DIM = SSM_INNER + 2 * SSM_GROUPS * SSM_STATE
SSD_GROUPS_PER_STEP = 8

HG_EXPAND = 128
HG_HEADS = 32
HG_F = HG_HEADS * HG_EXPAND
HG_V_DIM = 128
HG_CHUNK = 64
HG_LEVELS = 6
HG_HEADS_PER_STEP = 8

FFN_HIDDEN = 11008

E_Q, E_K, E_V, E_O, E_Z, E_XBC = 0, 1024, 2048, 4096, 6144, 8192
E_A = 6144
E_GATES = 16
E_B = SSM_INNER + SSM_CONV_DIM
E_MAIN = E_A + E_B
O_Q, O_F, O_I, O_G = 0, 4096, 12288, 16384

VMEM_LIMIT = 56 * 1024 * 1024


def _cparams(sem):
    return pltpu.CompilerParams(dimension_semantics=sem, vmem_limit_bytes=VMEM_LIMIT)


def _mm_kernel(*refs, has_bias, cast_w):
    x_ref, w_ref = refs[0], refs[1]
    b_ref = refs[2] if has_bias else None
    o_ref = refs[2 + has_bias]
    if cast_w:
        wb_ref = refs[3 + has_bias]

        @pl.when(pl.program_id(1) == 0)
        def _():
            wb_ref[...] = w_ref[...].astype(BF16)

        w = wb_ref[...]
    else:
        w = w_ref[...]
    acc = jnp.dot(x_ref[...].astype(BF16), w, preferred_element_type=F32)
    if has_bias:
        acc = acc + b_ref[...]
    o_ref[...] = acc.astype(o_ref.dtype)


def matmul(x, w, *, bm, bn, out_dtype, name, bias=None, layer=None, n=None):
    m, k = x.shape
    n = w.shape[-1] if n is None else n
    assert m % bm == 0 and n % bn == 0, (m, bm, n, bn)
    if layer is None:
        w_spec = pl.BlockSpec((k, bn), lambda j, i: (0, j))
    else:
        w_spec = pl.BlockSpec((None, k, bn), lambda j, i: (layer, 0, j))
    in_specs = [pl.BlockSpec((bm, k), lambda j, i: (i, 0)), w_spec]
    args = [x, w]
    if bias is not None:
        in_specs.append(pl.BlockSpec((1, bn), lambda j, i: (0, j)))
        args.append(bias.reshape(1, n))
    cast_w = w.dtype != BF16
    return pl.pallas_call(
        functools.partial(_mm_kernel, has_bias=bias is not None, cast_w=cast_w),
        grid=(n // bn, m // bm),
        in_specs=in_specs,
        out_specs=pl.BlockSpec((bm, bn), lambda j, i: (i, j)),
        out_shape=jax.ShapeDtypeStruct((m, n), out_dtype),
        scratch_shapes=[pltpu.VMEM((k, bn), BF16)] if cast_w else [],
        compiler_params=_cparams(("parallel", "arbitrary" if cast_w else "parallel")),
        name=name,
    )(*args)


def _swiglu(g, u):
    return g * jax.nn.sigmoid(g) * u


def _mm_stream_kernel(*refs, n_w, n_x, nj, mi, kc):
    x_refs, w_refs = refs[:n_x], refs[n_x:n_x + n_w]
    o_refs, wb_refs = refs[n_x + n_w:2 * n_x + n_w], refs[2 * n_x + n_w:]
    j, i = pl.program_id(0), pl.program_id(1)
    slot = j % 2

    @pl.when((j < nj) & (i < mi))
    def _():
        r0 = pl.multiple_of(i * kc, 16)
        for w_ref, wb_ref in zip(w_refs, wb_refs):
            wb_ref[slot, pl.ds(r0, kc), :] = w_ref[...].astype(BF16)

    def product(x_ref, o_ref):
        x = x_ref[...]
        accs = [jnp.dot(x, wb_ref[1 - slot], preferred_element_type=F32) for wb_ref in wb_refs]
        out = _swiglu(*accs) if n_w == 2 else accs[0]
        o_ref[...] = out.astype(o_ref.dtype)

    @pl.when((j > 0) & (i < mi))
    def _():
        product(x_refs[0], o_refs[0])

    if n_x == 2:
        @pl.when((j > 0) & (i == mi))
        def _():
            product(x_refs[1], o_refs[1])


def matmul_stream(x, ws, layer, *, bm, bn, out_dtype, name, x2=None, n=None):
    m, k = x.shape
    n = ws[0].shape[-1] if n is None else n
    mi, nj = m // bm, n // bn
    kc = k // mi
    assert m % bm == 0 and n % bn == 0 and k % mi == 0 and kc % 16 == 0 and mi > 1, (m, bm, n, bn, k)
    n_x = 1 if x2 is None else 2
    row = lambda j, i: jnp.where(j == 0, 0, jnp.minimum(i, mi - 1))
    col = lambda j: jnp.maximum(j - 1, 0)
    w_spec = pl.BlockSpec(
        (None, kc, bn),
        lambda j, i: (layer, jnp.where(j < nj, jnp.minimum(i, mi - 1), mi - 1), jnp.minimum(j, nj - 1)))
    in_specs = [pl.BlockSpec((bm, k), lambda j, i: (row(j, i), 0))]
    out_specs = [pl.BlockSpec((bm, bn), lambda j, i: (row(j, i), col(j)))]
    out_shape = [jax.ShapeDtypeStruct((m, n), out_dtype)]
    args = [x]
    if x2 is not None:
        m2 = x2.shape[0]
        assert m2 <= bm and x2.shape[1] == k
        in_specs.append(pl.BlockSpec((m2, k), lambda j, i: (0, 0)))
        out_specs.append(pl.BlockSpec((m2, bn), lambda j, i: (0, col(j))))
        out_shape.append(jax.ShapeDtypeStruct((m2, n), out_dtype))
        args.append(x2)
    outs = pl.pallas_call(
        functools.partial(_mm_stream_kernel, n_w=len(ws), n_x=n_x, nj=nj, mi=mi, kc=kc),
        grid=(nj + 1, mi + n_x - 1),
        in_specs=in_specs + [w_spec] * len(ws),
        out_specs=out_specs,
        out_shape=out_shape,
        scratch_shapes=[pltpu.VMEM((2, k, bn), BF16) for _ in ws],
        compiler_params=_cparams(("arbitrary", "arbitrary")),
        name=name,
    )(*args, *ws)
    return outs[0] if x2 is None else (outs[0], outs[1])


def _swiglu_kernel(x_ref, wg_ref, wu_ref, o_ref):
    x = x_ref[...]
    g = jnp.dot(x, wg_ref[...].astype(BF16), preferred_element_type=F32)
    u = jnp.dot(x, wu_ref[...].astype(BF16), preferred_element_type=F32)
    o_ref[...] = _swiglu(g, u).astype(o_ref.dtype)


def swiglu_single(x, wg, wu, layer, *, bn, name):
    m, k = x.shape
    n = wg.shape[-1]
    assert n % bn == 0
    w_spec = pl.BlockSpec((None, k, bn), lambda j: (layer, 0, j))
    return pl.pallas_call(
        _swiglu_kernel,
        grid=(n // bn,),
        in_specs=[pl.BlockSpec((m, k), lambda j: (0, 0)), w_spec, w_spec],
        out_specs=pl.BlockSpec((m, bn), lambda j: (0, j)),
        out_shape=jax.ShapeDtypeStruct((m, n), BF16),
        compiler_params=_cparams(("parallel",)),
        name=name,
    )(x, wg, wu)


def dense2(x, x2, w, layer, *, out_dtype, name, bm=1024, bn=1024, n=None):
    if x.shape[0] <= bm:
        ncol = w.shape[-1] if n is None else n
        one = lambda a, nm: matmul(a, w, bm=a.shape[0], bn=256, out_dtype=out_dtype, layer=layer, name=nm,
                                   n=ncol)
        return one(x, name), (None if x2 is None else one(x2, name + "_ctx"))
    y = matmul_stream(x, (w,), layer, bm=bm, bn=bn, out_dtype=out_dtype, name=name, x2=x2, n=n)
    return (y, None) if x2 is None else y


def dense(x, w, layer, *, out_dtype, name, bm=1024, bn=1024):
    m = x.shape[0]
    if m > bm:
        return matmul_stream(x, (w,), layer, bm=bm, bn=bn, out_dtype=out_dtype, name=name)
    return matmul(x, w, bm=m, bn=256, out_dtype=out_dtype, layer=layer, name=name)


def _rms(x):
    return x * lax.rsqrt(jnp.mean(x * x, axis=-1, keepdims=True) + EPS)


def _resid_norm_kernel(*refs, has_y, want_a):
    it = iter(refs)
    h_ref = next(it)
    if has_y:
        y_ref, gate_ref, g1_ref = next(it), next(it), next(it)
    if want_a:
        g2_ref, shift_ref, scale_ref = next(it), next(it), next(it)
    if has_y:
        hout_ref = next(it)
    if want_a:
        a_ref = next(it)
    h = h_ref[0]
    if has_y:
        h = h + gate_ref[0] * (_rms(y_ref[0].astype(F32)) * g1_ref[...])
        hout_ref[0] = h
    if want_a:
        a = (_rms(h) * g2_ref[...]) * (1.0 + scale_ref[0]) + shift_ref[0]
        a_ref[0] = a.astype(a_ref.dtype)


def resid_norm(h, y=None, gate=None, g1=None, g2=None, shift=None, scale=None, *, bt=256):
    b, t, d = h.shape
    has_y, want_a = y is not None, g2 is not None
    bt = min(bt, t)
    tok = pl.BlockSpec((1, bt, d), lambda i, j: (i, j, 0))
    per_b = pl.BlockSpec((1, 1, d), lambda i, j: (i, 0, 0))
    vec = pl.BlockSpec((1, d), lambda i, j: (0, 0))
    args, in_specs, out_shape, out_specs = [h], [tok], [], []
    if has_y:
        args += [y, gate, g1.reshape(1, d)]
        in_specs += [tok, per_b, vec]
        out_shape.append(jax.ShapeDtypeStruct((b, t, d), F32))
        out_specs.append(tok)
    if want_a:
        args += [g2.reshape(1, d), shift, scale]
        in_specs += [vec, per_b, per_b]
        out_shape.append(jax.ShapeDtypeStruct((b, t, d), BF16))
        out_specs.append(tok)
    outs = pl.pallas_call(
        functools.partial(_resid_norm_kernel, has_y=has_y, want_a=want_a),
        grid=(b, t // bt),
        in_specs=in_specs,
        out_specs=out_specs,
        out_shape=out_shape,
        compiler_params=_cparams(("parallel", "parallel")),
        name="resid_norm",
    )(*args)
    outs = list(outs)
    h_new = outs.pop(0) if has_y else h
    a = outs.pop(0) if want_a else None
    return h_new, a


def _softplus(x):
    return jnp.maximum(x, 0.0) + jnp.log1p(jnp.exp(-jnp.abs(x)))


def _log_sigmoid(x):
    return jnp.minimum(x, 0.0) - jnp.log1p(jnp.exp(-jnp.abs(x)))


def _chunk_index(c, nc, reverse):
    return (nc - 1 - c) if reverse else c


def _mlstm_kernel(*refs):
    ins, outs = refs[:14], refs[14:]

    @pl.when(pl.program_id(2) == 0)
    def _():
        for d in range(2):
            for s in range(3):
                outs[4 * d + 1 + s][...] = ins[7 * d + 4 + s][...]

    gates = [_mlstm_gate_forms(ins[7 * d + 3], reverse=d == 1) for d in range(2)]
    for hh in range(ML_HEADS_PER_STEP):
        for d in range(2):
            _mlstm_chunk(*ins[7 * d:7 * d + 3], gates[d], *outs[4 * d:4 * d + 4], hh=hh, reverse=d == 1)


def _mlstm_gate_forms(gr_ref, *, reverse):
    l = ML_CHUNK
    nr = 2 * ML_HEADS_PER_STEP
    row = lax.broadcasted_iota(jnp.int32, (l, l), 0)
    col = lax.broadcasted_iota(jnp.int32, (l, l), 1)
    mask_t = (col <= row) if reverse else (col >= row)
    is_forget = (lax.broadcasted_iota(jnp.int32, (nr, l), 0) & 1) == 1
    gcap = GATE_CAP * jnp.tanh(gr_ref[0, 0, 0] / GATE_CAP)
    rows = jnp.where(is_forget, _log_sigmoid(gcap), gcap)
    cums = _dot3((((1,), (0,)), ((), ())), _split3(rows), mask_t.astype(BF16))
    mixed = jnp.where(is_forget, cums, gcap)
    bc = _dot3((((0,), (0,)), ((), ())), _split3(mixed), _head_selector(nr, l))
    return gcap, cums, bc


def _mlstm_chunk(q_ref, k_ref, v_ref, gates, y_ref, c_ref, n_ref, m_ref, *, hh, reverse):
    qsl = slice(hh * ML_QK_DIM, (hh + 1) * ML_QK_DIM)
    vsl = slice(hh * ML_V_DIM, (hh + 1) * ML_V_DIM)
    l = ML_CHUNK
    row = lax.broadcasted_iota(jnp.int32, (l, l), 0)
    col = lax.broadcasted_iota(jnp.int32, (l, l), 1)
    mask = (col >= row) if reverse else (col <= row)

    gcap, cums, bc = gates
    ig_r = gcap[2 * hh:2 * hh + 1, :]
    b_r = cums[2 * hh + 1:2 * hh + 2, :]
    end_col = 0 if reverse else l - 1
    total = b_r[:, end_col:end_col + 1]
    ig_bc = bc[:, 2 * hh * l:(2 * hh + 1) * l]
    b_bc = bc[:, (2 * hh + 1) * l:(2 * hh + 2) * l]
    ig_c, b_c = ig_bc[:, 0:1], b_bc[:, 0:1]

    m_prev = m_ref[0, hh]
    logw = jnp.where(mask, b_bc - b_r + ig_r, -jnp.inf)
    inter = b_c + m_prev
    m_t = jnp.maximum(jnp.max(logw, axis=1, keepdims=True), inter)

    q = q_ref[0, :, qsl]
    k = k_ref[0, :, qsl]
    v = v_ref[0, :, vsl]
    scale = ML_QK_DIM ** -0.5
    qk = lax.dot_general(q, k, (((1,), (1,)), ((), ())), preferred_element_type=F32)
    s = qk * scale * jnp.exp(logw - m_t)
    inter_w = jnp.exp(inter - m_t) * scale

    c_st = c_ref[0, hh]
    n_st = n_ref[0, hh]
    num = (jnp.dot(s.astype(BF16), v, preferred_element_type=F32)
           + inter_w * jnp.dot(q, c_st.astype(BF16), preferred_element_type=F32))
    qn = jnp.sum(q.astype(F32) * n_st, axis=1, keepdims=True)
    den = jnp.sum(s, axis=1, keepdims=True) + inter_w * qn
    y_ref[0, :, vsl] = (num * (1.0 / jnp.maximum(jnp.abs(den), jnp.exp(-m_t)))).astype(y_ref.dtype)

    g_c = total - b_c + ig_c
    g_r = total - b_r + ig_r
    m_new = jnp.maximum(total + m_prev, jnp.max(g_r, axis=1, keepdims=True))
    w_c = jnp.exp(g_c - m_new)
    decay = jnp.exp(total + m_prev - m_new)
    kw = k.astype(F32) * w_c
    kv = lax.dot_general(kw.astype(BF16), v, (((0,), (0,)), ((), ())), preferred_element_type=F32)
    c_ref[0, hh] = decay * c_st + kv
    n_ref[0, hh] = decay * n_st + jnp.sum(kw, axis=0, keepdims=True)
    m_ref[0, hh] = m_new


def mlstm_scan(proj, gates_row, states, *, name):
    b, t, _ = proj.shape
    l = ML_CHUNK
    nc = t // l
    hp = ML_HEADS_PER_STEP
    gates_step = gates_row.reshape(2, b, ML_HEADS // hp, 2 * hp, t)
    qw, vw = hp * ML_QK_DIM, hp * ML_V_DIM
    qb, kb, vb = E_Q // qw, E_K // qw, E_V // vw
    st_specs = [pl.BlockSpec((1, hp, ML_QK_DIM, ML_V_DIM), lambda i, h, c: (i, h, 0, 0)),
                pl.BlockSpec((1, hp, 1, ML_QK_DIM), lambda i, h, c: (i, h, 0, 0)),
                pl.BlockSpec((1, hp, 1, 1), lambda i, h, c: (i, h, 0, 0))]
    in_specs, args, out_specs, out_shape = [], [], [], []
    for d in range(2):
        ci = functools.partial(_chunk_index, nc=nc, reverse=d == 1)
        in_specs += [
            pl.BlockSpec((1, l, qw), lambda i, h, c, ci=ci: (i, ci(c), qb + h)),
            pl.BlockSpec((1, l, qw), lambda i, h, c, ci=ci: (i, ci(c), kb + h)),
            pl.BlockSpec((1, l, vw), lambda i, h, c, ci=ci: (i, ci(c), vb + h)),
            pl.BlockSpec((1, 1, 1, 2 * hp, l), lambda i, h, c, ci=ci, d=d: (d, i, h, 0, ci(c))),
        ] + st_specs
        args += [proj, proj, proj, gates_step, *states[d]]
        out_specs += [pl.BlockSpec((1, l, vw), lambda i, h, c, ci=ci: (i, ci(c), h))] + st_specs
        out_shape += [jax.ShapeDtypeStruct((b, t, ML_V), BF16)]
        out_shape += [jax.ShapeDtypeStruct(s.shape, F32) for s in states[d]]
    outs = pl.pallas_call(
        _mlstm_kernel,
        grid=(b, ML_HEADS // hp, nc),
        in_specs=in_specs,
        out_specs=out_specs,
        out_shape=out_shape,
        compiler_params=_cparams(("parallel", "parallel", "arbitrary")),
        name=name,
    )(*args)
    return (outs[0], outs[4]), (tuple(outs[1:4]), tuple(outs[5:8]))


CONV_ROWS = 256
CONV_HALO = 16


def _conv_kernel(x_ref, w_ref, b_ref, o_ref, *, t):
    w = w_ref[...]
    bias = b_ref[...]
    r = min(CONV_ROWS, t)
    half = SSM_CONV // 2
    cw = x_ref.shape[-1]
    for r0 in range(0, t, r):
        parts = []
        if r0 == 0:
            parts.append(jnp.zeros((CONV_HALO, cw), F32))
        else:
            parts.append(x_ref[0, pl.ds(r0 - CONV_HALO, CONV_HALO), :].astype(F32))
        parts.append(x_ref[0, pl.ds(r0, r), :].astype(F32))
        if r0 + r == t:
            parts.append(jnp.zeros((CONV_HALO, cw), F32))
        else:
            parts.append(x_ref[0, pl.ds(r0 + r, CONV_HALO), :].astype(F32))
        win = jnp.concatenate(parts, axis=0)
        n = r + 2 * CONV_HALO
        acc = jnp.zeros((r, cw), F32) + bias
        for kk in range(SSM_CONV):
            sh = (half - kk) % n
            rolled = win if sh == 0 else pltpu.roll(win, sh, 0)
            acc = acc + w[kk:kk + 1, :] * rolled[CONV_HALO:CONV_HALO + r, :]
        o_ref[0, pl.ds(r0, r), :] = (acc * jax.nn.sigmoid(acc)).astype(o_ref.dtype)


def conv_silu(proj, conv_w, conv_b, *, bc=512):
    b, t, _ = proj.shape
    c = SSM_CONV_DIM
    off = E_XBC // bc
    return pl.pallas_call(
        functools.partial(_conv_kernel, t=t),
        grid=(b, c // bc),
        in_specs=[pl.BlockSpec((1, t, bc), lambda i, j: (i, 0, off + j)),
                  pl.BlockSpec((SSM_CONV, bc), lambda i, j: (0, j)),
                  pl.BlockSpec((1, bc), lambda i, j: (0, j))],
        out_specs=pl.BlockSpec((1, t, bc), lambda i, j: (i, 0, j)),
        out_shape=jax.ShapeDtypeStruct((b, t, c), BF16),
        compiler_params=_cparams(("parallel", "parallel")),
        name="conv_silu",
    )(proj, conv_w, conv_b.reshape(1, c))


def _split3(x):
    hi = x.astype(BF16)
    r1 = x - hi.astype(F32)
    mid = r1.astype(BF16)
    lo = (r1 - mid.astype(F32)).astype(BF16)
    return hi, mid, lo


def _dot3(dims, parts_lhs, rhs):
    return sum(lax.dot_general(p, rhs, dims, preferred_element_type=F32) for p in parts_lhs)


def _head_selector(heads, width):
    rows = lax.broadcasted_iota(jnp.int32, (heads, heads * width), 0)
    lanes = lax.broadcasted_iota(jnp.int32, (heads, heads * width), 1)
    return (lanes // width == rows).astype(BF16)


def _ssd_kernel(*refs):
    ins, outs = refs[:12], refs[12:]

    @pl.when(pl.program_id(2) == 0)
    def _():
        for d in range(2):
            outs[2 * d + 1][...] = ins[6 * d + 5][...]

    for d in range(2):
        _ssd_chunk(*ins[6 * d:6 * d + 5], *outs[2 * d:2 * d + 2], reverse=d == 1)


def _ssd_chunk(xs_ref, bm_ref, cm_ref, dtr_ref, pr_ref, y_ref, h_ref, *, reverse):
    l = SSM_CHUNK
    gp = SSD_GROUPS_PER_STEP
    nh = gp * SSM_HPG
    row = lax.broadcasted_iota(jnp.int32, (l, l), 0)
    col = lax.broadcasted_iota(jnp.int32, (l, l), 1)
    mask = (col >= row) if reverse else (col <= row)
    mask_t = (col <= row) if reverse else (col >= row)

    nn = (((1,), (0,)), ((), ()))
    nt = (((1,), (1,)), ((), ()))
    tn = (((0,), (0,)), ((), ()))
    pr = pr_ref[0, 0]
    dt_r = _softplus(dtr_ref[0, 0, 0] + pr[:, 0:1])
    a_r = dt_r * pr[:, 1:2]
    cum_r = _dot3(nn, _split3(a_r), mask_t.astype(BF16))
    dt_parts, cum_parts = _split3(dt_r), _split3(cum_r)
    sel_ch = _head_selector(nh, SSM_HEAD_DIM)
    dt_ch = _dot3(tn, dt_parts, sel_ch)
    cum_ch = _dot3(tn, cum_parts, sel_ch)
    cum_bc = _dot3(tn, cum_parts, _head_selector(nh, l))
    end_row = 0 if reverse else l - 1
    total_ch = cum_ch[end_row:end_row + 1, :]

    xdt = xs_ref[0].astype(F32) * dt_ch
    e_cum = jnp.exp(cum_ch)
    xw = (xdt * jnp.exp(total_ch - cum_ch)).astype(BF16)
    s_decay = jnp.exp(total_ch)
    lane_head = lax.broadcasted_iota(jnp.int32, (l, SSM_GROUP_W), 1) // SSM_HEAD_DIM
    for g in range(gp):
        xsl = slice(g * SSM_GROUP_W, (g + 1) * SSM_GROUP_W)
        nsl = slice(g * SSM_STATE, (g + 1) * SSM_STATE)
        bm = bm_ref[0, :, nsl]
        cm = cm_ref[0, :, nsl]
        hst = h_ref[0, g]
        cb = lax.dot_general(cm, bm, nt, preferred_element_type=F32)
        inter = jnp.dot(cm, hst.astype(BF16), preferred_element_type=F32)
        xdt_g = xdt[:, xsl]
        ws, xjs = [], []
        for j in range(SSM_HPG):
            hd = g * SSM_HPG + j
            decay = jnp.exp(jnp.where(mask, cum_bc[:, hd * l:(hd + 1) * l] - cum_r[hd:hd + 1, :], -jnp.inf))
            ws.append((decay * cb).astype(BF16))
            xjs.append(jnp.where(lane_head == j, xdt_g, 0.0).astype(BF16))
        y = jnp.dot(jnp.concatenate(ws, axis=1), jnp.concatenate(xjs, axis=0), preferred_element_type=F32)
        y_ref[0, :, xsl] = (y + e_cum[:, xsl] * inter).astype(y_ref.dtype)
        upd = lax.dot_general(bm, xw[:, xsl], tn, preferred_element_type=F32)
        h_ref[0, g] = s_decay[:, xsl] * hst + upd


def ssd_scan(xbc, dt_row, par_row, states, *, name):
    b, t, _ = xbc.shape
    l = SSM_CHUNK
    nc = t // l
    gp = SSD_GROUPS_PER_STEP
    xw, nw = gp * SSM_GROUP_W, gp * SSM_STATE
    bmb = SSM_INNER // nw
    cmb = bmb + SSM_GROUPS // gp
    st_spec = pl.BlockSpec((1, gp, SSM_STATE, SSM_GROUP_W), lambda i, g, c: (i, g, 0, 0))
    dt_step = dt_row.reshape(2, b, SSM_GROUPS // gp, gp * SSM_HPG, t)
    par_step = par_row.reshape(2, SSM_GROUPS // gp, gp * SSM_HPG, 2)
    in_specs, args, out_specs, out_shape = [], [], [], []
    for d in range(2):
        ci = functools.partial(_chunk_index, nc=nc, reverse=d == 1)
        in_specs += [
            pl.BlockSpec((1, l, xw), lambda i, g, c, ci=ci: (i, ci(c), g)),
            pl.BlockSpec((1, l, nw), lambda i, g, c, ci=ci: (i, ci(c), bmb + g)),
            pl.BlockSpec((1, l, nw), lambda i, g, c, ci=ci: (i, ci(c), cmb + g)),
            pl.BlockSpec((1, 1, 1, gp * SSM_HPG, l), lambda i, g, c, ci=ci, d=d: (d, i, g, 0, ci(c))),
            pl.BlockSpec((1, 1, gp * SSM_HPG, 2), lambda i, g, c, d=d: (d, g, 0, 0)),
            st_spec,
        ]
        args += [xbc, xbc, xbc, dt_step, par_step, states[d]]
        out_specs += [pl.BlockSpec((1, l, xw), lambda i, g, c, ci=ci: (i, ci(c), g)), st_spec]
        out_shape += [jax.ShapeDtypeStruct((b, t, SSM_INNER), BF16),
                      jax.ShapeDtypeStruct(states[d].shape, F32)]
    yf, hf, yb, hb = pl.pallas_call(
        _ssd_kernel,
        grid=(b, SSM_GROUPS // gp, nc),
        in_specs=in_specs,
        out_specs=out_specs,
        out_shape=out_shape,
        compiler_params=_cparams(("parallel", "parallel", "arbitrary")),
        name=name,
    )(*args)
    return (yf, yb), (hf, hb)


def _even_out_kernel(mlf_ref, mlb_ref, ssdf_ref, ssdb_ref, o_ref, z_ref, xs_ref, mlg_ref, dsk_ref, ssg_ref,
                     out_ref):
    for h in range(ML_HEADS):
        sl = slice(h * ML_V_DIM, (h + 1) * ML_V_DIM)
        y = mlf_ref[0, :, sl].astype(F32) + mlb_ref[0, :, sl].astype(F32)
        og = o_ref[0, :, sl].astype(F32)
        out_ref[0, :, sl] = (_rms(y) * mlg_ref[:, sl] * jax.nn.sigmoid(og)).astype(out_ref.dtype)
    for g in range(SSM_GROUPS):
        sl = slice(g * SSM_GROUP_W, (g + 1) * SSM_GROUP_W)
        z = z_ref[0, :, sl].astype(F32)
        y = ((ssdf_ref[0, :, sl].astype(F32) + ssdb_ref[0, :, sl].astype(F32))
             + dsk_ref[:, sl] * xs_ref[0, :, sl].astype(F32))
        y = y * (z * jax.nn.sigmoid(z))
        so = slice(ML_V + g * SSM_GROUP_W, ML_V + (g + 1) * SSM_GROUP_W)
        out_ref[0, :, so] = (_rms(y) * ssg_ref[:, sl]).astype(out_ref.dtype)


def even_out(ml_y, ssd_y, proj_a, proj_b, xbc, ml_norm_g, d_skip_full, ssm_norm_g, *, bt=256):
    b, t, _ = ml_y[0].shape
    bt = min(bt, t)
    w = ML_V
    tok = lambda blk: pl.BlockSpec((1, bt, w), lambda i, j: (i, j, blk))
    vec = pl.BlockSpec((1, w), lambda i, j: (0, 0))
    return pl.pallas_call(
        _even_out_kernel,
        grid=(b, t // bt),
        in_specs=[tok(0), tok(0), tok(0), tok(0), tok(E_O // w), tok(E_Z // w), tok(0), vec, vec, vec],
        out_specs=pl.BlockSpec((1, bt, 2 * w), lambda i, j: (i, j, 0)),
        out_shape=jax.ShapeDtypeStruct((b, t, 2 * w), BF16),
        compiler_params=_cparams(("parallel", "parallel")),
        name="even_gate",
    )(ml_y[0], ml_y[1], ssd_y[0], ssd_y[1], proj_a, proj_b, xbc, ml_norm_g.reshape(1, w), d_skip_full.reshape(1, w),
      ssm_norm_g.reshape(1, w))


def _hg_constants(reverse):
    l = HG_CHUNK
    pos = np.arange(l)
    tri = (pos[None, :] <= pos[:, None]).astype(np.float32)
    masks = []
    for lv in range(HG_LEVELS):
        m = 1 << lv
        blk = pos // (2 * m)
        later = (pos % (2 * m)) >= m
        masks.append(((blk[:, None] == blk[None, :]) & later[:, None] & (~later)[None, :]).astype(np.float32))
    masks = np.stack(masks)
    if reverse:
        tri = tri[::-1, ::-1]
        masks = masks[:, ::-1, ::-1]
    return (jnp.asarray(tri, BF16), jnp.asarray(masks.reshape(-1, l), F32))


def _hg_level_factors(q, k, f, cum, reverse):
    l, w = q.shape
    row = lax.broadcasted_iota(jnp.int32, (l, w), 0)
    sub = lax.broadcasted_iota(jnp.int32, (8, w), 0)
    zs = []
    for lv in range(HG_LEVELS):
        m = 1 << lv
        late = ((row & m) == 0) if reverse else ((row & m) != 0)
        if m >= 8:
            pieces = []
            for b0 in range(0, l, 2 * m):
                lo_rows, hi_rows = slice(b0, b0 + m), slice(b0 + m, b0 + 2 * m)
                if reverse:
                    mid = cum[b0 + m:b0 + m + 1]
                    pieces.append(q[lo_rows] * jnp.exp(cum[lo_rows] - mid))
                    pieces.append(k[hi_rows] * jnp.exp(mid - cum[hi_rows]))
                else:
                    mid = cum[b0 + m - 1:b0 + m]
                    pieces.append(k[lo_rows] * jnp.exp(mid - cum[lo_rows]))
                    pieces.append(q[hi_rows] * jnp.exp(cum[hi_rows] - mid))
            zs.append(jnp.concatenate(pieces, axis=0).astype(BF16))
            continue
        if m == 1:
            ex = jnp.where(late, f, 1.0)
        else:
            pieces = []
            if m >= 4:
                for b0 in range(0, l, 2 * m):
                    r = b0 + (m if reverse else m - 1)
                    pieces.append(cum[b0:b0 + 2 * m] - cum[r:r + 1])
            else:
                for b0 in range(0, l, 8):
                    ra, rb = (b0 + 2, b0 + 6) if reverse else (b0 + 1, b0 + 5)
                    mid = jnp.where(sub < 4, cum[ra:ra + 1], cum[rb:rb + 1])
                    pieces.append(cum[b0:b0 + 8] - mid)
            dd = pieces[0] if len(pieces) == 1 else jnp.concatenate(pieces, axis=0)
            ex = jnp.exp(jnp.where(late, dd, -dd))
        zs.append((jnp.where(late, q, k) * ex).astype(BF16))
    return zs


def _hg_chunk(q_ref, f_ref, i_ref, lb, tri, masks_ref, y_ref, s_ref, r0, reverse):
    l, e = HG_CHUNK, HG_EXPAND
    q_raw = q_ref[0, pl.ds(r0, l), :].astype(F32)
    f_raw = f_ref[0, pl.ds(r0, l), :].astype(F32)
    v = i_ref[0, pl.ds(r0, l), :]
    f = lb + (1.0 - lb) * jax.nn.sigmoid(f_raw)
    k = 1.0 - f
    lf = jnp.log(f)
    q = q_raw * jax.nn.sigmoid(q_raw) * (e ** -0.5)
    hi, mid, lo = _split3(lf)
    cum = (jnp.dot(tri, hi, preferred_element_type=F32)
           + jnp.dot(tri, mid, preferred_element_type=F32)
           + jnp.dot(tri, lo, preferred_element_type=F32))
    zs = _hg_level_factors(q, k, f, cum, reverse)
    end_row = 0 if reverse else l - 1
    cum_end = cum[end_row:end_row + 1, :]
    qc = (q * jnp.exp(cum)).astype(BF16)
    kt_end = (k * jnp.exp(cum_end - cum)).astype(BF16)
    s_decay = jnp.exp(cum_end)
    qk = q * k
    row = lax.broadcasted_iota(jnp.int32, (l, l), 0)
    col = lax.broadcasted_iota(jnp.int32, (l, l), 1)
    eye = row == col
    nt = (((1,), (1,)), ((), ()))
    tn = (((0,), (0,)), ((), ()))
    for hh in range(HG_HEADS_PER_STEP):
        sl = slice(hh * e, (hh + 1) * e)
        att = jnp.where(eye, jnp.sum(qk[:, sl], axis=1, keepdims=True), 0.0)
        for lv in range(HG_LEVELS):
            z = zs[lv][:, sl]
            a_lv = lax.dot_general(z, z, nt, preferred_element_type=F32)
            att = att + masks_ref[pl.ds(lv * l, l), :] * a_lv
        st = s_ref[0, hh]
        vh = v[:, sl]
        o = (jnp.dot(att.astype(BF16), vh, preferred_element_type=F32)
             + lax.dot_general(qc[:, sl], st.astype(BF16), nt, preferred_element_type=F32))
        y_ref[0, pl.ds(r0, l), sl] = o.astype(y_ref.dtype)
        upd = lax.dot_general(vh, kt_end[:, sl], tn, preferred_element_type=F32)
        s_ref[0, hh] = s_decay[:, sl] * st + upd


def _hg_kernel(qf_ref, ff_ref, if_ref, qb_ref, fb_ref, ib_ref, lbf_ref, lbb_ref, trif_ref, trib_ref,
               mf_ref, mb_ref, sf0_ref, sb0_ref, yf_ref, yb_ref, sf_ref, sb_ref, *, n_inner):
    l = HG_CHUNK

    @pl.when(pl.program_id(2) == 0)
    def _():
        sf_ref[...] = sf0_ref[...]
        sb_ref[...] = sb0_ref[...]

    lbf = lbf_ref[0]
    lbb = lbb_ref[0]
    trif = trif_ref[...]
    trib = trib_ref[...]

    def chunk(ci, carry):
        rf = pl.multiple_of(ci * l, l)
        rb = pl.multiple_of((n_inner - 1 - ci) * l, l)
        _hg_chunk(qf_ref, ff_ref, if_ref, lbf, trif, mf_ref, yf_ref, sf_ref, rf, False)
        _hg_chunk(qb_ref, fb_ref, ib_ref, lbb, trib, mb_ref, yb_ref, sb_ref, rb, True)
        return carry

    lax.fori_loop(0, n_inner, chunk, 0, unroll=4)


def hgrn2_scan(proj, lb, states, *, tb=512, name):
    b, t, _ = proj.shape
    tb = min(tb, t)
    nb = t // tb
    n_inner = tb // HG_CHUNK
    hw = HG_HEADS_PER_STEP * HG_EXPAND
    nh = HG_HEADS // HG_HEADS_PER_STEP
    qb_, ffb, fbb, ib_ = O_Q // hw, O_F // hw, (O_F + HG_F) // hw, O_I // hw
    trif, mf = _hg_constants(False)
    trib, mb = _hg_constants(True)
    fwd = lambda blk: pl.BlockSpec((1, tb, hw), lambda i, h, c: (i, c, blk + h))
    bwd = lambda blk: pl.BlockSpec((1, tb, hw), lambda i, h, c: (i, nb - 1 - c, blk + h))
    const = lambda a: pl.BlockSpec(a.shape, lambda i, h, c: (0,) * a.ndim)
    st_spec = pl.BlockSpec((1, HG_HEADS_PER_STEP, HG_V_DIM, HG_EXPAND), lambda i, h, c: (i, h, 0, 0))
    in_specs = [fwd(qb_), fwd(ffb), fwd(ib_), bwd(qb_), bwd(fbb), bwd(ib_),
                pl.BlockSpec((1, 1, hw), lambda i, h, c: (0, 0, h)),
                pl.BlockSpec((1, 1, hw), lambda i, h, c: (1, 0, h)),
                const(trif), const(trib), const(mf), const(mb), st_spec, st_spec]
    yf, yb, sf, sb = pl.pallas_call(
        functools.partial(_hg_kernel, n_inner=n_inner),
        grid=(b, nh, nb),
        in_specs=in_specs,
        out_specs=[fwd(0), bwd(0), st_spec, st_spec],
        out_shape=[jax.ShapeDtypeStruct((b, t, D_MODEL), BF16),
                   jax.ShapeDtypeStruct((b, t, D_MODEL), BF16),
                   jax.ShapeDtypeStruct(states[0].shape, F32),
                   jax.ShapeDtypeStruct(states[1].shape, F32)],
        compiler_params=_cparams(("parallel", "parallel", "arbitrary")),
        name=name,
    )(proj, proj, proj, proj, proj, proj, lb, lb, trif, trib, mf, mb, states[0], states[1])
    return (yf, yb), (sf, sb)


def _hg_out_kernel(of_ref, ob_ref, g_ref, ng_ref, out_ref):
    for h in range(of_ref.shape[-1] // HG_V_DIM):
        sl = slice(h * HG_V_DIM, (h + 1) * HG_V_DIM)
        g = g_ref[0, :, sl].astype(F32)
        o = of_ref[0, :, sl].astype(F32) + ob_ref[0, :, sl].astype(F32)
        out_ref[0, :, sl] = (_rms(o) * ng_ref[:, sl] * (g * jax.nn.sigmoid(g))).astype(out_ref.dtype)


def hg_out(o_f, o_b, proj, norm_g, *, bt=512, bc=1024, name):
    b, t, d = o_f.shape
    bt = min(bt, t)
    gb = O_G // bc
    tok = pl.BlockSpec((1, bt, bc), lambda i, j, c: (i, j, c))
    return pl.pallas_call(
        _hg_out_kernel,
        grid=(b, t // bt, d // bc),
        in_specs=[tok, tok,
                  pl.BlockSpec((1, bt, bc), lambda i, j, c: (i, j, gb + c)),
                  pl.BlockSpec((1, bc), lambda i, j, c: (0, c))],
        out_specs=tok,
        out_shape=jax.ShapeDtypeStruct((b, t, d), BF16),
        compiler_params=_cparams(("parallel", "parallel", "parallel")),
        name=name,
    )(o_f, o_b, proj, norm_g.reshape(1, d))


def even_mixer(a_c, a_l, w_main, w_small, w_out, layer, ml_gate_b, ml_norm_g, conv_w, conv_b,
               dt_bias, a_log, d_skip, ssm_norm_g, need_ctx):
    bsz = a_l.shape[0]

    def features(a, tag):
        b, t, d = a.shape
        bm = min(1024, b * t)
        a2 = a.reshape(b * t, d)
        proj = matmul(a2, w_main, bm=bm, bn=1024, out_dtype=BF16, name="ab_in_" + tag).reshape(b, t, E_MAIN)
        proj_a = proj
        small = matmul(a2, w_small, bm=bm, bn=128, out_dtype=F32, name="ab_in_small_" + tag).reshape(b, t, 128)
        gates = small[..., :16].reshape(b, t, 2, 2, ML_HEADS) + ml_gate_b.astype(F32)
        g_row = gates.transpose(2, 0, 4, 3, 1)
        dt = small[..., 16:16 + 2 * SSM_HEADS].reshape(b, t, 2, SSM_GROUPS, SSM_HPG)
        dt_row = dt.transpose(2, 0, 3, 4, 1)
        xbc = conv_silu(proj, conv_w, conv_b)
        return dict(proj_a=proj_a, proj=proj, xbc=xbc, g_row=g_row, dt_row=dt_row)

    fc, fl = features(a_c, "ctx"), features(a_l, "lat")
    neg_a = -jnp.exp(a_log.astype(F32))
    par = jnp.stack([dt_bias.astype(F32), neg_a], axis=1).reshape(2, 2, SSM_GROUPS, SSM_HPG)
    par_row = par.transpose(0, 2, 3, 1)

    ml_init = (jnp.zeros((bsz, ML_HEADS, ML_QK_DIM, ML_V_DIM), F32),
               jnp.zeros((bsz, ML_HEADS, 1, ML_QK_DIM), F32),
               jnp.zeros((bsz, ML_HEADS, 1, 1), F32))
    ssd_init = jnp.zeros((bsz, SSM_GROUPS, SSM_STATE, SSM_GROUP_W), F32)

    def ml(f, st, tag):
        return mlstm_scan(f['proj_a'], f['g_row'], st, name="mlstm_scan_" + tag)

    def ssd(f, st, tag):
        return ssd_scan(f['xbc'], f['dt_row'], par_row, st, name="ssd_scan_" + tag)

    ml_c, ml_st = ml(fc, (ml_init, ml_init), "ctx")
    ml_l, _ = ml(fl, ml_st, "lat")
    ssd_c, ssd_st = ssd(fc, (ssd_init, ssd_init), "ctx")
    ssd_l, _ = ssd(fl, ssd_st, "lat")
    d_full = jnp.repeat(d_skip.astype(F32), SSM_HEAD_DIM)

    def gated(f, ml_y, ssd_y):
        b, t, _ = ml_y[0].shape
        cat = even_out(ml_y, ssd_y, f['proj_a'], f['proj'], f['xbc'], ml_norm_g, d_full, ssm_norm_g)
        return cat.reshape(b * t, D_MODEL)

    cat_c = gated(fc, ml_c, ssd_c) if need_ctx else None
    y_l, y_c = dense2(gated(fl, ml_l, ssd_l), cat_c, w_out, layer, out_dtype=BF16, name="ab_out")
    y_l = y_l.reshape(a_l.shape)
    return (y_c.reshape(a_c.shape) if need_ctx else None), y_l


def hgrn2_mixer(a_c, a_l, w_in, w_out, layer, lb, norm_g, need_ctx, lat_colmajor=True):
    bsz = a_l.shape[0]
    lb3 = lb.astype(F32).reshape(2, 1, HG_F)

    flat = lambda a: a.reshape(a.shape[0] * a.shape[1], a.shape[2])
    pl_, pc = dense2(flat(a_l), flat(a_c), w_in, layer, out_dtype=BF16, name="hg_in")
    pl_ = pl_.reshape(a_l.shape[:2] + (5 * D_MODEL,))
    pc = pc.reshape(a_c.shape[:2] + (5 * D_MODEL,))
    init = jnp.zeros((bsz, HG_HEADS, HG_V_DIM, HG_EXPAND), F32)
    o_c, st = hgrn2_scan(pc, lb3, (init, init), name="hg_scan_ctx")
    o_l, _ = hgrn2_scan(pl_, lb3, st, name="hg_scan_lat")

    def gated(p, o, colmajor, tag):
        g = hg_out(o[0], o[1], p, norm_g, name="hg_gate_" + tag)
        return flat(grid_from_colmajor(g) if colmajor else g)

    g_c = gated(pc, o_c, False, "ctx") if need_ctx else None
    y_l, y_c = dense2(gated(pl_, o_l, lat_colmajor, "lat"), g_c, w_out, layer, out_dtype=BF16, name="hg_out")
    return (y_c.reshape(a_c.shape) if need_ctx else None), y_l.reshape(a_l.shape)


def grid_to_colmajor(h):
    bsz, t, d = h.shape
    rows = t // GRID_W
    return h.reshape(bsz, rows, GRID_W, d).transpose(0, 2, 1, 3).reshape(bsz, t, d)


def grid_from_colmajor(h):
    bsz, t, d = h.shape
    rows = t // GRID_W
    return h.reshape(bsz, GRID_W, rows, d).transpose(0, 2, 1, 3).reshape(bsz, t, d)


def ffn(a_l, a_c, wg, wu, wd, layer):
    flat = lambda a: a.reshape(a.shape[0] * a.shape[1], a.shape[2])
    hid_l = matmul_stream(flat(a_l), (wg, wu), layer, bm=2048, bn=256, out_dtype=BF16, name="ffn_up_lat")
    f_l = dense(hid_l, wd, layer, out_dtype=BF16, name="ffn_down_lat", bm=512, bn=512).reshape(a_l.shape)
    if a_c is None:
        return f_l, None
    hid_c = swiglu_single(flat(a_c), wg, wu, layer, bn=256, name="ffn_up_ctx")
    f_c = dense(hid_c, wd, layer, out_dtype=BF16, name="ffn_down_ctx", bm=512, bn=512).reshape(a_c.shape)
    return f_l, f_c


def kernel(x, c, ctx, c_ctx, ada_w, ada_b, norm_g, ffn_w_gate, ffn_w_up, ffn_w_down,
           ab_w_in, ab_w_out, ml_gate_b, ml_norm_g, ssm_conv_w, ssm_conv_b, ssm_dt_bias,
           ssm_a_log, ssm_d, ssm_norm_g, hg_w_in, hg_w_out, hg_lb, hg_norm_g):
    depth = ada_w.shape[0]
    bsz = x.shape[0]
    d = D_MODEL
    lb_p = jax.nn.softmax(hg_lb.astype(F32), axis=1)
    lower_bounds = jnp.cumsum(lb_p, axis=1) - lb_p[:, :1]
    cond = jnp.concatenate([jax.nn.silu(c), jax.nn.silu(c_ctx)[None, :],
                            jnp.zeros((8 - bsz - 1, d), F32)], axis=0)
    h_lat, h_ctx = x, ctx
    y_l = y_c = f_l = f_c = None
    gate_l = gate_c = g_prev = None
    for layer in range(depth):
        need_ctx = layer < depth - 1
        j = layer // 2
        mod = matmul(cond, ada_w, bm=8, bn=512, out_dtype=F32, bias=ada_b[layer], layer=layer,
                     name="ada_mod")
        mod_l = [mod[:bsz, i * d:(i + 1) * d][:, None, :] for i in range(6)]
        mod_c = [jnp.broadcast_to(mod[bsz, i * d:(i + 1) * d][None, None, :], (bsz, 1, d)) for i in range(6)]
        g = norm_g[layer]
        h_lat, a_l = resid_norm(h_lat, f_l, gate_l, g_prev, g[0], mod_l[0], mod_l[1])
        h_ctx, a_c = resid_norm(h_ctx, f_c, gate_c, g_prev, g[0], mod_c[0], mod_c[1])
        if layer % 2 == 0:
            w_in = ab_w_in[j]
            zx0 = E_A + E_GATES
            w_main = jnp.concatenate([w_in[:, :E_A].astype(BF16), w_in[:, zx0:zx0 + E_B].astype(BF16)], axis=1)
            w_small = jnp.concatenate(
                [w_in[:, E_A:zx0], w_in[:, zx0 + E_B:],
                 jnp.zeros((d, 128 - E_GATES - 2 * SSM_HEADS), F32)], axis=1).astype(BF16)
            y_c, y_l = even_mixer(a_c, a_l, w_main, w_small, ab_w_out, j, ml_gate_b[j],
                                  ml_norm_g[j], ssm_conv_w[j], ssm_conv_b[j], ssm_dt_bias[j],
                                  ssm_a_log[j], ssm_d[j], ssm_norm_g[j], need_ctx)
        else:
            y_c, y_l = hgrn2_mixer(a_c, grid_to_colmajor(a_l), hg_w_in, hg_w_out, j,
                                   lower_bounds[:, layer], hg_norm_g[j], need_ctx)
        h_lat, a2_l = resid_norm(h_lat, y_l, mod_l[2], g[1], g[2], mod_l[3], mod_l[4])
        a2_c = None
        if need_ctx:
            h_ctx, a2_c = resid_norm(h_ctx, y_c, mod_c[2], g[1], g[2], mod_c[3], mod_c[4])
        f_l, f_c = ffn(a2_l, a2_c, ffn_w_gate, ffn_w_up, ffn_w_down, layer)
        gate_l = mod_l[5]
        gate_c = mod_c[5] if need_ctx else None
        g_prev = g[3]
    h_lat, _ = resid_norm(h_lat, f_l, gate_l, g_prev)
    return h_lat
```

```python
import functools
import math

import jax
import jax.numpy as jnp
import numpy as np
from jax import lax
from jax.experimental import pallas as pl
from jax.experimental.pallas import tpu as pltpu

F32 = jnp.float32
BF16 = jnp.bfloat16

D_MODEL = 4096
GRID_W = 64
EPS = 1e-6

ML_HEADS = 4
ML_QK_DIM = 256
ML_V_DIM = 512
ML_QK = ML_HEADS * ML_QK_DIM
ML_V = ML_HEADS * ML_V_DIM
ML_CHUNK = 128
ML_HEADS_PER_STEP = 4
GATE_CAP = 15.0

SSM_HEAD_DIM = 64
SSM_INNER = 2048
SSM_HEADS = 32
SSM_GROUPS = 8
SSM_HPG = 4
SSM_STATE = 128
SSM_CONV = 5
SSM_CHUNK = 128
SSM_GROUP_W = SSM_HPG * SSM_HEAD_DIM
SSM_CONV_DIM = SSM_INNER + 2 * SSM_GROUPS * SSM_STATE
SSD_GROUPS_PER_STEP = 8

HG_EXPAND = 128
HG_HEADS = 32
HG_F = HG_HEADS * HG_EXPAND
HG_V_DIM = 128
HG_CHUNK = 64
HG_LEVELS = 6
HG_HEADS_PER_STEP = 8

FFN_HIDDEN = 11008

E_Q, E_K, E_V, E_O, E_Z, E_XBC = 0, 1024, 2048, 4096, 6144, 8192
E_A = 6144
E_GATES = 16
E_B = SSM_INNER + SSM_CONV_DIM
E_MAIN = E_A + E_B
O_Q, O_F, O_I, O_G = 0, 4096, 12288, 16384

VMEM_LIMIT = 56 * 1024 * 1024


def _cparams(sem):
    return pltpu.CompilerParams(dimension_semantics=sem, vmem_limit_bytes=VMEM_LIMIT)


def _mm_kernel(*refs, has_bias, cast_w):
    x_ref, w_ref = refs[0], refs[1]
    b_ref = refs[2] if has_bias else None
    o_ref = refs[2 + has_bias]
    if cast_w:
        wb_ref = refs[3 + has_bias]

        @pl.when(pl.program_id(1) == 0)
        def _():
            wb_ref[...] = w_ref[...].astype(BF16)

        w = wb_ref[...]
    else:
        w = w_ref[...]
    acc = jnp.dot(x_ref[...].astype(BF16), w, preferred_element_type=F32)
    if has_bias:
        acc = acc + b_ref[...]
    o_ref[...] = acc.astype(o_ref.dtype)


def matmul(x, w, *, bm, bn, out_dtype, name, bias=None, layer=None, n=None):
    m, k = x.shape
    n = w.shape[-1] if n is None else n
    assert m % bm == 0 and n % bn == 0, (m, bm, n, bn)
    if layer is None:
        w_spec = pl.BlockSpec((k, bn), lambda j, i: (0, j))
    else:
        w_spec = pl.BlockSpec((None, k, bn), lambda j, i: (layer, 0, j))
    in_specs = [pl.BlockSpec((bm, k), lambda j, i: (i, 0)), w_spec]
    args = [x, w]
    if bias is not None:
        in_specs.append(pl.BlockSpec((1, bn), lambda j, i: (0, j)))
        args.append(bias.reshape(1, n))
    cast_w = w.dtype != BF16
    return pl.pallas_call(
        functools.partial(_mm_kernel, has_bias=bias is not None, cast_w=cast_w),
        grid=(n // bn, m // bm),
        in_specs=in_specs,
        out_specs=pl.BlockSpec((bm, bn), lambda j, i: (i, j)),
        out_shape=jax.ShapeDtypeStruct((m, n), out_dtype),
        scratch_shapes=[pltpu.VMEM((k, bn), BF16)] if cast_w else [],
        compiler_params=_cparams(("parallel", "arbitrary" if cast_w else "parallel")),
        name=name,
    )(*args)


def _swiglu(g, u):
    return g * jax.nn.sigmoid(g) * u


def _mm_stream_kernel(*refs, n_w, n_x, nj, mi, kc):
    x_refs, w_refs = refs[:n_x], refs[n_x:n_x + n_w]
    o_refs, wb_refs = refs[n_x + n_w:2 * n_x + n_w], refs[2 * n_x + n_w:]
    j, i = pl.program_id(0), pl.program_id(1)
    slot = j % 2

    @pl.when((j < nj) & (i < mi))
    def _():
        r0 = pl.multiple_of(i * kc, 16)
        for w_ref, wb_ref in zip(w_refs, wb_refs):
            wb_ref[slot, pl.ds(r0, kc), :] = w_ref[...].astype(BF16)

    def product(x_ref, o_ref):
        x = x_ref[...]
        accs = [jnp.dot(x, wb_ref[1 - slot], preferred_element_type=F32) for wb_ref in wb_refs]
        out = _swiglu(*accs) if n_w == 2 else accs[0]
        o_ref[...] = out.astype(o_ref.dtype)

    @pl.when((j > 0) & (i < mi))
    def _():
        product(x_refs[0], o_refs[0])

    if n_x == 2:
        @pl.when((j > 0) & (i == mi))
        def _():
            product(x_refs[1], o_refs[1])


def matmul_stream(x, ws, layer, *, bm, bn, out_dtype, name, x2=None, n=None):
    m, k = x.shape
    n = ws[0].shape[-1] if n is None else n
    mi, nj = m // bm, n // bn
    kc = k // mi
    assert m % bm == 0 and n % bn == 0 and k % mi == 0 and kc % 16 == 0 and mi > 1, (m, bm, n, bn, k)
    n_x = 1 if x2 is None else 2
    row = lambda j, i: jnp.where(j == 0, 0, jnp.minimum(i, mi - 1))
    col = lambda j: jnp.maximum(j - 1, 0)
    w_spec = pl.BlockSpec(
        (None, kc, bn),
        lambda j, i: (layer, jnp.where(j < nj, jnp.minimum(i, mi - 1), mi - 1), jnp.minimum(j, nj - 1)))
    in_specs = [pl.BlockSpec((bm, k), lambda j, i: (row(j, i), 0))]
    out_specs = [pl.BlockSpec((bm, bn), lambda j, i: (row(j, i), col(j)))]
    out_shape = [jax.ShapeDtypeStruct((m, n), out_dtype)]
    args = [x]
    if x2 is not None:
        m2 = x2.shape[0]
        assert m2 <= bm and x2.shape[1] == k
        in_specs.append(pl.BlockSpec((m2, k), lambda j, i: (0, 0)))
        out_specs.append(pl.BlockSpec((m2, bn), lambda j, i: (0, col(j))))
        out_shape.append(jax.ShapeDtypeStruct((m2, n), out_dtype))
        args.append(x2)
    outs = pl.pallas_call(
        functools.partial(_mm_stream_kernel, n_w=len(ws), n_x=n_x, nj=nj, mi=mi, kc=kc),
        grid=(nj + 1, mi + n_x - 1),
        in_specs=in_specs + [w_spec] * len(ws),
        out_specs=out_specs,
        out_shape=out_shape,
        scratch_shapes=[pltpu.VMEM((2, k, bn), BF16) for _ in ws],
        compiler_params=_cparams(("arbitrary", "arbitrary")),
        name=name,
    )(*args, *ws)
    return outs[0] if x2 is None else (outs[0], outs[1])


def _swiglu_kernel(x_ref, wg_ref, wu_ref, o_ref):
    x = x_ref[...]
    g = jnp.dot(x, wg_ref[...].astype(BF16), preferred_element_type=F32)
    u = jnp.dot(x, wu_ref[...].astype(BF16), preferred_element_type=F32)
    o_ref[...] = _swiglu(g, u).astype(o_ref.dtype)


def swiglu_single(x, wg, wu, layer, *, bn, name):
    m, k = x.shape
    n = wg.shape[-1]
    assert n % bn == 0
    w_spec = pl.BlockSpec((None, k, bn), lambda j: (layer, 0, j))
    return pl.pallas_call(
        _swiglu_kernel,
        grid=(n // bn,),
        in_specs=[pl.BlockSpec((m, k), lambda j: (0, 0)), w_spec, w_spec],
        out_specs=pl.BlockSpec((m, bn), lambda j: (0, j)),
        out_shape=jax.ShapeDtypeStruct((m, n), BF16),
        compiler_params=_cparams(("parallel",)),
        name=name,
    )(x, wg, wu)


def dense2(x, x2, w, layer, *, out_dtype, name, bm=1024, bn=1024, n=None):
    if x.shape[0] <= bm:
        ncol = w.shape[-1] if n is None else n
        one = lambda a, nm: matmul(a, w, bm=a.shape[0], bn=256, out_dtype=out_dtype, layer=layer, name=nm,
                                   n=ncol)
        return one(x, name), (None if x2 is None else one(x2, name + "_ctx"))
    y = matmul_stream(x, (w,), layer, bm=bm, bn=bn, out_dtype=out_dtype, name=name, x2=x2, n=n)
    return (y, None) if x2 is None else y


def dense(x, w, layer, *, out_dtype, name, bm=1024, bn=1024):
    m = x.shape[0]
    if m > bm:
        return matmul_stream(x, (w,), layer, bm=bm, bn=bn, out_dtype=out_dtype, name=name)
    return matmul(x, w, bm=m, bn=256, out_dtype=out_dtype, layer=layer, name=name)


def _rms(x):
    return x * lax.rsqrt(jnp.mean(x * x, axis=-1, keepdims=True) + EPS)


def _resid_norm_kernel(*refs, has_y, want_a):
    it = iter(refs)
    h_ref = next(it)
    if has_y:
        y_ref, gate_ref, g1_ref = next(it), next(it), next(it)
    if want_a:
        g2_ref, shift_ref, scale_ref = next(it), next(it), next(it)
    if has_y:
        hout_ref = next(it)
    if want_a:
        a_ref = next(it)
    h = h_ref[0]
    if has_y:
        h = h + gate_ref[0] * (_rms(y_ref[0].astype(F32)) * g1_ref[...])
        hout_ref[0] = h
    if want_a:
        a = (_rms(h) * g2_ref[...]) * (1.0 + scale_ref[0]) + shift_ref[0]
        a_ref[0] = a.astype(a_ref.dtype)


def resid_norm(h, y=None, gate=None, g1=None, g2=None, shift=None, scale=None, *, bt=256):
    b, t, d = h.shape
    has_y, want_a = y is not None, g2 is not None
    bt = min(bt, t)
    tok = pl.BlockSpec((1, bt, d), lambda i, j: (i, j, 0))
    per_b = pl.BlockSpec((1, 1, d), lambda i, j: (i, 0, 0))
    vec = pl.BlockSpec((1, d), lambda i, j: (0, 0))
    args, in_specs, out_shape, out_specs = [h], [tok], [], []
    if has_y:
        args += [y, gate, g1.reshape(1, d)]
        in_specs += [tok, per_b, vec]
        out_shape.append(jax.ShapeDtypeStruct((b, t, d), F32))
        out_specs.append(tok)
    if want_a:
        args += [g2.reshape(1, d), shift, scale]
        in_specs += [vec, per_b, per_b]
        out_shape.append(jax.ShapeDtypeStruct((b, t, d), BF16))
        out_specs.append(tok)
    outs = pl.pallas_call(
        functools.partial(_resid_norm_kernel, has_y=has_y, want_a=want_a),
        grid=(b, t // bt),
        in_specs=in_specs,
        out_specs=out_specs,
        out_shape=out_shape,
        compiler_params=_cparams(("parallel", "parallel")),
        name="resid_norm",
    )(*args)
    outs = list(outs)
    h_new = outs.pop(0) if has_y else h
    a = outs.pop(0) if want_a else None
    return h_new, a


def _softplus(x):
    return jnp.maximum(x, 0.0) + jnp.log1p(jnp.exp(-jnp.abs(x)))


def _log_sigmoid(x):
    return jnp.minimum(x, 0.0) - jnp.log1p(jnp.exp(-jnp.abs(x)))


def _chunk_index(c, nc, reverse):
    return (nc - 1 - c) if reverse else c


def _mlstm_kernel(*refs):
    ins, outs = refs[:14], refs[14:]

    @pl.when(pl.program_id(2) == 0)
    def _():
        for d in range(2):
            for s in range(3):
                outs[4 * d + 1 + s][...] = ins[7 * d + 4 + s][...]

    gates = [_mlstm_gate_forms(ins[7 * d + 3], reverse=d == 1) for d in range(2)]
    for hh in range(ML_HEADS_PER_STEP):
        for d in range(2):
            _mlstm_chunk(*ins[7 * d:7 * d + 3], gates[d], *outs[4 * d:4 * d + 4], hh=hh, reverse=d == 1)


def _mlstm_gate_forms(gr_ref, *, reverse):
    l = ML_CHUNK
    nr = 2 * ML_HEADS_PER_STEP
    row = lax.broadcasted_iota(jnp.int32, (l, l), 0)
    col = lax.broadcasted_iota(jnp.int32, (l, l), 1)
    mask_t = (col <= row) if reverse else (col >= row)
    is_forget = (lax.broadcasted_iota(jnp.int32, (nr, l), 0) & 1) == 1
    gcap = GATE_CAP * jnp.tanh(gr_ref[0, 0, 0] / GATE_CAP)
    rows = jnp.where(is_forget, _log_sigmoid(gcap), gcap)
    cums = _dot3((((1,), (0,)), ((), ())), _split3(rows), mask_t.astype(BF16))
    mixed = jnp.where(is_forget, cums, gcap)
    bc = _dot3((((0,), (0,)), ((), ())), _split3(mixed), _head_selector(nr, l))
    return gcap, cums, bc


def _mlstm_chunk(q_ref, k_ref, v_ref, gates, y_ref, c_ref, n_ref, m_ref, *, hh, reverse):
    qsl = slice(hh * ML_QK_DIM, (hh + 1) * ML_QK_DIM)
    vsl = slice(hh * ML_V_DIM, (hh + 1) * ML_V_DIM)
    l = ML_CHUNK
    row = lax.broadcasted_iota(jnp.int32, (l, l), 0)
    col = lax.broadcasted_iota(jnp.int32, (l, l), 1)
    mask = (col >= row) if reverse else (col <= row)

    gcap, cums, bc = gates
    ig_r = gcap[2 * hh:2 * hh + 1, :]
    b_r = cums[2 * hh + 1:2 * hh + 2, :]
    end_col = 0 if reverse else l - 1
    total = b_r[:, end_col:end_col + 1]
    ig_bc = bc[:, 2 * hh * l:(2 * hh + 1) * l]
    b_bc = bc[:, (2 * hh + 1) * l:(2 * hh + 2) * l]
    ig_c, b_c = ig_bc[:, 0:1], b_bc[:, 0:1]

    m_prev = m_ref[0, hh]
    logw = jnp.where(mask, b_bc - b_r + ig_r, -jnp.inf)
    inter = b_c + m_prev
    m_t = jnp.maximum(jnp.max(logw, axis=1, keepdims=True), inter)

    q = q_ref[0, :, qsl]
    k = k_ref[0, :, qsl]
    v = v_ref[0, :, vsl]
    scale = ML_QK_DIM ** -0.5
    qk = lax.dot_general(q, k, (((1,), (1,)), ((), ())), preferred_element_type=F32)
    s = qk * scale * jnp.exp(logw - m_t)
    inter_w = jnp.exp(inter - m_t) * scale

    c_st = c_ref[0, hh]
    n_st = n_ref[0, hh]
    num = (jnp.dot(s.astype(BF16), v, preferred_element_type=F32)
           + inter_w * jnp.dot(q, c_st.astype(BF16), preferred_element_type=F32))
    qn = jnp.sum(q.astype(F32) * n_st, axis=1, keepdims=True)
    den = jnp.sum(s, axis=1, keepdims=True) + inter_w * qn
    y_ref[0, :, vsl] = (num * (1.0 / jnp.maximum(jnp.abs(den), jnp.exp(-m_t)))).astype(y_ref.dtype)

    g_c = total - b_c + ig_c
    g_r = total - b_r + ig_r
    m_new = jnp.maximum(total + m_prev, jnp.max(g_r, axis=1, keepdims=True))
    w_c = jnp.exp(g_c - m_new)
    decay = jnp.exp(total + m_prev - m_new)
    kw = k.astype(F32) * w_c
    kv = lax.dot_general(kw.astype(BF16), v, (((0,), (0,)), ((), ())), preferred_element_type=F32)
    c_ref[0, hh] = decay * c_st + kv
    n_ref[0, hh] = decay * n_st + jnp.sum(kw, axis=0, keepdims=True)
    m_ref[0, hh] = m_new


def mlstm_scan(proj, gates_row, states, *, name):
    b, t, _ = proj.shape
    l = ML_CHUNK
    nc = t // l
    hp = ML_HEADS_PER_STEP
    gates_step = gates_row.reshape(2, b, ML_HEADS // hp, 2 * hp, t)
    qw, vw = hp * ML_QK_DIM, hp * ML_V_DIM
    qb, kb, vb = E_Q // qw, E_K // qw, E_V // vw
    st_specs = [pl.BlockSpec((1, hp, ML_QK_DIM, ML_V_DIM), lambda i, h, c: (i, h, 0, 0)),
                pl.BlockSpec((1, hp, 1, ML_QK_DIM), lambda i, h, c: (i, h, 0, 0)),
                pl.BlockSpec((1, hp, 1, 1), lambda i, h, c: (i, h, 0, 0))]
    in_specs, args, out_specs, out_shape = [], [], [], []
    for d in range(2):
        ci = functools.partial(_chunk_index, nc=nc, reverse=d == 1)
        in_specs += [
            pl.BlockSpec((1, l, qw), lambda i, h, c, ci=ci: (i, ci(c), qb + h)),
            pl.BlockSpec((1, l, qw), lambda i, h, c, ci=ci: (i, ci(c), kb + h)),
            pl.BlockSpec((1, l, vw), lambda i, h, c, ci=ci: (i, ci(c), vb + h)),
            pl.BlockSpec((1, 1, 1, 2 * hp, l), lambda i, h, c, ci=ci, d=d: (d, i, h, 0, ci(c))),
        ] + st_specs
        args += [proj, proj, proj, gates_step, *states[d]]
        out_specs += [pl.BlockSpec((1, l, vw), lambda i, h, c, ci=ci: (i, ci(c), h))] + st_specs
        out_shape += [jax.ShapeDtypeStruct((b, t, ML_V), BF16)]
        out_shape += [jax.ShapeDtypeStruct(s.shape, F32) for s in states[d]]
    outs = pl.pallas_call(
        _mlstm_kernel,
        grid=(b, ML_HEADS // hp, nc),
        in_specs=in_specs,
        out_specs=out_specs,
        out_shape=out_shape,
        compiler_params=_cparams(("parallel", "parallel", "arbitrary")),
        name=name,
    )(*args)
    return (outs[0], outs[4]), (tuple(outs[1:4]), tuple(outs[5:8]))


CONV_ROWS = 256
CONV_HALO = 16


def _conv_kernel(x_ref, w_ref, b_ref, o_ref, *, t):
    w = w_ref[...]
    bias = b_ref[...]
    r = min(CONV_ROWS, t)
    half = SSM_CONV // 2
    cw = x_ref.shape[-1]
    for r0 in range(0, t, r):
        parts = []
        if r0 == 0:
            parts.append(jnp.zeros((CONV_HALO, cw), F32))
        else:
            parts.append(x_ref[0, pl.ds(r0 - CONV_HALO, CONV_HALO), :].astype(F32))
        parts.append(x_ref[0, pl.ds(r0, r), :].astype(F32))
        if r0 + r == t:
            parts.append(jnp.zeros((CONV_HALO, cw), F32))
        else:
            parts.append(x_ref[0, pl.ds(r0 + r, CONV_HALO), :].astype(F32))
        win = jnp.concatenate(parts, axis=0)
        n = r + 2 * CONV_HALO
        acc = jnp.zeros((r, cw), F32) + bias
        for kk in range(SSM_CONV):
            sh = (half - kk) % n
            rolled = win if sh == 0 else pltpu.roll(win, sh, 0)
            acc = acc + w[kk:kk + 1, :] * rolled[CONV_HALO:CONV_HALO + r, :]
        o_ref[0, pl.ds(r0, r), :] = (acc * jax.nn.sigmoid(acc)).astype(o_ref.dtype)


def conv_silu(proj, conv_w, conv_b, *, bc=512):
    b, t, _ = proj.shape
    c = SSM_CONV_DIM
    off = E_XBC // bc
    return pl.pallas_call(
        functools.partial(_conv_kernel, t=t),
        grid=(b, c // bc),
        in_specs=[pl.BlockSpec((1, t, bc), lambda i, j: (i, 0, off + j)),
                  pl.BlockSpec((SSM_CONV, bc), lambda i, j: (0, j)),
                  pl.BlockSpec((1, bc), lambda i, j: (0, j))],
        out_specs=pl.BlockSpec((1, t, bc), lambda i, j: (i, 0, j)),
        out_shape=jax.ShapeDtypeStruct((b, t, c), BF16),
        compiler_params=_cparams(("parallel", "parallel")),
        name="conv_silu",
    )(proj, conv_w, conv_b.reshape(1, c))


def _split3(x):
    hi = x.astype(BF16)
    r1 = x - hi.astype(F32)
    mid = r1.astype(BF16)
    lo = (r1 - mid.astype(F32)).astype(BF16)
    return hi, mid, lo


def _dot3(dims, parts_lhs, rhs):
    return sum(lax.dot_general(p, rhs, dims, preferred_element_type=F32) for p in parts_lhs)


def _head_selector(heads, width):
    rows = lax.broadcasted_iota(jnp.int32, (heads, heads * width), 0)
    lanes = lax.broadcasted_iota(jnp.int32, (heads, heads * width), 1)
    return (lanes // width == rows).astype(BF16)


def _ssd_kernel(*refs):
    ins, outs = refs[:12], refs[12:]

    @pl.when(pl.program_id(2) == 0)
    def _():
        for d in range(2):
            outs[2 * d + 1][...] = ins[6 * d + 5][...]

    for d in range(2):
        _ssd_chunk(*ins[6 * d:6 * d + 5], *outs[2 * d:2 * d + 2], reverse=d == 1)


def _ssd_chunk(xs_ref, bm_ref, cm_ref, dtr_ref, pr_ref, y_ref, h_ref, *, reverse):
    l = SSM_CHUNK
    gp = SSD_GROUPS_PER_STEP
    nh = gp * SSM_HPG
    row = lax.broadcasted_iota(jnp.int32, (l, l), 0)
    col = lax.broadcasted_iota(jnp.int32, (l, l), 1)
    mask = (col >= row) if reverse else (col <= row)
    mask_t = (col <= row) if reverse else (col >= row)

    nn = (((1,), (0,)), ((), ()))
    nt = (((1,), (1,)), ((), ()))
    tn = (((0,), (0,)), ((), ()))
    pr = pr_ref[0, 0]
    dt_r = _softplus(dtr_ref[0, 0, 0] + pr[:, 0:1])
    a_r = dt_r * pr[:, 1:2]
    cum_r = _dot3(nn, _split3(a_r), mask_t.astype(BF16))
    dt_parts, cum_parts = _split3(dt_r), _split3(cum_r)
    sel_ch = _head_selector(nh, SSM_HEAD_DIM)
    dt_ch = _dot3(tn, dt_parts, sel_ch)
    cum_ch = _dot3(tn, cum_parts, sel_ch)
    cum_bc = _dot3(tn, cum_parts, _head_selector(nh, l))
    end_row = 0 if reverse else l - 1
    total_ch = cum_ch[end_row:end_row + 1, :]

    xdt = xs_ref[0].astype(F32) * dt_ch
    e_cum = jnp.exp(cum_ch)
    xw = (xdt * jnp.exp(total_ch - cum_ch)).astype(BF16)
    s_decay = jnp.exp(total_ch)
    lane_head = lax.broadcasted_iota(jnp.int32, (l, SSM_GROUP_W), 1) // SSM_HEAD_DIM
    for g in range(gp):
        xsl = slice(g * SSM_GROUP_W, (g + 1) * SSM_GROUP_W)
        nsl = slice(g * SSM_STATE, (g + 1) * SSM_STATE)
        bm = bm_ref[0, :, nsl]
        cm = cm_ref[0, :, nsl]
        hst = h_ref[0, g]
        cb = lax.dot_general(cm, bm, nt, preferred_element_type=F32)
        inter = jnp.dot(cm, hst.astype(BF16), preferred_element_type=F32)
        xdt_g = xdt[:, xsl]
        ws, xjs = [], []
        for j in range(SSM_HPG):
            hd = g * SSM_HPG + j
            decay = jnp.exp(jnp.where(mask, cum_bc[:, hd * l:(hd + 1) * l] - cum_r[hd:hd + 1, :], -jnp.inf))
            ws.append((decay * cb).astype(BF16))
            xjs.append(jnp.where(lane_head == j, xdt_g, 0.0).astype(BF16))
        y = jnp.dot(jnp.concatenate(ws, axis=1), jnp.concatenate(xjs, axis=0), preferred_element_type=F32)
        y_ref[0, :, xsl] = (y + e_cum[:, xsl] * inter).astype(y_ref.dtype)
        upd = lax.dot_general(bm, xw[:, xsl], tn, preferred_element_type=F32)
        h_ref[0, g] = s_decay[:, xsl] * hst + upd


def ssd_scan(xbc, dt_row, par_row, states, *, name):
    b, t, _ = xbc.shape
    l = SSM_CHUNK
    nc = t // l
    gp = SSD_GROUPS_PER_STEP
    xw, nw = gp * SSM_GROUP_W, gp * SSM_STATE
    bmb = SSM_INNER // nw
    cmb = bmb + SSM_GROUPS // gp
    st_spec = pl.BlockSpec((1, gp, SSM_STATE, SSM_GROUP_W), lambda i, g, c: (i, g, 0, 0))
    dt_step = dt_row.reshape(2, b, SSM_GROUPS // gp, gp * SSM_HPG, t)
    par_step = par_row.reshape(2, SSM_GROUPS // gp, gp * SSM_HPG, 2)
    in_specs, args, out_specs, out_shape = [], [], [], []
    for d in range(2):
        ci = functools.partial(_chunk_index, nc=nc, reverse=d == 1)
        in_specs += [
            pl.BlockSpec((1, l, xw), lambda i, g, c, ci=ci: (i, ci(c), g)),
            pl.BlockSpec((1, l, nw), lambda i, g, c, ci=ci: (i, ci(c), bmb + g)),
            pl.BlockSpec((1, l, nw), lambda i, g, c, ci=ci: (i, ci(c), cmb + g)),
            pl.BlockSpec((1, 1, 1, gp * SSM_HPG, l), lambda i, g, c, ci=ci, d=d: (d, i, g, 0, ci(c))),
            pl.BlockSpec((1, 1, gp * SSM_HPG, 2), lambda i, g, c, d=d: (d, g, 0, 0)),
            st_spec,
        ]
        args += [xbc, xbc, xbc, dt_step, par_step, states[d]]
        out_specs += [pl.BlockSpec((1, l, xw), lambda i, g, c, ci=ci: (i, ci(c), g)), st_spec]
        out_shape += [jax.ShapeDtypeStruct((b, t, SSM_INNER), BF16),
                      jax.ShapeDtypeStruct(states[d].shape, F32)]
    yf, hf, yb, hb = pl.pallas_call(
        _ssd_kernel,
        grid=(b, SSM_GROUPS // gp, nc),
        in_specs=in_specs,
        out_specs=out_specs,
        out_shape=out_shape,
        compiler_params=_cparams(("parallel", "parallel", "arbitrary")),
        name=name,
    )(*args)
    return (yf, yb), (hf, hb)


def _even_out_kernel(mlf_ref, mlb_ref, ssdf_ref, ssdb_ref, o_ref, z_ref, xs_ref, mlg_ref, dsk_ref, ssg_ref,
                     out_ref):
    for h in range(ML_HEADS):
        sl = slice(h * ML_V_DIM, (h + 1) * ML_V_DIM)
        y = mlf_ref[0, :, sl].astype(F32) + mlb_ref[0, :, sl].astype(F32)
        og = o_ref[0, :, sl].astype(F32)
        out_ref[0, :, sl] = (_rms(y) * mlg_ref[:, sl] * jax.nn.sigmoid(og)).astype(out_ref.dtype)
    for g in range(SSM_GROUPS):
        sl = slice(g * SSM_GROUP_W, (g + 1) * SSM_GROUP_W)
        z = z_ref[0, :, sl].astype(F32)
        y = ((ssdf_ref[0, :, sl].astype(F32) + ssdb_ref[0, :, sl].astype(F32))
             + dsk_ref[:, sl] * xs_ref[0, :, sl].astype(F32))
        y = y * (z * jax.nn.sigmoid(z))
        so = slice(ML_V + g * SSM_GROUP_W, ML_V + (g + 1) * SSM_GROUP_W)
        out_ref[0, :, so] = (_rms(y) * ssg_ref[:, sl]).astype(out_ref.dtype)


def even_out(ml_y, ssd_y, proj_a, proj_b, xbc, ml_norm_g, d_skip_full, ssm_norm_g, *, bt=256):
    b, t, _ = ml_y[0].shape
    bt = min(bt, t)
    w = ML_V
    tok = lambda blk: pl.BlockSpec((1, bt, w), lambda i, j: (i, j, blk))
    vec = pl.BlockSpec((1, w), lambda i, j: (0, 0))
    return pl.pallas_call(
        _even_out_kernel,
        grid=(b, t // bt),
        in_specs=[tok(0), tok(0), tok(0), tok(0), tok(E_O // w), tok(E_Z // w), tok(0), vec, vec, vec],
        out_specs=pl.BlockSpec((1, bt, 2 * w), lambda i, j: (i, j, 0)),
        out_shape=jax.ShapeDtypeStruct((b, t, 2 * w), BF16),
        compiler_params=_cparams(("parallel", "parallel")),
        name="even_gate",
    )(ml_y[0], ml_y[1], ssd_y[0], ssd_y[1], proj_a, proj_b, xbc, ml_norm_g.reshape(1, w), d_skip_full.reshape(1, w),
      ssm_norm_g.reshape(1, w))


def _hg_constants(reverse):
    l = HG_CHUNK
    pos = np.arange(l)
    tri = (pos[None, :] <= pos[:, None]).astype(np.float32)
    masks = []
    for lv in range(HG_LEVELS):
        m = 1 << lv
        blk = pos // (2 * m)
        later = (pos % (2 * m)) >= m
        masks.append(((blk[:, None] == blk[None, :]) & later[:, None] & (~later)[None, :]).astype(np.float32))
    masks = np.stack(masks)
    if reverse:
        tri = tri[::-1, ::-1]
        masks = masks[:, ::-1, ::-1]
    return (jnp.asarray(tri, BF16), jnp.asarray(masks.reshape(-1, l), F32))


def _hg_level_factors(q, k, f, cum, reverse):
    l, w = q.shape
    row = lax.broadcasted_iota(jnp.int32, (l, w), 0)
    sub = lax.broadcasted_iota(jnp.int32, (8, w), 0)
    zs = []
    for lv in range(HG_LEVELS):
        m = 1 << lv
        late = ((row & m) == 0) if reverse else ((row & m) != 0)
        if m >= 8:
            pieces = []
            for b0 in range(0, l, 2 * m):
                lo_rows, hi_rows = slice(b0, b0 + m), slice(b0 + m, b0 + 2 * m)
                if reverse:
                    mid = cum[b0 + m:b0 + m + 1]
                    pieces.append(q[lo_rows] * jnp.exp(cum[lo_rows] - mid))
                    pieces.append(k[hi_rows] * jnp.exp(mid - cum[hi_rows]))
                else:
                    mid = cum[b0 + m - 1:b0 + m]
                    pieces.append(k[lo_rows] * jnp.exp(mid - cum[lo_rows]))
                    pieces.append(q[hi_rows] * jnp.exp(cum[hi_rows] - mid))
            zs.append(jnp.concatenate(pieces, axis=0).astype(BF16))
            continue
        if m == 1:
            ex = jnp.where(late, f, 1.0)
        else:
            pieces = []
            if m >= 4:
                for b0 in range(0, l, 2 * m):
                    r = b0 + (m if reverse else m - 1)
                    pieces.append(cum[b0:b0 + 2 * m] - cum[r:r + 1])
            else:
                for b0 in range(0, l, 8):
                    ra, rb = (b0 + 2, b0 + 6) if reverse else (b0 + 1, b0 + 5)
                    mid = jnp.where(sub < 4, cum[ra:ra + 1], cum[rb:rb + 1])
                    pieces.append(cum[b0:b0 + 8] - mid)
            dd = pieces[0] if len(pieces) == 1 else jnp.concatenate(pieces, axis=0)
            ex = jnp.exp(jnp.where(late, dd, -dd))
        zs.append((jnp.where(late, q, k) * ex).astype(BF16))
    return zs


def _hg_chunk(q_ref, f_ref, i_ref, lb, tri, masks_ref, y_ref, s_ref, r0, reverse):
    l, e = HG_CHUNK, HG_EXPAND
    q_raw = q_ref[0, pl.ds(r0, l), :].astype(F32)
    f_raw = f_ref[0, pl.ds(r0, l), :].astype(F32)
    v = i_ref[0, pl.ds(r0, l), :]
    f = lb + (1.0 - lb) * jax.nn.sigmoid(f_raw)
    k = 1.0 - f
    lf = jnp.log(f)
    q = q_raw * jax.nn.sigmoid(q_raw) * (e ** -0.5)
    hi, mid, lo = _split3(lf)
    cum = (jnp.dot(tri, hi, preferred_element_type=F32)
           + jnp.dot(tri, mid, preferred_element_type=F32)
           + jnp.dot(tri, lo, preferred_element_type=F32))
    zs = _hg_level_factors(q, k, f, cum, reverse)
    end_row = 0 if reverse else l - 1
    cum_end = cum[end_row:end_row + 1, :]
    qc = (q * jnp.exp(cum)).astype(BF16)
    kt_end = (k * jnp.exp(cum_end - cum)).astype(BF16)
    s_decay = jnp.exp(cum_end)
    qk = q * k
    row = lax.broadcasted_iota(jnp.int32, (l, l), 0)
    col = lax.broadcasted_iota(jnp.int32, (l, l), 1)
    eye = row == col
    nt = (((1,), (1,)), ((), ()))
    tn = (((0,), (0,)), ((), ()))
    for hh in range(HG_HEADS_PER_STEP):
        sl = slice(hh * e, (hh + 1) * e)
        att = jnp.where(eye, jnp.sum(qk[:, sl], axis=1, keepdims=True), 0.0)
        for lv in range(HG_LEVELS):
            z = zs[lv][:, sl]
            a_lv = lax.dot_general(z, z, nt, preferred_element_type=F32)
            att = att + masks_ref[pl.ds(lv * l, l), :] * a_lv
        st = s_ref[0, hh]
        vh = v[:, sl]
        o = (jnp.dot(att.astype(BF16), vh, preferred_element_type=F32)
             + lax.dot_general(qc[:, sl], st.astype(BF16), nt, preferred_element_type=F32))
        y_ref[0, pl.ds(r0, l), sl] = o.astype(y_ref.dtype)
        upd = lax.dot_general(vh, kt_end[:, sl], tn, preferred_element_type=F32)
        s_ref[0, hh] = s_decay[:, sl] * st + upd


def _hg_kernel(qf_ref, ff_ref, if_ref, qb_ref, fb_ref, ib_ref, lbf_ref, lbb_ref, trif_ref, trib_ref,
               mf_ref, mb_ref, sf0_ref, sb0_ref, yf_ref, yb_ref, sf_ref, sb_ref, *, n_inner):
    l = HG_CHUNK

    @pl.when(pl.program_id(2) == 0)
    def _():
        sf_ref[...] = sf0_ref[...]
        sb_ref[...] = sb0_ref[...]

    lbf = lbf_ref[0]
    lbb = lbb_ref[0]
    trif = trif_ref[...]
    trib = trib_ref[...]

    def chunk(ci, carry):
        rf = pl.multiple_of(ci * l, l)
        rb = pl.multiple_of((n_inner - 1 - ci) * l, l)
        _hg_chunk(qf_ref, ff_ref, if_ref, lbf, trif, mf_ref, yf_ref, sf_ref, rf, False)
        _hg_chunk(qb_ref, fb_ref, ib_ref, lbb, trib, mb_ref, yb_ref, sb_ref, rb, True)
        return carry

    lax.fori_loop(0, n_inner, chunk, 0, unroll=4)


def hgrn2_scan(proj, lb, states, *, tb=512, name):
    b, t, _ = proj.shape
    tb = min(tb, t)
    nb = t // tb
    n_inner = tb // HG_CHUNK
    hw = HG_HEADS_PER_STEP * HG_EXPAND
    nh = HG_HEADS // HG_HEADS_PER_STEP
    qb_, ffb, fbb, ib_ = O_Q // hw, O_F // hw, (O_F + HG_F) // hw, O_I // hw
    trif, mf = _hg_constants(False)
    trib, mb = _hg_constants(True)
    fwd = lambda blk: pl.BlockSpec((1, tb, hw), lambda i, h, c: (i, c, blk + h))
    bwd = lambda blk: pl.BlockSpec((1, tb, hw), lambda i, h, c: (i, nb - 1 - c, blk + h))
    const = lambda a: pl.BlockSpec(a.shape, lambda i, h, c: (0,) * a.ndim)
    st_spec = pl.BlockSpec((1, HG_HEADS_PER_STEP, HG_V_DIM, HG_EXPAND), lambda i, h, c: (i, h, 0, 0))
    in_specs = [fwd(qb_), fwd(ffb), fwd(ib_), bwd(qb_), bwd(fbb), bwd(ib_),
                pl.BlockSpec((1, 1, hw), lambda i, h, c: (0, 0, h)),
                pl.BlockSpec((1, 1, hw), lambda i, h, c: (1, 0, h)),
                const(trif), const(trib), const(mf), const(mb), st_spec, st_spec]
    yf, yb, sf, sb = pl.pallas_call(
        functools.partial(_hg_kernel, n_inner=n_inner),
        grid=(b, nh, nb),
        in_specs=in_specs,
        out_specs=[fwd(0), bwd(0), st_spec, st_spec],
        out_shape=[jax.ShapeDtypeStruct((b, t, D_MODEL), BF16),
                   jax.ShapeDtypeStruct((b, t, D_MODEL), BF16),
                   jax.ShapeDtypeStruct(states[0].shape, F32),
                   jax.ShapeDtypeStruct(states[1].shape, F32)],
        compiler_params=_cparams(("parallel", "parallel", "arbitrary")),
        name=name,
    )(proj, proj, proj, proj, proj, proj, lb, lb, trif, trib, mf, mb, states[0], states[1])
    return (yf, yb), (sf, sb)


def _hg_out_kernel(of_ref, ob_ref, g_ref, ng_ref, out_ref):
    for h in range(of_ref.shape[-1] // HG_V_DIM):
        sl = slice(h * HG_V_DIM, (h + 1) * HG_V_DIM)
        g = g_ref[0, :, sl].astype(F32)
        o = of_ref[0, :, sl].astype(F32) + ob_ref[0, :, sl].astype(F32)
        out_ref[0, :, sl] = (_rms(o) * ng_ref[:, sl] * (g * jax.nn.sigmoid(g))).astype(out_ref.dtype)


def hg_out(o_f, o_b, proj, norm_g, *, bt=512, bc=1024, name):
    b, t, d = o_f.shape
    bt = min(bt, t)
    gb = O_G // bc
    buf = pl.Buffered(3)
    tok = pl.BlockSpec((1, bt, bc), lambda i, j, c: (i, j, c))
    in_specs = [pl.BlockSpec((1, bt, bc), lambda i, j, c: (i, j, c), pipeline_mode=buf),
                pl.BlockSpec((1, bt, bc), lambda i, j, c: (i, j, c), pipeline_mode=buf),
                pl.BlockSpec((1, bt, bc), lambda i, j, c: (i, j, gb + c), pipeline_mode=buf),
                pl.BlockSpec((1, bc), lambda i, j, c: (0, c))]

    def outer(*hbm_refs):
        pltpu.emit_pipeline(_hg_out_kernel, grid=(b, t // bt, d // bc), in_specs=in_specs,
                            out_specs=[tok])(*hbm_refs)

    any_spec = pl.BlockSpec(memory_space=pl.ANY)
    return pl.pallas_call(
        outer,
        in_specs=[any_spec] * 4,
        out_specs=any_spec,
        out_shape=jax.ShapeDtypeStruct((b, t, d), BF16),
        compiler_params=pltpu.CompilerParams(vmem_limit_bytes=VMEM_LIMIT),
        name=name,
    )(o_f, o_b, proj, norm_g.reshape(1, d))


def even_mixer(a_c, a_l, w_main, w_small, w_out, layer, ml_gate_b, ml_norm_g, conv_w, conv_b,
               dt_bias, a_log, d_skip, ssm_norm_g, need_ctx):
    bsz = a_l.shape[0]

    def features(a, tag):
        b, t, d = a.shape
        bm = min(1024, b * t)
        a2 = a.reshape(b * t, d)
        proj = matmul(a2, w_main, bm=bm, bn=1024, out_dtype=BF16, name="ab_in_" + tag).reshape(b, t, E_MAIN)
        proj_a = proj
        small = matmul(a2, w_small, bm=bm, bn=128, out_dtype=F32, name="ab_in_small_" + tag).reshape(b, t, 128)
        gates = small[..., :16].reshape(b, t, 2, 2, ML_HEADS) + ml_gate_b.astype(F32)
        g_row = gates.transpose(2, 0, 4, 3, 1)
        dt = small[..., 16:16 + 2 * SSM_HEADS].reshape(b, t, 2, SSM_GROUPS, SSM_HPG)
        dt_row = dt.transpose(2, 0, 3, 4, 1)
        xbc = conv_silu(proj, conv_w, conv_b)
        return dict(proj_a=proj_a, proj=proj, xbc=xbc, g_row=g_row, dt_row=dt_row)

    fc, fl = features(a_c, "ctx"), features(a_l, "lat")
    neg_a = -jnp.exp(a_log.astype(F32))
    par = jnp.stack([dt_bias.astype(F32), neg_a], axis=1).reshape(2, 2, SSM_GROUPS, SSM_HPG)
    par_row = par.transpose(0, 2, 3, 1)

    ml_init = (jnp.zeros((bsz, ML_HEADS, ML_QK_DIM, ML_V_DIM), F32),
               jnp.zeros((bsz, ML_HEADS, 1, ML_QK_DIM), F32),
               jnp.zeros((bsz, ML_HEADS, 1, 1), F32))
    ssd_init = jnp.zeros((bsz, SSM_GROUPS, SSM_STATE, SSM_GROUP_W), F32)

    def ml(f, st, tag):
        return mlstm_scan(f['proj_a'], f['g_row'], st, name="mlstm_scan_" + tag)

    def ssd(f, st, tag):
        return ssd_scan(f['xbc'], f['dt_row'], par_row, st, name="ssd_scan_" + tag)

    ml_c, ml_st = ml(fc, (ml_init, ml_init), "ctx")
    ml_l, _ = ml(fl, ml_st, "lat")
    ssd_c, ssd_st = ssd(fc, (ssd_init, ssd_init), "ctx")
    ssd_l, _ = ssd(fl, ssd_st, "lat")
    d_full = jnp.repeat(d_skip.astype(F32), SSM_HEAD_DIM)

    def gated(f, ml_y, ssd_y):
        b, t, _ = ml_y[0].shape
        cat = even_out(ml_y, ssd_y, f['proj_a'], f['proj'], f['xbc'], ml_norm_g, d_full, ssm_norm_g)
        return cat.reshape(b * t, D_MODEL)

    cat_c = gated(fc, ml_c, ssd_c) if need_ctx else None
    y_l, y_c = dense2(gated(fl, ml_l, ssd_l), cat_c, w_out, layer, out_dtype=BF16, name="ab_out")
    y_l = y_l.reshape(a_l.shape)
    return (y_c.reshape(a_c.shape) if need_ctx else None), y_l


def hgrn2_mixer(a_c, a_l, w_in, w_out, layer, lb, norm_g, need_ctx, lat_colmajor=True):
    bsz = a_l.shape[0]
    lb3 = lb.astype(F32).reshape(2, 1, HG_F)

    flat = lambda a: a.reshape(a.shape[0] * a.shape[1], a.shape[2])
    pl_, pc = dense2(flat(a_l), flat(a_c), w_in, layer, out_dtype=BF16, name="hg_in")
    pl_ = pl_.reshape(a_l.shape[:2] + (5 * D_MODEL,))
    pc = pc.reshape(a_c.shape[:2] + (5 * D_MODEL,))
    init = jnp.zeros((bsz, HG_HEADS, HG_V_DIM, HG_EXPAND), F32)
    o_c, st = hgrn2_scan(pc, lb3, (init, init), name="hg_scan_ctx")
    o_l, _ = hgrn2_scan(pl_, lb3, st, name="hg_scan_lat")

    def gated(p, o, colmajor, tag):
        g = hg_out(o[0], o[1], p, norm_g, name="hg_gate_" + tag)
        return flat(grid_from_colmajor(g) if colmajor else g)

    g_c = gated(pc, o_c, False, "ctx") if need_ctx else None
    y_l, y_c = dense2(gated(pl_, o_l, lat_colmajor, "lat"), g_c, w_out, layer, out_dtype=BF16, name="hg_out")
    return (y_c.reshape(a_c.shape) if need_ctx else None), y_l.reshape(a_l.shape)


def grid_to_colmajor(h):
    bsz, t, d = h.shape
    rows = t // GRID_W
    return h.reshape(bsz, rows, GRID_W, d).transpose(0, 2, 1, 3).reshape(bsz, t, d)


def grid_from_colmajor(h):
    bsz, t, d = h.shape
    rows = t // GRID_W
    return h.reshape(bsz, GRID_W, rows, d).transpose(0, 2, 1, 3).reshape(bsz, t, d)


def ffn(a_l, a_c, wg, wu, wd, layer):
    flat = lambda a: a.reshape(a.shape[0] * a.shape[1], a.shape[2])
    hid_l = matmul_stream(flat(a_l), (wg, wu), layer, bm=2048, bn=256, out_dtype=BF16, name="ffn_up_lat")
    f_l = dense(hid_l, wd, layer, out_dtype=BF16, name="ffn_down_lat", bm=512, bn=512).reshape(a_l.shape)
    if a_c is None:
        return f_l, None
    hid_c = swiglu_single(flat(a_c), wg, wu, layer, bn=256, name="ffn_up_ctx")
    f_c = dense(hid_c, wd, layer, out_dtype=BF16, name="ffn_down_ctx", bm=512, bn=512).reshape(a_c.shape)
    return f_l, f_c


def kernel(x, c, ctx, c_ctx, ada_w, ada_b, norm_g, ffn_w_gate, ffn_w_up, ffn_w_down,
           ab_w_in, ab_w_out, ml_gate_b, ml_norm_g, ssm_conv_w, ssm_conv_b, ssm_dt_bias,
           ssm_a_log, ssm_d, ssm_norm_g, hg_w_in, hg_w_out, hg_lb, hg_norm_g):
    depth = ada_w.shape[0]
    bsz = x.shape[0]
    d = D_MODEL
    lb_p = jax.nn.softmax(hg_lb.astype(F32), axis=1)
    lower_bounds = jnp.cumsum(lb_p, axis=1) - lb_p[:, :1]
    cond = jnp.concatenate([jax.nn.silu(c), jax.nn.silu(c_ctx)[None, :],
                            jnp.zeros((8 - bsz - 1, d), F32)], axis=0)
    h_lat, h_ctx = x, ctx
    y_l = y_c = f_l = f_c = None
    gate_l = gate_c = g_prev = None
    for layer in range(depth):
        need_ctx = layer < depth - 1
        j = layer // 2
        mod = matmul(cond, ada_w, bm=8, bn=512, out_dtype=F32, bias=ada_b[layer], layer=layer,
                     name="ada_mod")
        mod_l = [mod[:bsz, i * d:(i + 1) * d][:, None, :] for i in range(6)]
        mod_c = [jnp.broadcast_to(mod[bsz, i * d:(i + 1) * d][None, None, :], (bsz, 1, d)) for i in range(6)]
        g = norm_g[layer]
        h_lat, a_l = resid_norm(h_lat, f_l, gate_l, g_prev, g[0], mod_l[0], mod_l[1])
        h_ctx, a_c = resid_norm(h_ctx, f_c, gate_c, g_prev, g[0], mod_c[0], mod_c[1])
        if layer % 2 == 0:
            w_in = ab_w_in[j]
            zx0 = E_A + E_GATES
            w_main = jnp.concatenate([w_in[:, :E_A].astype(BF16), w_in[:, zx0:zx0 + E_B].astype(BF16)], axis=1)
            w_small = jnp.concatenate(
                [w_in[:, E_A:zx0], w_in[:, zx0 + E_B:],
                 jnp.zeros((d, 128 - E_GATES - 2 * SSM_HEADS), F32)], axis=1).astype(BF16)
            y_c, y_l = even_mixer(a_c, a_l, w_main, w_small, ab_w_out, j, ml_gate_b[j],
                                  ml_norm_g[j], ssm_conv_w[j], ssm_conv_b[j], ssm_dt_bias[j],
                                  ssm_a_log[j], ssm_d[j], ssm_norm_g[j], need_ctx)
        else:
            y_c, y_l = hgrn2_mixer(a_c, grid_to_colmajor(a_l), hg_w_in, hg_w_out, j,
                                   lower_bounds[:, layer], hg_norm_g[j], need_ctx)
        h_lat, a2_l = resid_norm(h_lat, y_l, mod_l[2], g[1], g[2], mod_l[3], mod_l[4])
        a2_c = None
        if need_ctx:
            h_ctx, a2_c = resid_norm(h_ctx, y_c, mod_c[2], g[1], g[2], mod_c[3], mod_c[4])
        f_l, f_c = ffn(a2_l, a2_c, ffn_w_gate, ffn_w_up, ffn_w_down, layer)
        gate_l = mod_l[5]
        gate_c = mod_c[5] if need_ctx else None
        g_prev = g[3]
    h_lat, _ = resid_norm(h_lat, f_l, gate_l, g_prev)
    return h_lat
```
